```python
import math
import jax, jax.numpy as jnp
from jax import lax
import numpy as np

D_MODEL = 4096
BATCH = 1
SEQ = 16384
DEPTH = 1
DEC_BATCH = 2
DEC_SEQ = 8192
PAST_LEN = 128

HEAD_DIM = 128
DILATION_PAIRS = ((128, 1), (512, 4), (2048, 16))
N_GROUPS = len(DILATION_PAIRS)
HEADS_PER_GROUP = 8
ATTN_WIDTH = N_GROUPS * HEADS_PER_GROUP * HEAD_DIM
ATTN_OUT = HEADS_PER_GROUP * HEAD_DIM
ROPE_THETA = 10000.0
POOL_WINDOWS = (2, 4, 8, 16)
POOL_GROUPS = len(POOL_WINDOWS)
POOL_WIDTH = D_MODEL // 4
POOL_GROUP_WIDTH = POOL_WIDTH // POOL_GROUPS
IN_COLS = 3 * ATTN_WIDTH + POOL_WIDTH + 2 * D_MODEL
N_EXPERTS = 32
TOP_K = 4
D_FF = D_MODEL
SWIGLU_LIMIT = 7.0
SWIGLU_ALPHA = 1.702
EXPERT_BLOCK = 256
RMS_EPS = 1e-6

kernel_name = "hybrid_dilated_attn_pool_moe_encoder"


def rms_norm(x, gain):
    xf = x.astype(jnp.float32)
    y = xf * lax.rsqrt(jnp.mean(xf * xf, axis=-1, keepdims=True) + RMS_EPS)
    return (y * gain.astype(jnp.float32)).astype(x.dtype)


def qk_norm_rope(t, gain):
    S = t.shape[1]
    tf = t.astype(jnp.float32)
    tf = tf * lax.rsqrt(jnp.mean(tf * tf, axis=-1, keepdims=True) + RMS_EPS)
    tf = tf * gain.astype(jnp.float32)[:, None, :]
    half = HEAD_DIM // 2
    inv_freq = ROPE_THETA ** (-jnp.arange(half, dtype=jnp.float32) / half)
    ang = jnp.arange(S, dtype=jnp.float32)[:, None] * inv_freq[None, :]
    cos = jnp.cos(ang)[None, :, None, None, :]
    sin = jnp.sin(ang)[None, :, None, None, :]
    t1, t2 = tf[..., :half], tf[..., half:]
    return jnp.concatenate([t1 * cos - t2 * sin, t2 * cos + t1 * sin], axis=-1).astype(t.dtype)


def dilated_window_attention(q, k, v, window, dilation):
    B, S, H, hd = q.shape
    side = window // (2 * dilation)
    L = S // dilation
    blk = math.gcd(L, side)
    nblk = L // blk
    win = blk + 2 * side

    def to_sub(t):
        return t.reshape(B, L, dilation, H, hd).transpose(0, 2, 1, 3, 4)

    qs, ks, vs = to_sub(q), to_sub(k), to_sub(v)
    pad = ((0, 0), (0, 0), (side, side), (0, 0), (0, 0))
    kp, vp = jnp.pad(ks, pad), jnp.pad(vs, pad)
    idx = jnp.arange(nblk)[:, None] * blk + jnp.arange(win)[None, :]
    kb = kp[:, :, idx]
    vb = vp[:, :, idx]
    qb = qs.reshape(B, dilation, nblk, blk, H, hd)
    s = jnp.einsum('brnqhc,brnkhc->brnhqk', qb, kb,
                   preferred_element_type=jnp.float32) * (HEAD_DIM ** -0.5)
    key_pos = idx - side
    q_pos = jnp.arange(nblk)[:, None] * blk + jnp.arange(blk)[None, :]
    rel = key_pos[:, None, :] - q_pos[:, :, None]
    valid = (key_pos >= 0) & (key_pos < L)
    mask = (jnp.abs(rel) <= side) & valid[:, None, :]
    s = jnp.where(mask[None, None, :, None, :, :], s, -jnp.inf)
    m = jnp.max(s, axis=-1, keepdims=True)
    p = jnp.exp(s - m)
    denom = jnp.sum(p, axis=-1)
    o = jnp.einsum('brnhqk,brnkhc->brnqhc', p.astype(vb.dtype), vb,
                   preferred_element_type=jnp.float32)
    o = o / denom.transpose(0, 1, 2, 4, 3)[..., None]
    lse = (m[..., 0] + jnp.log(denom)).transpose(0, 1, 2, 4, 3)
    o = o.reshape(B, dilation, L, H, hd).transpose(0, 2, 1, 3, 4).reshape(B, S, H, hd)
    lse = lse.reshape(B, dilation, L, H).transpose(0, 2, 1, 3).reshape(B, S, H)
    return o, lse


def multiscale_pool(u, w_pool, pool_scale):
    B, S, _ = u.shape
    ug = u.reshape(B, S, POOL_GROUPS, POOL_GROUP_WIDTH).astype(jnp.float32)
    csum = jnp.concatenate([jnp.zeros((B, 1, POOL_GROUPS, POOL_GROUP_WIDTH), jnp.float32),
                            jnp.cumsum(ug, axis=1)], axis=1)
    t = jnp.arange(S)
    pooled = []
    for g, w in enumerate(POOL_WINDOWS):
        lo = jnp.clip(t - w // 2, 0, S)
        hi = jnp.clip(t + (w - w // 2), 0, S)
        cnt = (hi - lo).astype(jnp.float32)[None, :, None]
        mean = (csum[:, hi, g] - csum[:, lo, g]) / cnt
        pooled.append(mean - ug[:, :, g])
    pooled = jnp.stack(pooled, axis=2).astype(u.dtype)
    mixed = jnp.einsum('bsgc,gcd->bsgd', pooled, w_pool)
    return mixed.reshape(B, S, POOL_WIDTH) * pool_scale


def clamped_swiglu(gu):
    gate = jnp.minimum(gu[..., :D_FF], SWIGLU_LIMIT)
    lin = jnp.clip(gu[..., D_FF:], -SWIGLU_LIMIT, SWIGLU_LIMIT)
    return gate * jax.nn.sigmoid(SWIGLU_ALPHA * gate) * (lin + 1.0)


def moe_ffn(h, w_router, b_router, w_gate_up, b_gate_up, w_down, b_down):
    T, D = h.shape
    logits = jnp.matmul(h, w_router, preferred_element_type=jnp.float32) + b_router.astype(jnp.float32)
    top_logits, top_idx = lax.top_k(logits, TOP_K)
    gates = jax.nn.softmax(top_logits, axis=-1)
    n_assign = T * TOP_K
    flat_e = top_idx.reshape(-1).astype(jnp.int32)
    flat_tok = jnp.arange(n_assign, dtype=jnp.int32) // TOP_K
    order = jnp.argsort(flat_e)
    e_sorted = flat_e[order]
    tok_sorted = flat_tok[order]
    g_sorted = gates.reshape(-1)[order]
    counts = jnp.bincount(flat_e, length=N_EXPERTS).astype(jnp.int32)
    padded = (counts + EXPERT_BLOCK - 1) // EXPERT_BLOCK * EXPERT_BLOCK
    start = jnp.cumsum(counts) - counts
    padded_end = jnp.cumsum(padded)
    pstart = padded_end - padded
    dest = pstart[e_sorted] + (jnp.arange(n_assign, dtype=jnp.int32) - start[e_sorted])
    n_blocks = -(-n_assign // EXPERT_BLOCK) + N_EXPERTS
    n_slots = n_blocks * EXPERT_BLOCK
    slot_tok = jnp.zeros((n_slots,), jnp.int32).at[dest].set(tok_sorted)
    slot_gate = jnp.zeros((n_slots,), jnp.float32).at[dest].set(g_sorted)
    block_start = jnp.arange(n_blocks, dtype=jnp.int32) * EXPERT_BLOCK
    block_expert = jnp.minimum(jnp.searchsorted(padded_end, block_start, side='right'),
                               N_EXPERTS - 1).astype(jnp.int32)

    def expert_block(args):
        tok, e = args
        xb = h[tok]
        gu = xb @ w_gate_up[e] + b_gate_up[e]
        return clamped_swiglu(gu) @ w_down[e] + b_down[e]

    y_slots = lax.map(expert_block, (slot_tok.reshape(n_blocks, EXPERT_BLOCK), block_expert))
    y = y_slots.reshape(n_slots, D).astype(jnp.float32) * slot_gate[:, None]
    return jax.ops.segment_sum(y, slot_tok, num_segments=T).astype(h.dtype)


def encoder_layer(x, norm_mix, w_in, q_norm, k_norm, w_pool, pool_scale, w_branch_a, w_branch_b,
                  w_out, norm_ffn, w_router, b_router, w_gate_up, b_gate_up, w_down, b_down):
    B, S, D = x.shape
    h = rms_norm(x, norm_mix)
    proj = h @ w_in
    A = ATTN_WIDTH
    q, k, v, u, g_a, g_b = jnp.split(proj, [A, 2 * A, 3 * A, 3 * A + POOL_WIDTH,
                                           3 * A + POOL_WIDTH + D_MODEL], axis=-1)
    heads = (B, S, N_GROUPS, HEADS_PER_GROUP, HEAD_DIM)
    q = qk_norm_rope(q.reshape(heads), q_norm)
    k = qk_norm_rope(k.reshape(heads), k_norm)
    v = v.reshape(heads)
    outs, lses = [], []
    for g, (window, dilation) in enumerate(DILATION_PAIRS):
        o_g, lse_g = dilated_window_attention(q[:, :, g], k[:, :, g], v[:, :, g], window, dilation)
        outs.append(o_g)
        lses.append(lse_g)
    alpha = jax.nn.softmax(jnp.stack(lses, axis=0), axis=0)
    o_attn = jnp.sum(alpha[..., None] * jnp.stack(outs, axis=0), axis=0)
    o_attn = o_attn.reshape(B, S, ATTN_OUT).astype(x.dtype)
    o_pool = multiscale_pool(u, w_pool, pool_scale)
    mix = jax.nn.sigmoid(g_a) * (o_attn @ w_branch_a) + jax.nn.sigmoid(g_b) * (o_pool @ w_branch_b)
    x = x + mix @ w_out
    h2 = rms_norm(x, norm_ffn).reshape(B * S, D)
    y = moe_ffn(h2, w_router, b_router, w_gate_up, b_gate_up, w_down, b_down)
    return x + y.reshape(B, S, D)


def setup_inputs(seed: int = 0) -> dict:
    key = jax.random.key(seed)
    ks = jax.random.split(key, 20)
    f32 = jnp.float32

    def nrm(k, shape, fan_in):
        return jax.random.normal(k, shape, f32) * (fan_in ** -0.5)

    def gain(k, shape):
        return 1.0 + 0.02 * jax.random.normal(k, shape, f32)

    return {
        "x_prompt": jax.random.normal(ks[0], (BATCH, SEQ, D_MODEL), f32),
        "x_sample": jax.random.normal(ks[1], (DEC_BATCH, DEC_SEQ, D_MODEL), f32),
        "norm_mix": gain(ks[2], (DEPTH, D_MODEL)),
        "w_in": nrm(ks[3], (DEPTH, D_MODEL, IN_COLS), D_MODEL),
        "q_norm": gain(ks[4], (DEPTH, N_GROUPS, HEAD_DIM)),
        "k_norm": gain(ks[5], (DEPTH, N_GROUPS, HEAD_DIM)),
        "w_pool": nrm(ks[6], (DEPTH, POOL_GROUPS, POOL_GROUP_WIDTH, POOL_GROUP_WIDTH), POOL_GROUP_WIDTH),
        "pool_scale": gain(ks[7], (DEPTH, POOL_WIDTH)),
        "w_branch_a": nrm(ks[8], (DEPTH, ATTN_OUT, D_MODEL), ATTN_OUT),
        "w_branch_b": nrm(ks[9], (DEPTH, POOL_WIDTH, D_MODEL), POOL_WIDTH),
        "w_out": nrm(ks[10], (DEPTH, D_MODEL, D_MODEL), D_MODEL),
        "norm_ffn": gain(ks[11], (DEPTH, D_MODEL)),
        "w_router": nrm(ks[12], (DEPTH, D_MODEL, N_EXPERTS), D_MODEL),
        "b_router": 0.01 * jax.random.normal(ks[13], (DEPTH, N_EXPERTS), f32),
        "w_gate_up": nrm(ks[14], (DEPTH, N_EXPERTS, D_MODEL, 2 * D_FF), D_MODEL),
        "b_gate_up": 0.01 * jax.random.normal(ks[15], (DEPTH, N_EXPERTS, 2 * D_FF), f32),
        "w_down": nrm(ks[16], (DEPTH, N_EXPERTS, D_FF, D_MODEL), D_FF),
        "b_down": 0.01 * jax.random.normal(ks[17], (DEPTH, N_EXPERTS, D_MODEL), f32),
    }


def reference(x_prompt, x_sample, norm_mix, w_in, q_norm, k_norm, w_pool, pool_scale, w_branch_a,
              w_branch_b, w_out, norm_ffn, w_router, b_router, w_gate_up, b_gate_up, w_down, b_down):
    y_prompt = x_prompt
    y_sample = x_sample
    for layer in range(DEPTH):
        params = (norm_mix[layer], w_in[layer], q_norm[layer], k_norm[layer], w_pool[layer],
                  pool_scale[layer], w_branch_a[layer], w_branch_b[layer], w_out[layer],
                  norm_ffn[layer], w_router[layer], b_router[layer], w_gate_up[layer],
                  b_gate_up[layer], w_down[layer], b_down[layer])
        y_prompt = encoder_layer(y_prompt, *params)
        y_sample = encoder_layer(y_sample, *params)
    return (y_prompt, y_sample)
```

```python
import functools

import numpy as np
import jax
import jax.numpy as jnp
from jax import lax
from jax.experimental import pallas as pl
from jax.experimental.pallas import tpu as pltpu

HEAD_DIM = 128
DILATION_PAIRS = ((128, 1), (512, 4), (2048, 16))
N_GROUPS = len(DILATION_PAIRS)
ATTN_SIDE = 64
ROPE_THETA = 10000.0
POOL_WINDOWS = (2, 4, 8, 16)
POOL_HALO = 8
TOP_K = 4
SWIGLU_LIMIT = 7.0
SWIGLU_ALPHA = 1.702
RMS_EPS = 1e-6

VMEM_LIMIT_BYTES = 56 * 1024 * 1024
LANES = 128

ATTN_CHUNK = 2048
ATTN_QBLK = 128
EXPERT_TILE = 512
DMA_CHUNK = 2048
COMBINE_TILE = 128


def _cparams(*sem):
    return pltpu.CompilerParams(dimension_semantics=sem, vmem_limit_bytes=VMEM_LIMIT_BYTES)


def _pack_halves(lo, hi):
    lo_bits = lax.bitcast_convert_type(lo.astype(jnp.bfloat16).astype(jnp.float32), jnp.uint32)
    hi_bits = lax.bitcast_convert_type(hi.astype(jnp.bfloat16).astype(jnp.float32), jnp.uint32)
    return (lo_bits >> 16) | (hi_bits & jnp.uint32(0xFFFF0000))


def _unpack_halves(packed):
    lo = lax.bitcast_convert_type(packed << 16, jnp.float32)
    hi = lax.bitcast_convert_type(packed & jnp.uint32(0xFFFF0000), jnp.float32)
    return lo, hi


def _rmsnorm_kernel(x_ref, g_ref, o_ref):
    x = x_ref[...]
    ms = jnp.mean(x * x, axis=-1, keepdims=True)
    o_ref[...] = (x * lax.rsqrt(ms + RMS_EPS) * g_ref[...]).astype(o_ref.dtype)


def _rmsnorm(x, gain, tm=256):
    T, D = x.shape
    return pl.pallas_call(
        _rmsnorm_kernel,
        grid=(T // tm,),
        in_specs=[pl.BlockSpec((tm, D), lambda i: (i, 0)), pl.BlockSpec((1, D), lambda i: (0, 0))],
        out_specs=pl.BlockSpec((tm, D), lambda i: (i, 0)),
        out_shape=jax.ShapeDtypeStruct((T, D), jnp.bfloat16),
        compiler_params=_cparams("parallel"),
        name="rmsnorm",
    )(x, gain.reshape(1, D))


def _qk_proj_kernel(h_ref, w_ref, gain_ref, cos_ref, sin_ref, o_ref):
    acc = jnp.dot(h_ref[...], w_ref[...], preferred_element_type=jnp.float32)
    cos = cos_ref[...]
    sin = sin_ref[...]
    for c in range(acc.shape[1] // HEAD_DIM):
        t = acc[:, c * HEAD_DIM:(c + 1) * HEAD_DIM]
        ms = jnp.mean(t * t, axis=-1, keepdims=True)
        t = t * lax.rsqrt(ms + RMS_EPS) * gain_ref[:, c * HEAD_DIM:(c + 1) * HEAD_DIM]
        o_ref[c] = t * cos + pltpu.roll(t, HEAD_DIM // 2, axis=1) * sin


def _qk_proj(h, w_in, gain_cols, cos_t, sin_t, ncols, tm=512, tn=512):
    T, D = h.shape
    n_heads = ncols // HEAD_DIM
    return pl.pallas_call(
        _qk_proj_kernel,
        grid=(T // tm, ncols // tn),
        in_specs=[
            pl.BlockSpec((tm, D), lambda i, j: (i, 0)),
            pl.BlockSpec((D, tn), lambda i, j: (0, j)),
            pl.BlockSpec((1, tn), lambda i, j: (0, j)),
            pl.BlockSpec((tm, HEAD_DIM), lambda i, j: (i, 0)),
            pl.BlockSpec((tm, HEAD_DIM), lambda i, j: (i, 0)),
        ],
        out_specs=pl.BlockSpec((tn // HEAD_DIM, tm, HEAD_DIM), lambda i, j: (j, i, 0)),
        out_shape=jax.ShapeDtypeStruct((n_heads, T, HEAD_DIM), jnp.float32),
        compiler_params=_cparams("parallel", "arbitrary"),
        name="qk_proj",
    )(h, w_in, gain_cols, cos_t, sin_t)


def _v_proj_kernel(h_ref, w_ref, o_ref):
    acc = jnp.dot(h_ref[...], w_ref[...], preferred_element_type=jnp.float32)
    for c in range(acc.shape[1] // HEAD_DIM):
        o_ref[c] = acc[:, c * HEAD_DIM:(c + 1) * HEAD_DIM]


def _v_proj(h, w_in, col0, ncols, tm=512, tn=512):
    T, D = h.shape
    cb0 = col0 // tn
    return pl.pallas_call(
        _v_proj_kernel,
        grid=(T // tm, ncols // tn),
        in_specs=[
            pl.BlockSpec((tm, D), lambda i, j: (i, 0)),
            pl.BlockSpec((D, tn), lambda i, j: (0, cb0 + j)),
        ],
        out_specs=pl.BlockSpec((tn // HEAD_DIM, tm, HEAD_DIM), lambda i, j: (j, i, 0)),
        out_shape=jax.ShapeDtypeStruct((ncols // HEAD_DIM, T, HEAD_DIM), jnp.float32),
        compiler_params=_cparams("parallel", "arbitrary"),
        name="v_proj",
    )(h, w_in)


def _u_proj_kernel(h_ref, w_ref, o_ref):
    o_ref[...] = jnp.dot(h_ref[...], w_ref[...], preferred_element_type=jnp.float32)


def _u_proj(h, w_in, col0, ncols, tm=512, tn=512):
    T, D = h.shape
    cb0 = col0 // tn
    return pl.pallas_call(
        _u_proj_kernel,
        grid=(T // tm, ncols // tn),
        in_specs=[
            pl.BlockSpec((tm, D), lambda i, j: (i, 0)),
            pl.BlockSpec((D, tn), lambda i, j: (0, cb0 + j)),
        ],
        out_specs=pl.BlockSpec((tm, tn), lambda i, j: (i, j)),
        out_shape=jax.ShapeDtypeStruct((T, ncols), jnp.float32),
        compiler_params=_cparams("parallel", "arbitrary"),
        name="u_proj",
    )(h, w_in)


def _attn_kernel(lo_ref, hi_ref, *refs, chunk):
    in_refs = refs[:7 * N_GROUPS]
    o_ref = refs[7 * N_GROUPS]
    o_sc, lse_sc = refs[7 * N_GROUPS + 1:]
    c = pl.program_id(0)
    t0 = c * chunk
    seq_lo = lo_ref[c]
    seq_hi = hi_ref[c]
    win = ATTN_QBLK + 2 * ATTN_SIDE
    qq = lax.broadcasted_iota(jnp.int32, (ATTN_QBLK, win), 0)
    kk = lax.broadcasted_iota(jnp.int32, (ATTN_QBLK, win), 1)
    band = jnp.abs(kk - ATTN_SIDE - qq) <= ATTN_SIDE
    scale = HEAD_DIM ** -0.5

    for g, (_, d) in enumerate(DILATION_PAIRS):
        q_ref, kc_ref, kl_ref, kr_ref, vc_ref, vl_ref, vr_ref = in_refs[7 * g:7 * g + 7]
        shift = d.bit_length() - 1
        rows = chunk // d
        m0 = t0 >> shift
        m_lo = seq_lo >> shift
        m_hi = seq_hi >> shift
        for r in range(d):
            def sub(ref, n, r=r, d=d):
                return ref[0, pl.ds(r, n, stride=d), :]
            q_r = sub(q_ref, rows).astype(jnp.bfloat16)
            k_r = jnp.concatenate([sub(kl_ref, ATTN_SIDE), sub(kc_ref, rows), sub(kr_ref, ATTN_SIDE)],
                                  axis=0).astype(jnp.bfloat16)
            v_r = jnp.concatenate([sub(vl_ref, ATTN_SIDE), sub(vc_ref, rows), sub(vr_ref, ATTN_SIDE)],
                                  axis=0).astype(jnp.bfloat16)
            for b in range(rows // ATTN_QBLK):
                qs = q_r[b * ATTN_QBLK:(b + 1) * ATTN_QBLK]
                ks = k_r[b * ATTN_QBLK:b * ATTN_QBLK + win]
                vs = v_r[b * ATTN_QBLK:b * ATTN_QBLK + win]
                s = lax.dot_general(qs, ks, (((1,), (1,)), ((), ())),
                                    preferred_element_type=jnp.float32) * scale
                key_m = kk + (m0 + (b * ATTN_QBLK - ATTN_SIDE))
                valid = band & (key_m >= m_lo) & (key_m < m_hi)
                s = jnp.where(valid, s, -jnp.inf)
                m = jnp.max(s, axis=-1, keepdims=True)
                p = jnp.exp(s - m)
                denom = jnp.sum(p, axis=-1, keepdims=True)
                o = jnp.dot(p.astype(jnp.bfloat16), vs, preferred_element_type=jnp.float32) / denom
                lse = m + jnp.log(denom)
                dst = pl.ds(b * ATTN_QBLK * d + r, ATTN_QBLK, stride=d)
                o_sc[g, dst, :] = o
                lse_sc[g, dst, :] = jnp.broadcast_to(lse, (ATTN_QBLK, HEAD_DIM))

    lses = [lse_sc[g] for g in range(N_GROUPS)]
    top = functools.reduce(jnp.maximum, lses)
    ws = [jnp.exp(l - top) for l in lses]
    num = sum(w * o_sc[g] for g, w in enumerate(ws))
    o_ref[...] = (num / sum(ws)).astype(o_ref.dtype)


def _attention(qk, v, seq_lo, seq_hi, heads_per_group):
    n_qk, T, _ = qk.shape
    H = heads_per_group
    k_base = n_qk // 2
    chunk = ATTN_CHUNK
    n_chunks = T // chunk
    in_specs = []
    operands = []
    for g, (_, d) in enumerate(DILATION_PAIRS):
        halo = ATTN_SIDE * d
        per = chunk // halo
        n_halo = T // halo

        def center(base, g=g):
            return pl.BlockSpec((1, chunk, HEAD_DIM), lambda c, h, lo, hi: (base + g * H + h, c, 0))

        def left(base, g=g, per=per):
            return pl.BlockSpec((1, halo, HEAD_DIM),
                                lambda c, h, lo, hi: (base + g * H + h, jnp.maximum(c * per - 1, 0), 0))

        def right(base, g=g, per=per, n_halo=n_halo):
            return pl.BlockSpec((1, halo, HEAD_DIM),
                                lambda c, h, lo, hi: (base + g * H + h, jnp.minimum((c + 1) * per, n_halo - 1), 0))

        in_specs += [center(0), center(k_base), left(k_base), right(k_base), center(0), left(0), right(0)]
        operands += [qk, qk, qk, qk, v, v, v]
    grid_spec = pltpu.PrefetchScalarGridSpec(
        num_scalar_prefetch=2,
        grid=(n_chunks, H),
        in_specs=in_specs,
        out_specs=pl.BlockSpec((chunk, HEAD_DIM), lambda c, h, lo, hi: (c, h)),
        scratch_shapes=[pltpu.VMEM((N_GROUPS, chunk, HEAD_DIM), jnp.float32),
                        pltpu.VMEM((N_GROUPS, chunk, HEAD_DIM), jnp.float32)],
    )
    return pl.pallas_call(
        functools.partial(_attn_kernel, chunk=chunk),
        grid_spec=grid_spec,
        out_shape=jax.ShapeDtypeStruct((T, H * HEAD_DIM), jnp.bfloat16),
        compiler_params=_cparams("parallel", "arbitrary"),
        name="dilated_attention",
    )(seq_lo, seq_hi, *operands)


def _pool_kernel(lo_ref, hi_ref, uc_ref, ul_ref, ur_ref, w_ref, scale_ref, o_ref, ext_sc, *, tp):
    i = pl.program_id(0)
    t0 = i * tp
    seq_lo = lo_ref[i]
    seq_hi = hi_ref[i]
    gw = w_ref.shape[1]
    pos_l = t0 - POOL_HALO + lax.broadcasted_iota(jnp.int32, (POOL_HALO, 1), 0)
    pos_r = t0 + tp + lax.broadcasted_iota(jnp.int32, (POOL_HALO, 1), 0)
    ext_sc[0:POOL_HALO, :] = jnp.where(pos_l >= seq_lo, ul_ref[...], 0.0)
    ext_sc[POOL_HALO:POOL_HALO + tp, :] = uc_ref[...]
    ext_sc[POOL_HALO + tp:, :] = jnp.where(pos_r < seq_hi, ur_ref[...], 0.0)
    pos = t0 + lax.broadcasted_iota(jnp.int32, (tp, 1), 0)
    for gi, w in enumerate(POOL_WINDOWS):
        cols = slice(gi * gw, (gi + 1) * gw)
        total = None
        for j in range(-(w // 2), w - w // 2):
            piece = ext_sc[POOL_HALO + j:POOL_HALO + j + tp, cols]
            total = piece if total is None else total + piece
        cnt = jnp.minimum(pos + (w - w // 2), seq_hi) - jnp.maximum(pos - w // 2, seq_lo)
        pooled = total / cnt.astype(jnp.float32) - uc_ref[:, cols]
        mixed = jnp.dot(pooled.astype(jnp.bfloat16), w_ref[gi], preferred_element_type=jnp.float32)
        o_ref[:, cols] = (mixed * scale_ref[:, cols]).astype(o_ref.dtype)


def _pool(u, w_pool, pool_scale, seq_lo, seq_hi, tp):
    T, P = u.shape
    per = tp // POOL_HALO
    n_halo = T // POOL_HALO
    grid_spec = pltpu.PrefetchScalarGridSpec(
        num_scalar_prefetch=2,
        grid=(T // tp,),
        in_specs=[
            pl.BlockSpec((tp, P), lambda i, lo, hi: (i, 0)),
            pl.BlockSpec((POOL_HALO, P), lambda i, lo, hi: (jnp.maximum(i * per - 1, 0), 0)),
            pl.BlockSpec((POOL_HALO, P), lambda i, lo, hi: (jnp.minimum((i + 1) * per, n_halo - 1), 0)),
            pl.BlockSpec(w_pool.shape, lambda i, lo, hi: (0, 0, 0)),
            pl.BlockSpec((1, P), lambda i, lo, hi: (0, 0)),
        ],
        out_specs=pl.BlockSpec((tp, P), lambda i, lo, hi: (i, 0)),
        scratch_shapes=[pltpu.VMEM((tp + 2 * POOL_HALO, P), jnp.float32)],
    )
    return pl.pallas_call(
        functools.partial(_pool_kernel, tp=tp),
        grid_spec=grid_spec,
        out_shape=jax.ShapeDtypeStruct((T, P), jnp.bfloat16),
        compiler_params=_cparams("parallel"),
        name="multiscale_pool",
    )(seq_lo, seq_hi, u, u, u, w_pool, pool_scale.reshape(1, P))


def _mix_kernel(h_ref, wga_ref, wgb_ref, oa_ref, wa_ref, ob_ref, wb_ref, o_ref):
    h = h_ref[...]
    g_a = jnp.dot(h, wga_ref[...], preferred_element_type=jnp.float32)
    g_b = jnp.dot(h, wgb_ref[...], preferred_element_type=jnp.float32)
    a = jnp.dot(oa_ref[...], wa_ref[...], preferred_element_type=jnp.float32)
    b = jnp.dot(ob_ref[...], wb_ref[...], preferred_element_type=jnp.float32)
    o_ref[...] = (jax.nn.sigmoid(g_a) * a + jax.nn.sigmoid(g_b) * b).astype(o_ref.dtype)


def _mix(h, w_in, ga_col0, o_attn, w_a, o_pool, w_b, tm=512, tn=512):
    T, D = h.shape
    cb_a = ga_col0 // tn
    cb_b = (ga_col0 + D) // tn
    A = o_attn.shape[1]
    P = o_pool.shape[1]
    return pl.pallas_call(
        _mix_kernel,
        grid=(T // tm, D // tn),
        in_specs=[
            pl.BlockSpec((tm, D), lambda i, j: (i, 0)),
            pl.BlockSpec((D, tn), lambda i, j: (0, cb_a + j)),
            pl.BlockSpec((D, tn), lambda i, j: (0, cb_b + j)),
            pl.BlockSpec((tm, A), lambda i, j: (i, 0)),
            pl.BlockSpec((A, tn), lambda i, j: (0, j)),
            pl.BlockSpec((tm, P), lambda i, j: (i, 0)),
            pl.BlockSpec((P, tn), lambda i, j: (0, j)),
        ],
        out_specs=pl.BlockSpec((tm, tn), lambda i, j: (i, j)),
        out_shape=jax.ShapeDtypeStruct((T, D), jnp.bfloat16),
        compiler_params=_cparams("parallel", "arbitrary"),
        name="gated_mix",
    )(h, w_in, w_in, o_attn, w_a, o_pool, w_b)


def _out_proj_kernel(m_ref, w_ref, x_ref, o_ref):
    o_ref[...] = x_ref[...] + jnp.dot(m_ref[...], w_ref[...], preferred_element_type=jnp.float32)


def _out_proj(mix, w_out, x, tm=512, tn=1024):
    T, D = x.shape
    tn = min(tn, D)
    return pl.pallas_call(
        _out_proj_kernel,
        grid=(T // tm, D // tn),
        in_specs=[
            pl.BlockSpec((tm, D), lambda i, j: (i, 0)),
            pl.BlockSpec((D, tn), lambda i, j: (0, j)),
            pl.BlockSpec((tm, tn), lambda i, j: (i, j)),
        ],
        out_specs=pl.BlockSpec((tm, tn), lambda i, j: (i, j)),
        out_shape=jax.ShapeDtypeStruct((T, D), jnp.float32),
        compiler_params=_cparams("parallel", "arbitrary"),
        name="out_proj_residual",
    )(mix, w_out, x)


def _router_kernel(x_ref, g_ref, wr_ref, br_ref, hp_ref, idx_ref, gate_ref):
    x = x_ref[...]
    ms = jnp.mean(x * x, axis=-1, keepdims=True)
    h = x * lax.rsqrt(ms + RMS_EPS) * g_ref[...]
    half = h.shape[1] // 2
    hp_ref[...] = _pack_halves(h[:, :half], h[:, half:])
    logits = jnp.dot(h.astype(jnp.bfloat16), wr_ref[...], preferred_element_type=jnp.float32) + br_ref[...]
    n_exp = logits.shape[1]
    lane = lax.broadcasted_iota(jnp.int32, logits.shape, 1)
    vals, idxs = [], []
    for _ in range(TOP_K):
        m = jnp.max(logits, axis=-1, keepdims=True)
        idx = jnp.min(jnp.where(logits == m, lane, n_exp), axis=-1, keepdims=True)
        vals.append(m)
        idxs.append(idx)
        logits = jnp.where(lane == idx, -jnp.inf, logits)
    exps = [jnp.exp(v - vals[0]) for v in vals]
    total = sum(exps)
    for k in range(TOP_K):
        idx_ref[:, k:k + 1] = idxs[k]
        gate_ref[:, k:k + 1] = exps[k] / total


def _router(x1, gain, w_router, b_router, tr=256):
    T, D = x1.shape
    E = w_router.shape[1]
    return pl.pallas_call(
        _router_kernel,
        grid=(T // tr,),
        in_specs=[
            pl.BlockSpec((tr, D), lambda i: (i, 0)),
            pl.BlockSpec((1, D), lambda i: (0, 0)),
            pl.BlockSpec((D, E), lambda i: (0, 0)),
            pl.BlockSpec((1, E), lambda i: (0, 0)),
        ],
        out_specs=[
            pl.BlockSpec((tr, D // 2), lambda i: (i, 0)),
            pl.BlockSpec((tr, TOP_K), lambda i: (i, 0)),
            pl.BlockSpec((tr, TOP_K), lambda i: (i, 0)),
        ],
        out_shape=[
            jax.ShapeDtypeStruct((T, D // 2), jnp.uint32),
            jax.ShapeDtypeStruct((T, TOP_K), jnp.int32),
            jax.ShapeDtypeStruct((T, TOP_K), jnp.float32),
        ],
        compiler_params=_cparams("parallel"),
        name="router_topk",
    )(x1, gain.reshape(1, D), w_router, b_router.reshape(1, E))


def _routing_tables(top_idx, n_experts, n_tiles):
    e = top_idx.reshape(-1)
    onehot = (e[:, None] == jnp.arange(n_experts, dtype=jnp.int32)[None, :]).astype(jnp.int32)
    csum = jnp.cumsum(onehot, axis=0)
    counts = csum[-1]
    padded = (counts + EXPERT_TILE - 1) // EXPERT_TILE * EXPERT_TILE
    pend = jnp.cumsum(padded)
    pstart = pend - padded
    pos = jnp.sum(onehot * (csum - 1 + pstart[None, :]), axis=1).astype(jnp.int32)
    tile_start = jnp.arange(n_tiles, dtype=jnp.int32) * EXPERT_TILE
    tile_expert = jnp.minimum(jnp.sum((pend[None, :] <= tile_start[:, None]).astype(jnp.int32), axis=1),
                              n_experts - 1).astype(jnp.int32)
    n_used = (pend[-1] // EXPERT_TILE).astype(jnp.int32).reshape(1)
    pad_lo = (pstart + counts).astype(jnp.int32)
    return pos, tile_expert, n_used, pad_lo, pend.astype(jnp.int32)


def _dispatch_kernel(pos_ref, pad_lo_ref, pad_hi_ref, hp_ref, xs_ref, sem, *, n_experts):
    step = pl.program_id(0)
    n_steps = pl.num_programs(0)
    n = pos_ref.shape[2]
    a0 = step * n

    def row_copy(src_row, dst_row):
        return pltpu.make_async_copy(hp_ref.at[pl.ds(src_row, 1)], xs_ref.at[pl.ds(dst_row, 1)], sem)

    def issue(a, carry):
        row_copy((a0 + a) >> 2, pos_ref[0, 0, a]).start()
        return carry

    lax.fori_loop(0, n, issue, 0, unroll=8)

    def drain(a, carry):
        row_copy(0, 0).wait()
        return carry

    lax.fori_loop(0, n, drain, 0, unroll=8)

    @pl.when(step == n_steps - 1)
    def _():
        for e in range(n_experts):
            lo = pad_lo_ref[e]
            hi = pad_hi_ref[e]

            def fill(p, carry):
                row_copy(0, p).start()
                return carry

            lax.fori_loop(lo, hi, fill, 0)

            def fill_wait(p, carry):
                row_copy(0, 0).wait()
                return carry

            lax.fori_loop(lo, hi, fill_wait, 0)


def _dispatch(hp, pos, pad_lo, pad_hi, n_slots):
    T, W = hp.shape
    n_assign = pos.shape[0]
    n = min(DMA_CHUNK, n_assign)
    n_experts = pad_lo.shape[0]
    grid_spec = pltpu.PrefetchScalarGridSpec(
        num_scalar_prefetch=0,
        grid=(n_assign // n,),
        in_specs=[
            pl.BlockSpec((1, 1, n), lambda s: (s, 0, 0), memory_space=pltpu.SMEM),
            pl.BlockSpec(memory_space=pltpu.SMEM),
            pl.BlockSpec(memory_space=pltpu.SMEM),
            pl.BlockSpec(memory_space=pl.ANY),
        ],
        out_specs=pl.BlockSpec(memory_space=pl.ANY),
        scratch_shapes=[pltpu.SemaphoreType.DMA(())],
    )
    return pl.pallas_call(
        functools.partial(_dispatch_kernel, n_experts=n_experts),
        grid_spec=grid_spec,
        out_shape=jax.ShapeDtypeStruct((n_slots, W), jnp.uint32),
        compiler_params=_cparams("arbitrary"),
        name="moe_dispatch",
    )(pos.reshape(n_assign // n, 1, n), pad_lo, pad_hi, hp)


def _gate_up_kernel(te_ref, nu_ref, xs_ref, wg_ref, wu_ref, bg_ref, bu_ref, o_ref, xlo_sc, xhi_sc):
    i = pl.program_id(0)
    j = pl.program_id(1)
    used = i < nu_ref[0]

    @pl.when(used & (j == 0))
    def _():
        lo, hi = _unpack_halves(xs_ref[...])
        xlo_sc[...] = lo.astype(jnp.bfloat16)
        xhi_sc[...] = hi.astype(jnp.bfloat16)

    @pl.when(used)
    def _():
        half = xlo_sc.shape[1]
        x_lo = xlo_sc[...]
        x_hi = xhi_sc[...]

        def proj(w_ref, b_ref):
            return (jnp.dot(x_lo, w_ref[0, :half, :], preferred_element_type=jnp.float32)
                    + jnp.dot(x_hi, w_ref[0, half:, :], preferred_element_type=jnp.float32) + b_ref[0])

        gate = jnp.minimum(proj(wg_ref, bg_ref), SWIGLU_LIMIT)
        lin = jnp.clip(proj(wu_ref, bu_ref), -SWIGLU_LIMIT, SWIGLU_LIMIT)
        o_ref[...] = (gate * jax.nn.sigmoid(SWIGLU_ALPHA * gate) * (lin + 1.0)).astype(o_ref.dtype)

    @pl.when(jnp.logical_not(used))
    def _():
        o_ref[...] = jnp.zeros_like(o_ref)


def _expert_gate_up(xs, w_gu, b_gu, tile_expert, n_used, tn=512):
    n_slots, half = xs.shape
    E, D, F2 = w_gu.shape
    F = F2 // 2
    tn = min(tn, F)
    nj = F // tn
    n_tiles = n_slots // EXPERT_TILE

    def wmap(off):
        def index(i, j, te, nu):
            jj = jnp.where(i < nu[0], j, nj - 1)
            return (te[i], 0, off + jj)
        return index

    grid_spec = pltpu.PrefetchScalarGridSpec(
        num_scalar_prefetch=2,
        grid=(n_tiles, nj),
        in_specs=[
            pl.BlockSpec((EXPERT_TILE, half), lambda i, j, te, nu: (i, 0)),
            pl.BlockSpec((1, D, tn), wmap(0)),
            pl.BlockSpec((1, D, tn), wmap(nj)),
            pl.BlockSpec((1, 1, tn), wmap(0)),
            pl.BlockSpec((1, 1, tn), wmap(nj)),
        ],
        out_specs=pl.BlockSpec((EXPERT_TILE, tn), lambda i, j, te, nu: (i, j)),
        scratch_shapes=[pltpu.VMEM((EXPERT_TILE, half), jnp.bfloat16),
                        pltpu.VMEM((EXPERT_TILE, half), jnp.bfloat16)],
    )
    return pl.pallas_call(
        _gate_up_kernel,
        grid_spec=grid_spec,
        out_shape=jax.ShapeDtypeStruct((n_slots, F), jnp.bfloat16),
        compiler_params=_cparams("arbitrary", "arbitrary"),
        name="expert_gate_up",
    )(tile_expert, n_used, xs, w_gu, w_gu, b_gu.reshape(E, 1, F2), b_gu.reshape(E, 1, F2))


def _down_kernel(te_ref, nu_ref, a_ref, wlo_ref, whi_ref, blo_ref, bhi_ref, o_ref):
    i = pl.program_id(0)
    used = i < nu_ref[0]

    @pl.when(used)
    def _():
        a = a_ref[...]
        y_lo = jnp.dot(a, wlo_ref[0], preferred_element_type=jnp.float32) + blo_ref[0]
        y_hi = jnp.dot(a, whi_ref[0], preferred_element_type=jnp.float32) + bhi_ref[0]
        o_ref[...] = _pack_halves(y_lo, y_hi)

    @pl.when(jnp.logical_not(used))
    def _():
        o_ref[...] = jnp.zeros_like(o_ref)


def _expert_down(act, w_down, b_down, tile_expert, n_used, tn=512):
    n_slots, F = act.shape
    E, _, D = w_down.shape
    half = D // 2
    tn = min(tn, half)
    nj = half // tn
    n_tiles = n_slots // EXPERT_TILE

    def wmap(off):
        def index(i, j, te, nu):
            jj = jnp.where(i < nu[0], j, nj - 1)
            return (te[i], 0, off + jj)
        return index

    grid_spec = pltpu.PrefetchScalarGridSpec(
        num_scalar_prefetch=2,
        grid=(n_tiles, nj),
        in_specs=[
            pl.BlockSpec((EXPERT_TILE, F), lambda i, j, te, nu: (i, 0)),
            pl.BlockSpec((1, F, tn), wmap(0)),
            pl.BlockSpec((1, F, tn), wmap(nj)),
            pl.BlockSpec((1, 1, tn), wmap(0)),
            pl.BlockSpec((1, 1, tn), wmap(nj)),
        ],
        out_specs=pl.BlockSpec((EXPERT_TILE, tn), lambda i, j, te, nu: (i, j)),
    )
    return pl.pallas_call(
        _down_kernel,
        grid_spec=grid_spec,
        out_shape=jax.ShapeDtypeStruct((n_slots, half), jnp.uint32),
        compiler_params=_cparams("arbitrary", "arbitrary"),
        name="expert_down",
    )(tile_expert, n_used, act, w_down, w_down, b_down.reshape(E, 1, D), b_down.reshape(E, 1, D))


def _combine_kernel(pos_ref, x_ref, gate_ref, ys_ref, o_ref, buf, sem, *, tt):
    def row_copy(src_row, k, tok):
        return pltpu.make_async_copy(ys_ref.at[pl.ds(src_row, 1)], buf.at[k, pl.ds(tok, 1)], sem)

    def issue(tok, carry):
        for k in range(TOP_K):
            row_copy(pos_ref[0, 0, tok * TOP_K + k], k, tok).start()
        return carry

    lax.fori_loop(0, tt, issue, 0, unroll=4)

    def drain(tok, carry):
        for k in range(TOP_K):
            row_copy(0, k, 0).wait()
        return carry

    lax.fori_loop(0, tt, drain, 0, unroll=4)

    half = buf.shape[2]
    acc_lo = x_ref[:, :half]
    acc_hi = x_ref[:, half:]
    for k in range(TOP_K):
        lo, hi = _unpack_halves(buf[k])
        gk = gate_ref[:, k:k + 1]
        acc_lo = acc_lo + gk * lo
        acc_hi = acc_hi + gk * hi
    o_ref[:, :half] = acc_lo
    o_ref[:, half:] = acc_hi


def _combine(x1, gates, pos, ys, tt=COMBINE_TILE):
    T, D = x1.shape
    half = D // 2
    grid_spec = pltpu.PrefetchScalarGridSpec(
        num_scalar_prefetch=0,
        grid=(T // tt,),
        in_specs=[
            pl.BlockSpec((1, 1, tt * TOP_K), lambda i: (i, 0, 0), memory_space=pltpu.SMEM),
            pl.BlockSpec((tt, D), lambda i: (i, 0)),
            pl.BlockSpec((tt, TOP_K), lambda i: (i, 0)),
            pl.BlockSpec(memory_space=pl.ANY),
        ],
        out_specs=pl.BlockSpec((tt, D), lambda i: (i, 0)),
        scratch_shapes=[pltpu.VMEM((TOP_K, tt, half), jnp.uint32), pltpu.SemaphoreType.DMA(())],
    )
    return pl.pallas_call(
        functools.partial(_combine_kernel, tt=tt),
        grid_spec=grid_spec,
        out_shape=jax.ShapeDtypeStruct((T, D), jnp.float32),
        compiler_params=_cparams("arbitrary"),
        name="moe_combine",
    )(pos.reshape(T // tt, 1, tt * TOP_K), x1, gates, ys)


def _sequence_tables(seq_lens, tile):
    lo, hi, start = [], [], 0
    for n in seq_lens:
        assert n % tile == 0
        lo += [start] * (n // tile)
        hi += [start + n] * (n // tile)
        start += n
    return jnp.asarray(np.array(lo, np.int32)), jnp.asarray(np.array(hi, np.int32))


def _rope_tables(seq_lens):
    half = HEAD_DIM // 2
    inv_freq = ROPE_THETA ** (-jnp.arange(half, dtype=jnp.float32) / half)
    pos = jnp.concatenate([jnp.arange(n, dtype=jnp.float32) for n in seq_lens])
    ang = pos[:, None] * inv_freq[None, :]
    cos, sin = jnp.cos(ang), jnp.sin(ang)
    return jnp.concatenate([cos, cos], axis=1), jnp.concatenate([-sin, sin], axis=1)


def _encoder_layer(x, seq_lens, norm_mix, w_in, q_norm, k_norm, w_pool, pool_scale, w_branch_a,
                   w_branch_b, w_out, norm_ffn, w_router, b_router, w_gate_up, b_gate_up, w_down, b_down):
    T, D = x.shape
    bf16 = jnp.bfloat16
    A_out = w_branch_a.shape[0]
    H = A_out // HEAD_DIM
    A = N_GROUPS * A_out
    P = w_branch_b.shape[0]
    E = w_router.shape[1]

    w_in_b = w_in.astype(bf16)
    gain_cols = jnp.concatenate([jnp.tile(q_norm, (1, H)).reshape(1, A),
                                 jnp.tile(k_norm, (1, H)).reshape(1, A)], axis=1)
    cos_t, sin_t = _rope_tables(seq_lens)

    h = _rmsnorm(x, norm_mix)
    qk = _qk_proj(h, w_in_b, gain_cols, cos_t, sin_t, 2 * A)
    v = _v_proj(h, w_in_b, 2 * A, A)
    u = _u_proj(h, w_in_b, 3 * A, P)

    lo_a, hi_a = _sequence_tables(seq_lens, ATTN_CHUNK)
    o_attn = _attention(qk, v, lo_a, hi_a, H)
    tp = 512
    lo_p, hi_p = _sequence_tables(seq_lens, tp)
    o_pool = _pool(u, w_pool.astype(bf16), pool_scale, lo_p, hi_p, tp)

    mix = _mix(h, w_in_b, 3 * A + P, o_attn, w_branch_a.astype(bf16), o_pool, w_branch_b.astype(bf16))
    x1 = _out_proj(mix, w_out.astype(bf16), x)

    hp, top_idx, gates = _router(x1, norm_ffn, w_router.astype(bf16), b_router)
    n_assign = T * TOP_K
    n_tiles = n_assign // EXPERT_TILE + E
    pos, tile_expert, n_used, pad_lo, pad_hi = _routing_tables(top_idx, E, n_tiles)
    xs = _dispatch(hp, pos, pad_lo, pad_hi, n_tiles * EXPERT_TILE)
    act = _expert_gate_up(xs, w_gate_up.astype(bf16), b_gate_up, tile_expert, n_used)
    ys = _expert_down(act, w_down.astype(bf16), b_down, tile_expert, n_used)
    return _combine(x1, gates, pos, ys)


def kernel(x_prompt, x_sample, norm_mix, w_in, q_norm, k_norm, w_pool, pool_scale, w_branch_a, w_branch_b,
           w_out, norm_ffn, w_router, b_router, w_gate_up, b_gate_up, w_down, b_down):
    depth = norm_mix.shape[0]
    D = x_prompt.shape[-1]
    seq_lens = (x_prompt.shape[1],) * x_prompt.shape[0] + (x_sample.shape[1],) * x_sample.shape[0]
    n_prompt = x_prompt.shape[0] * x_prompt.shape[1]
    x = jnp.concatenate([x_prompt.reshape(-1, D), x_sample.reshape(-1, D)], axis=0)
    for layer in range(depth):
        x = _encoder_layer(x, seq_lens, norm_mix[layer], w_in[layer], q_norm[layer], k_norm[layer],
                           w_pool[layer], pool_scale[layer], w_branch_a[layer], w_branch_b[layer],
                           w_out[layer], norm_ffn[layer], w_router[layer], b_router[layer],
                           w_gate_up[layer], b_gate_up[layer], w_down[layer], b_down[layer])
    return x[:n_prompt].reshape(x_prompt.shape), x[n_prompt:].reshape(x_sample.shape)
```

```python
import functools

import numpy as np
import jax
import jax.numpy as jnp
from jax import lax
from jax.experimental import pallas as pl
from jax.experimental.pallas import tpu as pltpu

HEAD_DIM = 128
DILATION_PAIRS = ((128, 1), (512, 4), (2048, 16))
N_GROUPS = len(DILATION_PAIRS)
ATTN_SIDE = 64
ROPE_THETA = 10000.0
POOL_WINDOWS = (2, 4, 8, 16)
POOL_HALO = 8
TOP_K = 4
SWIGLU_LIMIT = 7.0
SWIGLU_ALPHA = 1.702
RMS_EPS = 1e-6

VMEM_LIMIT_BYTES = 56 * 1024 * 1024
QK_ROW_SPLIT = 256

ATTN_CHUNK = 2048
ATTN_QBLK = 128
EXPERT_TILE = 512
WEIGHT_CAST_ROWS = 512
DISPATCH_TILE = 256
COMBINE_TILE = 128


def _cparams(*sem):
    return pltpu.CompilerParams(dimension_semantics=sem, vmem_limit_bytes=VMEM_LIMIT_BYTES)


def _pack_halves(lo, hi):
    lo_bits = lax.bitcast_convert_type(lo.astype(jnp.bfloat16).astype(jnp.float32), jnp.uint32)
    hi_bits = lax.bitcast_convert_type(hi.astype(jnp.bfloat16).astype(jnp.float32), jnp.uint32)
    return (lo_bits >> 16) | (hi_bits & jnp.uint32(0xFFFF0000))


def _unpack_halves(packed):
    lo = lax.bitcast_convert_type(packed << 16, jnp.float32)
    hi = lax.bitcast_convert_type(packed & jnp.uint32(0xFFFF0000), jnp.float32)
    return lo, hi


def _rmsnorm_kernel(xa_ref, xb_ref, g_ref, o_ref, *, n_first):
    def body(x_ref):
        x = x_ref[...]
        ms = jnp.mean(x * x, axis=-1, keepdims=True)
        o_ref[...] = (x * lax.rsqrt(ms + RMS_EPS) * g_ref[...]).astype(o_ref.dtype)

    pl.when(pl.program_id(0) < n_first)(lambda: body(xa_ref))
    pl.when(pl.program_id(0) >= n_first)(lambda: body(xb_ref))


def _rmsnorm(xa, xb, gain, tm=256):
    D = xa.shape[1]
    T = xa.shape[0] + xb.shape[0]
    n_first = xa.shape[0] // tm
    return pl.pallas_call(
        functools.partial(_rmsnorm_kernel, n_first=n_first),
        grid=(T // tm,),
        in_specs=[pl.BlockSpec((tm, D), lambda i: (jnp.minimum(i, n_first - 1), 0)),
                  pl.BlockSpec((tm, D), lambda i: (jnp.maximum(i - n_first, 0), 0)),
                  pl.BlockSpec((1, D), lambda i: (0, 0))],
        out_specs=pl.BlockSpec((tm, D), lambda i: (i, 0)),
        out_shape=jax.ShapeDtypeStruct((T, D), jnp.bfloat16),
        compiler_params=_cparams("arbitrary"),
        name="rmsnorm",
    )(xa, xb, gain.reshape(1, D))


def _qk_proj_kernel(h_ref, w_ref, gain_ref, cos_ref, sin_ref, o_ref):
    w = w_ref[...]
    tm = h_ref.shape[0]
    for r0 in range(0, tm, QK_ROW_SPLIT):
        rows = slice(r0, r0 + QK_ROW_SPLIT)
        acc = jnp.dot(h_ref[rows, :], w, preferred_element_type=jnp.float32)
        cos = cos_ref[rows, :]
        sin = sin_ref[rows, :]
        for c in range(acc.shape[1] // HEAD_DIM):
            t = acc[:, c * HEAD_DIM:(c + 1) * HEAD_DIM]
            ms = jnp.mean(t * t, axis=-1, keepdims=True)
            t = t * lax.rsqrt(ms + RMS_EPS) * gain_ref[:, c * HEAD_DIM:(c + 1) * HEAD_DIM]
            o_ref[c, rows, :] = t * cos + pltpu.roll(t, HEAD_DIM // 2, axis=1) * sin


def _qk_proj(h, w_in, gain_cols, cos_t, sin_t, ncols, tm=512, tn=512):
    T, D = h.shape
    n_heads = ncols // HEAD_DIM
    return pl.pallas_call(
        _qk_proj_kernel,
        grid=(T // tm, ncols // tn),
        in_specs=[
            pl.BlockSpec((tm, D), lambda i, j: (i, 0)),
            pl.BlockSpec((D, tn), lambda i, j: (0, j)),
            pl.BlockSpec((1, tn), lambda i, j: (0, j)),
            pl.BlockSpec((tm, HEAD_DIM), lambda i, j: (i, 0)),
            pl.BlockSpec((tm, HEAD_DIM), lambda i, j: (i, 0)),
        ],
        out_specs=pl.BlockSpec((tn // HEAD_DIM, tm, HEAD_DIM), lambda i, j: (j, i, 0)),
        out_shape=jax.ShapeDtypeStruct((n_heads, T, HEAD_DIM), jnp.float32),
        compiler_params=_cparams("arbitrary", "arbitrary"),
        name="qk_proj",
    )(h, w_in, gain_cols, cos_t, sin_t)


def _v_proj_kernel(h_ref, w_ref, o_ref):
    acc = jnp.dot(h_ref[...], w_ref[...], preferred_element_type=jnp.float32)
    for c in range(acc.shape[1] // HEAD_DIM):
        o_ref[c] = acc[:, c * HEAD_DIM:(c + 1) * HEAD_DIM]


def _v_proj(h, w_in, col0, ncols, tm=512, tn=512):
    T, D = h.shape
    cb0 = col0 // tn
    return pl.pallas_call(
        _v_proj_kernel,
        grid=(T // tm, ncols // tn),
        in_specs=[
            pl.BlockSpec((tm, D), lambda i, j: (i, 0)),
            pl.BlockSpec((D, tn), lambda i, j: (0, cb0 + j)),
        ],
        out_specs=pl.BlockSpec((tn // HEAD_DIM, tm, HEAD_DIM), lambda i, j: (j, i, 0)),
        out_shape=jax.ShapeDtypeStruct((ncols // HEAD_DIM, T, HEAD_DIM), jnp.float32),
        compiler_params=_cparams("arbitrary", "arbitrary"),
        name="v_proj",
    )(h, w_in)


def _u_proj_kernel(h_ref, w_ref, o_ref):
    o_ref[...] = jnp.dot(h_ref[...], w_ref[...], preferred_element_type=jnp.float32)


def _u_proj(h, w_in, col0, ncols, tm=512, tn=512):
    T, D = h.shape
    cb0 = col0 // tn
    return pl.pallas_call(
        _u_proj_kernel,
        grid=(T // tm, ncols // tn),
        in_specs=[
            pl.BlockSpec((tm, D), lambda i, j: (i, 0)),
            pl.BlockSpec((D, tn), lambda i, j: (0, cb0 + j)),
        ],
        out_specs=pl.BlockSpec((tm, tn), lambda i, j: (i, j)),
        out_shape=jax.ShapeDtypeStruct((T, ncols), jnp.float32),
        compiler_params=_cparams("arbitrary", "arbitrary"),
        name="u_proj",
    )(h, w_in)


def _attn_kernel(lo_ref, hi_ref, *refs, chunk):
    in_refs = refs[:7 * N_GROUPS]
    o_ref = refs[7 * N_GROUPS]
    o_sc, lse_sc = refs[7 * N_GROUPS + 1:]
    c = pl.program_id(0)
    t0 = c * chunk
    seq_lo = lo_ref[c]
    seq_hi = hi_ref[c]
    win = ATTN_QBLK + 2 * ATTN_SIDE
    qq = lax.broadcasted_iota(jnp.int32, (ATTN_QBLK, win), 0)
    kk = lax.broadcasted_iota(jnp.int32, (ATTN_QBLK, win), 1)
    band = jnp.abs(kk - ATTN_SIDE - qq) <= ATTN_SIDE
    scale = HEAD_DIM ** -0.5

    for g, (_, d) in enumerate(DILATION_PAIRS):
        q_ref, kc_ref, kl_ref, kr_ref, vc_ref, vl_ref, vr_ref = in_refs[7 * g:7 * g + 7]
        shift = d.bit_length() - 1
        rows = chunk // d
        m0 = t0 >> shift
        m_lo = seq_lo >> shift
        m_hi = seq_hi >> shift
        for r in range(d):
            def sub(ref, n, r=r, d=d):
                return ref[0, pl.ds(r, n, stride=d), :]
            q_r = sub(q_ref, rows).astype(jnp.bfloat16)
            k_r = jnp.concatenate([sub(kl_ref, ATTN_SIDE), sub(kc_ref, rows), sub(kr_ref, ATTN_SIDE)],
                                  axis=0).astype(jnp.bfloat16)
            v_r = jnp.concatenate([sub(vl_ref, ATTN_SIDE), sub(vc_ref, rows), sub(vr_ref, ATTN_SIDE)],
                                  axis=0).astype(jnp.bfloat16)
            for b in range(rows // ATTN_QBLK):
                qs = q_r[b * ATTN_QBLK:(b + 1) * ATTN_QBLK]
                ks = k_r[b * ATTN_QBLK:b * ATTN_QBLK + win]
                vs = v_r[b * ATTN_QBLK:b * ATTN_QBLK + win]
                s = lax.dot_general(qs, ks, (((1,), (1,)), ((), ())),
                                    preferred_element_type=jnp.float32) * scale
                key_m = kk + (m0 + (b * ATTN_QBLK - ATTN_SIDE))
                valid = band & (key_m >= m_lo) & (key_m < m_hi)
                s = jnp.where(valid, s, -jnp.inf)
                m = jnp.max(s, axis=-1, keepdims=True)
                p = jnp.exp(s - m)
                denom = jnp.sum(p, axis=-1, keepdims=True)
                o = jnp.dot(p.astype(jnp.bfloat16), vs, preferred_element_type=jnp.float32) / denom
                lse = m + jnp.log(denom)
                dst = pl.ds(b * ATTN_QBLK * d + r, ATTN_QBLK, stride=d)
                o_sc[g, dst, :] = o
                lse_sc[g, dst, :] = jnp.broadcast_to(lse, (ATTN_QBLK, HEAD_DIM))

    lses = [lse_sc[g] for g in range(N_GROUPS)]
    top = functools.reduce(jnp.maximum, lses)
    ws = [jnp.exp(l - top) for l in lses]
    num = sum(w * o_sc[g] for g, w in enumerate(ws))
    o_ref[...] = (num / sum(ws)).astype(o_ref.dtype)


def _attention(qk, v, seq_lo, seq_hi, heads_per_group):
    n_qk, T, _ = qk.shape
    H = heads_per_group
    k_base = n_qk // 2
    chunk = ATTN_CHUNK
    n_chunks = T // chunk
    in_specs = []
    operands = []
    for g, (_, d) in enumerate(DILATION_PAIRS):
        halo = ATTN_SIDE * d
        per = chunk // halo
        n_halo = T // halo

        def center(base, g=g):
            return pl.BlockSpec((1, chunk, HEAD_DIM), lambda c, h, lo, hi: (base + g * H + h, c, 0))

        def left(base, g=g, per=per):
            return pl.BlockSpec((1, halo, HEAD_DIM),
                                lambda c, h, lo, hi: (base + g * H + h, jnp.maximum(c * per - 1, 0), 0))

        def right(base, g=g, per=per, n_halo=n_halo):
            return pl.BlockSpec((1, halo, HEAD_DIM),
                                lambda c, h, lo, hi: (base + g * H + h, jnp.minimum((c + 1) * per, n_halo - 1), 0))

        in_specs += [center(0), center(k_base), left(k_base), right(k_base), center(0), left(0), right(0)]
        operands += [qk, qk, qk, qk, v, v, v]
    grid_spec = pltpu.PrefetchScalarGridSpec(
        num_scalar_prefetch=2,
        grid=(n_chunks, H),
        in_specs=in_specs,
        out_specs=pl.BlockSpec((chunk, HEAD_DIM), lambda c, h, lo, hi: (c, h)),
        scratch_shapes=[pltpu.VMEM((N_GROUPS, chunk, HEAD_DIM), jnp.float32),
                        pltpu.VMEM((N_GROUPS, chunk, HEAD_DIM), jnp.float32)],
    )
    return pl.pallas_call(
        functools.partial(_attn_kernel, chunk=chunk),
        grid_spec=grid_spec,
        out_shape=jax.ShapeDtypeStruct((T, H * HEAD_DIM), jnp.bfloat16),
        compiler_params=_cparams("arbitrary", "arbitrary"),
        name="dilated_attention",
    )(seq_lo, seq_hi, *operands)


def _pool_kernel(lo_ref, hi_ref, uc_ref, ul_ref, ur_ref, w_ref, scale_ref, o_ref, ext_sc, *, tp):
    i = pl.program_id(0)
    t0 = i * tp
    seq_lo = lo_ref[i]
    seq_hi = hi_ref[i]
    gw = w_ref.shape[1]
    pos_l = t0 - POOL_HALO + lax.broadcasted_iota(jnp.int32, (POOL_HALO, 1), 0)
    pos_r = t0 + tp + lax.broadcasted_iota(jnp.int32, (POOL_HALO, 1), 0)
    ext_sc[0:POOL_HALO, :] = jnp.where(pos_l >= seq_lo, ul_ref[...], 0.0)
    ext_sc[POOL_HALO:POOL_HALO + tp, :] = uc_ref[...]
    ext_sc[POOL_HALO + tp:, :] = jnp.where(pos_r < seq_hi, ur_ref[...], 0.0)
    pos = t0 + lax.broadcasted_iota(jnp.int32, (tp, 1), 0)
    for gi, w in enumerate(POOL_WINDOWS):
        cols = slice(gi * gw, (gi + 1) * gw)
        total = None
        for j in range(-(w // 2), w - w // 2):
            piece = ext_sc[POOL_HALO + j:POOL_HALO + j + tp, cols]
            total = piece if total is None else total + piece
        cnt = jnp.minimum(pos + (w - w // 2), seq_hi) - jnp.maximum(pos - w // 2, seq_lo)
        pooled = total / cnt.astype(jnp.float32) - uc_ref[:, cols]
        mixed = jnp.dot(pooled.astype(jnp.bfloat16), w_ref[gi], preferred_element_type=jnp.float32)
        o_ref[:, cols] = (mixed * scale_ref[:, cols]).astype(o_ref.dtype)


def _pool(u, w_pool, pool_scale, seq_lo, seq_hi, tp):
    T, P = u.shape
    per = tp // POOL_HALO
    n_halo = T // POOL_HALO
    grid_spec = pltpu.PrefetchScalarGridSpec(
        num_scalar_prefetch=2,
        grid=(T // tp,),
        in_specs=[
            pl.BlockSpec((tp, P), lambda i, lo, hi: (i, 0)),
            pl.BlockSpec((POOL_HALO, P), lambda i, lo, hi: (jnp.maximum(i * per - 1, 0), 0)),
            pl.BlockSpec((POOL_HALO, P), lambda i, lo, hi: (jnp.minimum((i + 1) * per, n_halo - 1), 0)),
            pl.BlockSpec(w_pool.shape, lambda i, lo, hi: (0, 0, 0)),
            pl.BlockSpec((1, P), lambda i, lo, hi: (0, 0)),
        ],
        out_specs=pl.BlockSpec((tp, P), lambda i, lo, hi: (i, 0)),
        scratch_shapes=[pltpu.VMEM((tp + 2 * POOL_HALO, P), jnp.float32)],
    )
    return pl.pallas_call(
        functools.partial(_pool_kernel, tp=tp),
        grid_spec=grid_spec,
        out_shape=jax.ShapeDtypeStruct((T, P), jnp.bfloat16),
        compiler_params=_cparams("arbitrary"),
        name="multiscale_pool",
    )(seq_lo, seq_hi, u, u, u, w_pool, pool_scale.reshape(1, P))


def _mix_kernel(h_ref, wga_ref, wgb_ref, oa_ref, wa_ref, ob_ref, wb_ref, o_ref):
    h = h_ref[...]
    g_a = jnp.dot(h, wga_ref[...], preferred_element_type=jnp.float32)
    g_b = jnp.dot(h, wgb_ref[...], preferred_element_type=jnp.float32)
    a = jnp.dot(oa_ref[...], wa_ref[...], preferred_element_type=jnp.float32)
    b = jnp.dot(ob_ref[...], wb_ref[...], preferred_element_type=jnp.float32)
    o_ref[...] = (jax.nn.sigmoid(g_a) * a + jax.nn.sigmoid(g_b) * b).astype(o_ref.dtype)


def _mix(h, w_in, ga_col0, o_attn, w_a, o_pool, w_b, tm=512, tn=512):
    T, D = h.shape
    cb_a = ga_col0 // tn
    cb_b = (ga_col0 + D) // tn
    A = o_attn.shape[1]
    P = o_pool.shape[1]
    return pl.pallas_call(
        _mix_kernel,
        grid=(T // tm, D // tn),
        in_specs=[
            pl.BlockSpec((tm, D), lambda i, j: (i, 0)),
            pl.BlockSpec((D, tn), lambda i, j: (0, cb_a + j)),
            pl.BlockSpec((D, tn), lambda i, j: (0, cb_b + j)),
            pl.BlockSpec((tm, A), lambda i, j: (i, 0)),
            pl.BlockSpec((A, tn), lambda i, j: (0, j)),
            pl.BlockSpec((tm, P), lambda i, j: (i, 0)),
            pl.BlockSpec((P, tn), lambda i, j: (0, j)),
        ],
        out_specs=pl.BlockSpec((tm, tn), lambda i, j: (i, j)),
        out_shape=jax.ShapeDtypeStruct((T, D), jnp.bfloat16),
        compiler_params=_cparams("arbitrary", "arbitrary"),
        name="gated_mix",
    )(h, w_in, w_in, o_attn, w_a, o_pool, w_b)


def _out_proj_kernel(m_ref, w_ref, xa_ref, xb_ref, o_ref, *, n_first):
    i = pl.program_id(0)

    def body(x_ref):
        w = w_ref[...]
        half = m_ref.shape[0] // 2
        for rows in (slice(0, half), slice(half, 2 * half)):
            o_ref[rows, :] = x_ref[rows, :] + jnp.dot(m_ref[rows, :], w, preferred_element_type=jnp.float32)

    pl.when(i < n_first)(lambda: body(xa_ref))
    pl.when(i >= n_first)(lambda: body(xb_ref))


def _out_proj(mix, w_out, xa, xb, tm=512, tn=1024):
    T, D = mix.shape
    tn = min(tn, D)
    nj = D // tn
    n_first = xa.shape[0] // tm
    return pl.pallas_call(
        functools.partial(_out_proj_kernel, n_first=n_first),
        grid=(T // tm, nj),
        in_specs=[
            pl.BlockSpec((tm, D), lambda i, j: (i, 0)),
            pl.BlockSpec((D, tn), lambda i, j: (0, j)),
            pl.BlockSpec((tm, tn), lambda i, j: (jnp.minimum(i, n_first - 1), jnp.where(i < n_first, j, nj - 1))),
            pl.BlockSpec((tm, tn), lambda i, j: (jnp.maximum(i - n_first, 0), jnp.where(i >= n_first, j, 0))),
        ],
        out_specs=pl.BlockSpec((tm, tn), lambda i, j: (i, j)),
        out_shape=jax.ShapeDtypeStruct((T, D), jnp.float32),
        compiler_params=_cparams("arbitrary", "arbitrary"),
        name="out_proj_residual",
    )(mix, w_out, xa, xb)


def _router_kernel(x_ref, g_ref, wr_ref, br_ref, hp_ref, idx_ref, gate_ref):
    x = x_ref[...]
    ms = jnp.mean(x * x, axis=-1, keepdims=True)
    h = x * lax.rsqrt(ms + RMS_EPS) * g_ref[...]
    half = h.shape[1] // 2
    hp_ref[...] = _pack_halves(h[:, :half], h[:, half:])
    logits = jnp.dot(h.astype(jnp.bfloat16), wr_ref[...], preferred_element_type=jnp.float32) + br_ref[...]
    n_exp = logits.shape[1]
    lane = lax.broadcasted_iota(jnp.int32, logits.shape, 1)
    vals, idxs = [], []
    for _ in range(TOP_K):
        m = jnp.max(logits, axis=-1, keepdims=True)
        idx = jnp.min(jnp.where(logits == m, lane, n_exp), axis=-1, keepdims=True)
        vals.append(m)
        idxs.append(idx)
        logits = jnp.where(lane == idx, -jnp.inf, logits)
    exps = [jnp.exp(v - vals[0]) for v in vals]
    total = sum(exps)
    for k in range(TOP_K):
        idx_ref[:, k:k + 1] = idxs[k]
        gate_ref[:, k:k + 1] = exps[k] / total


def _router(x1, gain, w_router, b_router, tr=256):
    T, D = x1.shape
    E = w_router.shape[1]
    return pl.pallas_call(
        _router_kernel,
        grid=(T // tr,),
        in_specs=[
            pl.BlockSpec((tr, D), lambda i: (i, 0)),
            pl.BlockSpec((1, D), lambda i: (0, 0)),
            pl.BlockSpec((D, E), lambda i: (0, 0)),
            pl.BlockSpec((1, E), lambda i: (0, 0)),
        ],
        out_specs=[
            pl.BlockSpec((tr, D // 2), lambda i: (i, 0)),
            pl.BlockSpec((tr, TOP_K), lambda i: (i, 0)),
            pl.BlockSpec((tr, TOP_K), lambda i: (i, 0)),
        ],
        out_shape=[
            jax.ShapeDtypeStruct((T, D // 2), jnp.uint32),
            jax.ShapeDtypeStruct((T, TOP_K), jnp.int32),
            jax.ShapeDtypeStruct((T, TOP_K), jnp.float32),
        ],
        compiler_params=_cparams("arbitrary"),
        name="router_topk",
    )(x1, gain.reshape(1, D), w_router, b_router.reshape(1, E))


def _routing_tables(top_idx, n_experts):
    e = top_idx.reshape(-1)
    onehot = (e[:, None] == jnp.arange(n_experts, dtype=jnp.int32)[None, :]).astype(jnp.int32)
    csum = jnp.cumsum(onehot, axis=0)
    counts = csum[-1]
    tiles_per_expert = (counts + EXPERT_TILE - 1) // EXPERT_TILE
    padded = tiles_per_expert * EXPERT_TILE
    pend = jnp.cumsum(padded)
    pstart = pend - padded
    pos = jnp.sum(onehot * (csum - 1 + pstart[None, :]), axis=1).astype(jnp.int32)
    pad_lo = (pstart + counts).astype(jnp.int32)
    return (pos, pad_lo, pend.astype(jnp.int32), tiles_per_expert.astype(jnp.int32),
            (pstart // EXPERT_TILE).astype(jnp.int32))


def _dispatch_kernel(pos_ref, pad_lo_ref, pad_hi_ref, hp_ref, xs_ref, zero_sc, sem, *, n_experts):
    step = pl.program_id(0)
    tt = hp_ref.shape[0]

    def row_copy(tok, dst_row):
        return pltpu.make_async_copy(hp_ref.at[pl.ds(tok, 1)], xs_ref.at[pl.ds(dst_row, 1)], sem)

    def issue(tok, carry):
        for k in range(TOP_K):
            row_copy(tok, pos_ref[0, 0, tok * TOP_K + k]).start()
        return carry

    lax.fori_loop(0, tt, issue, 0, unroll=4)

    def drain(tok, carry):
        for k in range(TOP_K):
            row_copy(0, 0).wait()
        return carry

    lax.fori_loop(0, tt, drain, 0, unroll=4)

    @pl.when(step == pl.num_programs(0) - 1)
    def _():
        zero_sc[...] = jnp.zeros_like(zero_sc)

        def pad_copy(dst_row):
            return pltpu.make_async_copy(zero_sc.at[pl.ds(0, 1)], xs_ref.at[pl.ds(dst_row, 1)], sem)

        for e in range(n_experts):
            lo = pad_lo_ref[e]
            hi = pad_hi_ref[e]

            def fill(p, carry):
                pad_copy(p).start()
                return carry

            lax.fori_loop(lo, hi, fill, 0)

            def fill_wait(p, carry):
                pad_copy(0).wait()
                return carry

            lax.fori_loop(lo, hi, fill_wait, 0)


def _dispatch(hp, pos, pad_lo, pad_hi, n_slots, tt=DISPATCH_TILE):
    T, W = hp.shape
    n_experts = pad_lo.shape[0]
    grid_spec = pltpu.PrefetchScalarGridSpec(
        num_scalar_prefetch=0,
        grid=(T // tt,),
        in_specs=[
            pl.BlockSpec((1, 1, tt * TOP_K), lambda s: (s, 0, 0), memory_space=pltpu.SMEM),
            pl.BlockSpec(memory_space=pltpu.SMEM),
            pl.BlockSpec(memory_space=pltpu.SMEM),
            pl.BlockSpec((tt, W), lambda s: (s, 0)),
        ],
        out_specs=pl.BlockSpec(memory_space=pl.ANY),
        scratch_shapes=[pltpu.VMEM((8, W), jnp.uint32), pltpu.SemaphoreType.DMA(())],
    )
    return pl.pallas_call(
        functools.partial(_dispatch_kernel, n_experts=n_experts),
        grid_spec=grid_spec,
        out_shape=jax.ShapeDtypeStruct((n_slots, W), jnp.uint32),
        compiler_params=_cparams("arbitrary"),
        name="moe_dispatch",
    )(pos.reshape(T // tt, 1, tt * TOP_K), pad_lo, pad_hi, hp)


def _expert_steps(tiles_per_expert, tile_base, nj, n_tiles):
    n_steps = n_tiles * nj
    steps_e = tiles_per_expert * nj
    send = jnp.cumsum(steps_e)
    total = send[-1]
    step = jnp.arange(n_steps, dtype=jnp.int32)
    s = jnp.minimum(step, total - 1)
    e = jnp.sum((send[None, :] <= s[:, None]).astype(jnp.int32), axis=1)
    local = s - (send - steps_e)[e]
    nt = jnp.maximum(tiles_per_expert[e], 1)
    j = local // nt
    tile = tile_base[e] + (local - j * nt)
    used = (step < total).astype(jnp.int32)
    return e.astype(jnp.int32), j.astype(jnp.int32), tile.astype(jnp.int32), used


def _dot_f32_weight(x_parts, w_ref, part_rows):
    acc = None
    for p, x in enumerate(x_parts):
        for k0 in range(0, part_rows, WEIGHT_CAST_ROWS):
            w = w_ref[0, p * part_rows + k0:p * part_rows + k0 + WEIGHT_CAST_ROWS, :].astype(jnp.bfloat16)
            d = jnp.dot(x[:, k0:k0 + WEIGHT_CAST_ROWS], w, preferred_element_type=jnp.float32)
            acc = d if acc is None else acc + d
    return acc


def _gate_up_kernel(e_ref, j_ref, t_ref, used_ref, xs_ref, wg_ref, wu_ref, bg_ref, bu_ref, o_ref):
    @pl.when(used_ref[pl.program_id(0)] == 1)
    def _():
        lo, hi = _unpack_halves(xs_ref[...])
        x_parts = (lo.astype(jnp.bfloat16), hi.astype(jnp.bfloat16))
        half = lo.shape[1]
        gate = jnp.minimum(_dot_f32_weight(x_parts, wg_ref, half) + bg_ref[0], SWIGLU_LIMIT)
        lin = jnp.clip(_dot_f32_weight(x_parts, wu_ref, half) + bu_ref[0], -SWIGLU_LIMIT, SWIGLU_LIMIT)
        o_ref[...] = (gate * jax.nn.sigmoid(SWIGLU_ALPHA * gate) * (lin + 1.0)).astype(o_ref.dtype)


def _expert_gate_up(xs, w_gu, b_gu, tiles_per_expert, tile_base, tn=512):
    n_slots, half = xs.shape
    E, D, F2 = w_gu.shape
    F = F2 // 2
    tn = min(tn, F)
    nj = F // tn
    n_tiles = n_slots // EXPERT_TILE
    tables = _expert_steps(tiles_per_expert, tile_base, nj, n_tiles)

    def wmap(off):
        return lambda s, e, j, t, u: (e[s], 0, off + j[s])

    grid_spec = pltpu.PrefetchScalarGridSpec(
        num_scalar_prefetch=4,
        grid=(n_tiles * nj,),
        in_specs=[
            pl.BlockSpec((EXPERT_TILE, half), lambda s, e, j, t, u: (t[s], 0)),
            pl.BlockSpec((1, D, tn), wmap(0)),
            pl.BlockSpec((1, D, tn), wmap(nj)),
            pl.BlockSpec((1, 1, tn), wmap(0)),
            pl.BlockSpec((1, 1, tn), wmap(nj)),
        ],
        out_specs=pl.BlockSpec((EXPERT_TILE, tn), lambda s, e, j, t, u: (t[s], j[s])),
    )
    return pl.pallas_call(
        _gate_up_kernel,
        grid_spec=grid_spec,
        out_shape=jax.ShapeDtypeStruct((n_slots, F), jnp.bfloat16),
        compiler_params=_cparams("arbitrary"),
        name="expert_gate_up",
    )(*tables, xs, w_gu, w_gu, b_gu.reshape(E, 1, F2), b_gu.reshape(E, 1, F2))


def _down_kernel(e_ref, j_ref, t_ref, used_ref, a_ref, wlo_ref, whi_ref, blo_ref, bhi_ref, o_ref):
    @pl.when(used_ref[pl.program_id(0)] == 1)
    def _():
        a = (a_ref[...],)
        rows = a_ref.shape[1]
        y_lo = _dot_f32_weight(a, wlo_ref, rows) + blo_ref[0]
        y_hi = _dot_f32_weight(a, whi_ref, rows) + bhi_ref[0]
        o_ref[...] = _pack_halves(y_lo, y_hi)


def _expert_down(act, w_down, b_down, tiles_per_expert, tile_base, tn=512):
    n_slots, F = act.shape
    E, _, D = w_down.shape
    half = D // 2
    tn = min(tn, half)
    nj = half // tn
    n_tiles = n_slots // EXPERT_TILE
    tables = _expert_steps(tiles_per_expert, tile_base, nj, n_tiles)

    def wmap(off):
        return lambda s, e, j, t, u: (e[s], 0, off + j[s])

    grid_spec = pltpu.PrefetchScalarGridSpec(
        num_scalar_prefetch=4,
        grid=(n_tiles * nj,),
        in_specs=[
            pl.BlockSpec((EXPERT_TILE, F), lambda s, e, j, t, u: (t[s], 0)),
            pl.BlockSpec((1, F, tn), wmap(0)),
            pl.BlockSpec((1, F, tn), wmap(nj)),
            pl.BlockSpec((1, 1, tn), wmap(0)),
            pl.BlockSpec((1, 1, tn), wmap(nj)),
        ],
        out_specs=pl.BlockSpec((EXPERT_TILE, tn), lambda s, e, j, t, u: (t[s], j[s])),
    )
    return pl.pallas_call(
        _down_kernel,
        grid_spec=grid_spec,
        out_shape=jax.ShapeDtypeStruct((n_slots, half), jnp.uint32),
        compiler_params=_cparams("arbitrary"),
        name="expert_down",
    )(*tables, act, w_down, w_down, b_down.reshape(E, 1, D), b_down.reshape(E, 1, D))


def _combine_kernel(pos_ref, x_ref, gate_ref, ys_ref, o_ref, buf, sem, *, tt):
    def row_copy(src_row, k, tok):
        return pltpu.make_async_copy(ys_ref.at[pl.ds(src_row, 1)], buf.at[k, pl.ds(tok, 1)], sem)

    def issue(tok, carry):
        for k in range(TOP_K):
            row_copy(pos_ref[0, 0, tok * TOP_K + k], k, tok).start()
        return carry

    lax.fori_loop(0, tt, issue, 0, unroll=4)

    def drain(tok, carry):
        for k in range(TOP_K):
            row_copy(0, k, 0).wait()
        return carry

    lax.fori_loop(0, tt, drain, 0, unroll=4)

    half = buf.shape[2]
    acc_lo = x_ref[:, :half]
    acc_hi = x_ref[:, half:]
    for k in range(TOP_K):
        lo, hi = _unpack_halves(buf[k])
        gk = gate_ref[:, k:k + 1]
        acc_lo = acc_lo + gk * lo
        acc_hi = acc_hi + gk * hi
    o_ref[:, :half] = acc_lo
    o_ref[:, half:] = acc_hi


def _combine(x1, gates, pos, ys, row0, n_rows, tt=COMBINE_TILE):
    T, D = x1.shape
    half = D // 2
    b0 = row0 // tt
    grid_spec = pltpu.PrefetchScalarGridSpec(
        num_scalar_prefetch=0,
        grid=(n_rows // tt,),
        in_specs=[
            pl.BlockSpec((1, 1, tt * TOP_K), lambda i: (b0 + i, 0, 0), memory_space=pltpu.SMEM),
            pl.BlockSpec((tt, D), lambda i: (b0 + i, 0)),
            pl.BlockSpec((tt, TOP_K), lambda i: (b0 + i, 0)),
            pl.BlockSpec(memory_space=pl.ANY),
        ],
        out_specs=pl.BlockSpec((tt, D), lambda i: (i, 0)),
        scratch_shapes=[pltpu.VMEM((TOP_K, tt, half), jnp.uint32), pltpu.SemaphoreType.DMA(())],
    )
    return pl.pallas_call(
        functools.partial(_combine_kernel, tt=tt),
        grid_spec=grid_spec,
        out_shape=jax.ShapeDtypeStruct((n_rows, D), jnp.float32),
        compiler_params=_cparams("arbitrary"),
        name="moe_combine",
    )(pos.reshape(T // tt, 1, tt * TOP_K), x1, gates, ys)


def _sequence_tables(seq_lens, tile):
    lo, hi, start = [], [], 0
    for n in seq_lens:
        assert n % tile == 0
        lo += [start] * (n // tile)
        hi += [start + n] * (n // tile)
        start += n
    return jnp.asarray(np.array(lo, np.int32)), jnp.asarray(np.array(hi, np.int32))


def _rope_tables(seq_lens):
    half = HEAD_DIM // 2
    inv_freq = ROPE_THETA ** (-jnp.arange(half, dtype=jnp.float32) / half)
    pos = jnp.concatenate([jnp.arange(n, dtype=jnp.float32) for n in seq_lens])
    ang = pos[:, None] * inv_freq[None, :]
    cos, sin = jnp.cos(ang), jnp.sin(ang)
    return jnp.concatenate([cos, cos], axis=1), jnp.concatenate([-sin, sin], axis=1)


def _encoder_layer(xa, xb, seq_lens, norm_mix, w_in, q_norm, k_norm, w_pool, pool_scale, w_branch_a,
                   w_branch_b, w_out, norm_ffn, w_router, b_router, w_gate_up, b_gate_up, w_down, b_down):
    Ta, D = xa.shape
    Tb = xb.shape[0]
    T = Ta + Tb
    bf16 = jnp.bfloat16
    A_out = w_branch_a.shape[0]
    H = A_out // HEAD_DIM
    A = N_GROUPS * A_out
    P = w_branch_b.shape[0]
    E = w_router.shape[1]

    w_in_b = w_in.astype(bf16)
    gain_cols = jnp.concatenate([jnp.tile(q_norm, (1, H)).reshape(1, A),
                                 jnp.tile(k_norm, (1, H)).reshape(1, A)], axis=1)
    cos_t, sin_t = _rope_tables(seq_lens)

    h = _rmsnorm(xa, xb, norm_mix)
    qk = _qk_proj(h, w_in_b, gain_cols, cos_t, sin_t, 2 * A)
    v = _v_proj(h, w_in_b, 2 * A, A)
    u = _u_proj(h, w_in_b, 3 * A, P)

    lo_a, hi_a = _sequence_tables(seq_lens, ATTN_CHUNK)
    o_attn = _attention(qk, v, lo_a, hi_a, H)
    tp = 512
    lo_p, hi_p = _sequence_tables(seq_lens, tp)
    o_pool = _pool(u, w_pool.astype(bf16), pool_scale, lo_p, hi_p, tp)

    mix = _mix(h, w_in_b, 3 * A + P, o_attn, w_branch_a.astype(bf16), o_pool, w_branch_b.astype(bf16))
    x1 = _out_proj(mix, w_out.astype(bf16), xa, xb)

    hp, top_idx, gates = _router(x1, norm_ffn, w_router.astype(bf16), b_router)
    n_assign = T * TOP_K
    n_tiles = n_assign // EXPERT_TILE + E
    pos, pad_lo, pad_hi, tiles_per_expert, tile_base = _routing_tables(top_idx, E)
    xs = _dispatch(hp, pos, pad_lo, pad_hi, n_tiles * EXPERT_TILE)
    act = _expert_gate_up(xs, w_gate_up, b_gate_up, tiles_per_expert, tile_base)
    ys = _expert_down(act, w_down, b_down, tiles_per_expert, tile_base)
    return _combine(x1, gates, pos, ys, 0, Ta), _combine(x1, gates, pos, ys, Ta, Tb)


def kernel(x_prompt, x_sample, norm_mix, w_in, q_norm, k_norm, w_pool, pool_scale, w_branch_a, w_branch_b,
           w_out, norm_ffn, w_router, b_router, w_gate_up, b_gate_up, w_down, b_down):
    depth = norm_mix.shape[0]
    D = x_prompt.shape[-1]
    seq_lens = (x_prompt.shape[1],) * x_prompt.shape[0] + (x_sample.shape[1],) * x_sample.shape[0]
    xa, xb = x_prompt.reshape(-1, D), x_sample.reshape(-1, D)
    for layer in range(depth):
        xa, xb = _encoder_layer(xa, xb, seq_lens, norm_mix[layer], w_in[layer], q_norm[layer], k_norm[layer],
                                w_pool[layer], pool_scale[layer], w_branch_a[layer], w_branch_b[layer],
                                w_out[layer], norm_ffn[layer], w_router[layer], b_router[layer],
                                w_gate_up[layer], b_gate_up[layer], w_down[layer], b_down[layer])
    return xa.reshape(x_prompt.shape), xb.reshape(x_sample.shape)
```

```python
import functools

import numpy as np
import jax
import jax.numpy as jnp
from jax import lax
from jax.experimental import pallas as pl
from jax.experimental.pallas import tpu as pltpu

HEAD_DIM = 128
DILATION_PAIRS = ((128, 1), (512, 4), (2048, 16))
N_GROUPS = len(DILATION_PAIRS)
ATTN_SIDE = 64
ROPE_THETA = 10000.0
POOL_WINDOWS = (2, 4, 8, 16)
POOL_HALO = 8
TOP_K = 4
SWIGLU_LIMIT = 7.0
SWIGLU_ALPHA = 1.702
RMS_EPS = 1e-6

VMEM_LIMIT_BYTES = 56 * 1024 * 1024
PROJ_ROWS = 1024
PROJ_ROW_SPLIT = 512
QK_ROW_SPLIT = 256

ATTN_CHUNK = 2048
ATTN_QBLK = 128
EXPERT_TILE = 512
WEIGHT_CAST_ROWS = 512
DISPATCH_TILE = 256
COMBINE_TILE = 256


def _cparams(*sem):
    return pltpu.CompilerParams(dimension_semantics=sem, vmem_limit_bytes=VMEM_LIMIT_BYTES)


def _pack_halves(lo, hi):
    lo_bits = lax.bitcast_convert_type(lo.astype(jnp.bfloat16).astype(jnp.float32), jnp.uint32)
    hi_bits = lax.bitcast_convert_type(hi.astype(jnp.bfloat16).astype(jnp.float32), jnp.uint32)
    return (lo_bits >> 16) | (hi_bits & jnp.uint32(0xFFFF0000))


def _unpack_halves(packed):
    lo = lax.bitcast_convert_type(packed << 16, jnp.float32)
    hi = lax.bitcast_convert_type(packed & jnp.uint32(0xFFFF0000), jnp.float32)
    return lo, hi


def _rmsnorm_kernel(xa_ref, xb_ref, g_ref, o_ref, *, n_first):
    def body(x_ref):
        x = x_ref[...]
        ms = jnp.mean(x * x, axis=-1, keepdims=True)
        o_ref[...] = (x * lax.rsqrt(ms + RMS_EPS) * g_ref[...]).astype(o_ref.dtype)

    pl.when(pl.program_id(0) < n_first)(lambda: body(xa_ref))
    pl.when(pl.program_id(0) >= n_first)(lambda: body(xb_ref))


def _rmsnorm(xa, xb, gain, tm=256):
    D = xa.shape[1]
    T = xa.shape[0] + xb.shape[0]
    n_first = xa.shape[0] // tm
    return pl.pallas_call(
        functools.partial(_rmsnorm_kernel, n_first=n_first),
        grid=(T // tm,),
        in_specs=[pl.BlockSpec((tm, D), lambda i: (jnp.minimum(i, n_first - 1), 0)),
                  pl.BlockSpec((tm, D), lambda i: (jnp.maximum(i - n_first, 0), 0)),
                  pl.BlockSpec((1, D), lambda i: (0, 0))],
        out_specs=pl.BlockSpec((tm, D), lambda i: (i, 0)),
        out_shape=jax.ShapeDtypeStruct((T, D), jnp.bfloat16),
        compiler_params=_cparams("arbitrary"),
        name="rmsnorm",
    )(xa, xb, gain.reshape(1, D))


def _qk_proj_kernel(h_ref, w_ref, gain_ref, cos_ref, sin_ref, o_ref):
    w = w_ref[...]
    tm = h_ref.shape[0]
    for r0 in range(0, tm, QK_ROW_SPLIT):
        rows = slice(r0, r0 + QK_ROW_SPLIT)
        acc = jnp.dot(h_ref[rows, :], w, preferred_element_type=jnp.float32)
        cos = cos_ref[rows, :]
        sin = sin_ref[rows, :]
        for c in range(acc.shape[1] // HEAD_DIM):
            t = acc[:, c * HEAD_DIM:(c + 1) * HEAD_DIM]
            ms = jnp.mean(t * t, axis=-1, keepdims=True)
            t = t * lax.rsqrt(ms + RMS_EPS) * gain_ref[:, c * HEAD_DIM:(c + 1) * HEAD_DIM]
            o_ref[c, rows, :] = t * cos + pltpu.roll(t, HEAD_DIM // 2, axis=1) * sin


def _qk_proj(h, w_in, gain_cols, cos_t, sin_t, ncols, tm=PROJ_ROWS, tn=512):
    T, D = h.shape
    n_heads = ncols // HEAD_DIM
    return pl.pallas_call(
        _qk_proj_kernel,
        grid=(T // tm, ncols // tn),
        in_specs=[
            pl.BlockSpec((tm, D), lambda i, j: (i, 0)),
            pl.BlockSpec((D, tn), lambda i, j: (0, j)),
            pl.BlockSpec((1, tn), lambda i, j: (0, j)),
            pl.BlockSpec((tm, HEAD_DIM), lambda i, j: (i, 0)),
            pl.BlockSpec((tm, HEAD_DIM), lambda i, j: (i, 0)),
        ],
        out_specs=pl.BlockSpec((tn // HEAD_DIM, tm, HEAD_DIM), lambda i, j: (j, i, 0)),
        out_shape=jax.ShapeDtypeStruct((n_heads, T, HEAD_DIM), jnp.float32),
        compiler_params=_cparams("arbitrary", "arbitrary"),
        name="qk_proj",
    )(h, w_in, gain_cols, cos_t, sin_t)


def _v_proj_kernel(h_ref, w_ref, o_ref):
    w = w_ref[...]
    for r0 in range(0, h_ref.shape[0], PROJ_ROW_SPLIT):
        rows = slice(r0, r0 + PROJ_ROW_SPLIT)
        acc = jnp.dot(h_ref[rows, :], w, preferred_element_type=jnp.float32)
        for c in range(acc.shape[1] // HEAD_DIM):
            o_ref[c, rows, :] = acc[:, c * HEAD_DIM:(c + 1) * HEAD_DIM]


def _v_proj(h, w_in, col0, ncols, tm=PROJ_ROWS, tn=512):
    T, D = h.shape
    cb0 = col0 // tn
    return pl.pallas_call(
        _v_proj_kernel,
        grid=(T // tm, ncols // tn),
        in_specs=[
            pl.BlockSpec((tm, D), lambda i, j: (i, 0)),
            pl.BlockSpec((D, tn), lambda i, j: (0, cb0 + j)),
        ],
        out_specs=pl.BlockSpec((tn // HEAD_DIM, tm, HEAD_DIM), lambda i, j: (j, i, 0)),
        out_shape=jax.ShapeDtypeStruct((ncols // HEAD_DIM, T, HEAD_DIM), jnp.float32),
        compiler_params=_cparams("arbitrary", "arbitrary"),
        name="v_proj",
    )(h, w_in)


def _u_proj_kernel(h_ref, w_ref, o_ref):
    w = w_ref[...]
    for r0 in range(0, h_ref.shape[0], PROJ_ROW_SPLIT):
        rows = slice(r0, r0 + PROJ_ROW_SPLIT)
        o_ref[rows, :] = jnp.dot(h_ref[rows, :], w, preferred_element_type=jnp.float32)


def _u_proj(h, w_in, col0, ncols, tm=PROJ_ROWS, tn=512):
    T, D = h.shape
    cb0 = col0 // tn
    return pl.pallas_call(
        _u_proj_kernel,
        grid=(T // tm, ncols // tn),
        in_specs=[
            pl.BlockSpec((tm, D), lambda i, j: (i, 0)),
            pl.BlockSpec((D, tn), lambda i, j: (0, cb0 + j)),
        ],
        out_specs=pl.BlockSpec((tm, tn), lambda i, j: (i, j)),
        out_shape=jax.ShapeDtypeStruct((T, ncols), jnp.float32),
        compiler_params=_cparams("arbitrary", "arbitrary"),
        name="u_proj",
    )(h, w_in)


def _attn_kernel(lo_ref, hi_ref, *refs, chunk):
    in_refs = refs[:7 * N_GROUPS]
    o_ref = refs[7 * N_GROUPS]
    o_sc, lse_sc = refs[7 * N_GROUPS + 1:]
    c = pl.program_id(0)
    t0 = c * chunk
    seq_lo = lo_ref[c]
    seq_hi = hi_ref[c]
    win = ATTN_QBLK + 2 * ATTN_SIDE
    qq = lax.broadcasted_iota(jnp.int32, (ATTN_QBLK, win), 0)
    kk = lax.broadcasted_iota(jnp.int32, (ATTN_QBLK, win), 1)
    band = jnp.abs(kk - ATTN_SIDE - qq) <= ATTN_SIDE
    scale = HEAD_DIM ** -0.5

    for g, (_, d) in enumerate(DILATION_PAIRS):
        q_ref, kc_ref, kl_ref, kr_ref, vc_ref, vl_ref, vr_ref = in_refs[7 * g:7 * g + 7]
        shift = d.bit_length() - 1
        rows = chunk // d
        m0 = t0 >> shift
        m_lo = seq_lo >> shift
        m_hi = seq_hi >> shift
        for r in range(d):
            def sub(ref, n, r=r, d=d):
                return ref[0, pl.ds(r, n, stride=d), :]
            q_r = sub(q_ref, rows).astype(jnp.bfloat16)
            k_r = jnp.concatenate([sub(kl_ref, ATTN_SIDE), sub(kc_ref, rows), sub(kr_ref, ATTN_SIDE)],
                                  axis=0).astype(jnp.bfloat16)
            v_r = jnp.concatenate([sub(vl_ref, ATTN_SIDE), sub(vc_ref, rows), sub(vr_ref, ATTN_SIDE)],
                                  axis=0).astype(jnp.bfloat16)
            for b in range(rows // ATTN_QBLK):
                qs = q_r[b * ATTN_QBLK:(b + 1) * ATTN_QBLK]
                ks = k_r[b * ATTN_QBLK:b * ATTN_QBLK + win]
                vs = v_r[b * ATTN_QBLK:b * ATTN_QBLK + win]
                s = lax.dot_general(qs, ks, (((1,), (1,)), ((), ())),
                                    preferred_element_type=jnp.float32) * scale
                key_m = kk + (m0 + (b * ATTN_QBLK - ATTN_SIDE))
                valid = band & (key_m >= m_lo) & (key_m < m_hi)
                s = jnp.where(valid, s, -jnp.inf)
                m = jnp.max(s, axis=-1, keepdims=True)
                p = jnp.exp(s - m)
                denom = jnp.sum(p, axis=-1, keepdims=True)
                o = jnp.dot(p.astype(jnp.bfloat16), vs, preferred_element_type=jnp.float32) / denom
                lse = m + jnp.log(denom)
                dst = pl.ds(b * ATTN_QBLK * d + r, ATTN_QBLK, stride=d)
                o_sc[g, dst, :] = o
                lse_sc[g, dst, :] = jnp.broadcast_to(lse, (ATTN_QBLK, HEAD_DIM))

    lses = [lse_sc[g] for g in range(N_GROUPS)]
    top = functools.reduce(jnp.maximum, lses)
    ws = [jnp.exp(l - top) for l in lses]
    num = sum(w * o_sc[g] for g, w in enumerate(ws))
    o_ref[...] = (num / sum(ws)).astype(o_ref.dtype)


def _attention(qk, v, seq_lo, seq_hi, heads_per_group):
    n_qk, T, _ = qk.shape
    H = heads_per_group
    k_base = n_qk // 2
    chunk = ATTN_CHUNK
    n_chunks = T // chunk
    in_specs = []
    operands = []
    for g, (_, d) in enumerate(DILATION_PAIRS):
        halo = ATTN_SIDE * d
        per = chunk // halo
        n_halo = T // halo

        def center(base, g=g):
            return pl.BlockSpec((1, chunk, HEAD_DIM), lambda c, h, lo, hi: (base + g * H + h, c, 0))

        def left(base, g=g, per=per):
            return pl.BlockSpec((1, halo, HEAD_DIM),
                                lambda c, h, lo, hi: (base + g * H + h, jnp.maximum(c * per - 1, 0), 0))

        def right(base, g=g, per=per, n_halo=n_halo):
            return pl.BlockSpec((1, halo, HEAD_DIM),
                                lambda c, h, lo, hi: (base + g * H + h, jnp.minimum((c + 1) * per, n_halo - 1), 0))

        in_specs += [center(0), center(k_base), left(k_base), right(k_base), center(0), left(0), right(0)]
        operands += [qk, qk, qk, qk, v, v, v]
    grid_spec = pltpu.PrefetchScalarGridSpec(
        num_scalar_prefetch=2,
        grid=(n_chunks, H),
        in_specs=in_specs,
        out_specs=pl.BlockSpec((chunk, HEAD_DIM), lambda c, h, lo, hi: (c, h)),
        scratch_shapes=[pltpu.VMEM((N_GROUPS, chunk, HEAD_DIM), jnp.float32),
                        pltpu.VMEM((N_GROUPS, chunk, HEAD_DIM), jnp.float32)],
    )
    return pl.pallas_call(
        functools.partial(_attn_kernel, chunk=chunk),
        grid_spec=grid_spec,
        out_shape=jax.ShapeDtypeStruct((T, H * HEAD_DIM), jnp.bfloat16),
        compiler_params=_cparams("arbitrary", "arbitrary"),
        name="dilated_attention",
    )(seq_lo, seq_hi, *operands)


def _pool_kernel(lo_ref, hi_ref, uc_ref, ul_ref, ur_ref, w_ref, scale_ref, o_ref, ext_sc, *, tp):
    i = pl.program_id(0)
    t0 = i * tp
    seq_lo = lo_ref[i]
    seq_hi = hi_ref[i]
    gw = w_ref.shape[1]
    pos_l = t0 - POOL_HALO + lax.broadcasted_iota(jnp.int32, (POOL_HALO, 1), 0)
    pos_r = t0 + tp + lax.broadcasted_iota(jnp.int32, (POOL_HALO, 1), 0)
    ext_sc[0:POOL_HALO, :] = jnp.where(pos_l >= seq_lo, ul_ref[...], 0.0)
    ext_sc[POOL_HALO:POOL_HALO + tp, :] = uc_ref[...]
    ext_sc[POOL_HALO + tp:, :] = jnp.where(pos_r < seq_hi, ur_ref[...], 0.0)
    pos = t0 + lax.broadcasted_iota(jnp.int32, (tp, 1), 0)
    for gi, w in enumerate(POOL_WINDOWS):
        cols = slice(gi * gw, (gi + 1) * gw)
        total = None
        for j in range(-(w // 2), w - w // 2):
            piece = ext_sc[POOL_HALO + j:POOL_HALO + j + tp, cols]
            total = piece if total is None else total + piece
        cnt = jnp.minimum(pos + (w - w // 2), seq_hi) - jnp.maximum(pos - w // 2, seq_lo)
        pooled = total / cnt.astype(jnp.float32) - uc_ref[:, cols]
        mixed = jnp.dot(pooled.astype(jnp.bfloat16), w_ref[gi], preferred_element_type=jnp.float32)
        o_ref[:, cols] = (mixed * scale_ref[:, cols]).astype(o_ref.dtype)


def _pool(u, w_pool, pool_scale, seq_lo, seq_hi, tp):
    T, P = u.shape
    per = tp // POOL_HALO
    n_halo = T // POOL_HALO
    grid_spec = pltpu.PrefetchScalarGridSpec(
        num_scalar_prefetch=2,
        grid=(T // tp,),
        in_specs=[
            pl.BlockSpec((tp, P), lambda i, lo, hi: (i, 0)),
            pl.BlockSpec((POOL_HALO, P), lambda i, lo, hi: (jnp.maximum(i * per - 1, 0), 0)),
            pl.BlockSpec((POOL_HALO, P), lambda i, lo, hi: (jnp.minimum((i + 1) * per, n_halo - 1), 0)),
            pl.BlockSpec(w_pool.shape, lambda i, lo, hi: (0, 0, 0)),
            pl.BlockSpec((1, P), lambda i, lo, hi: (0, 0)),
        ],
        out_specs=pl.BlockSpec((tp, P), lambda i, lo, hi: (i, 0)),
        scratch_shapes=[pltpu.VMEM((tp + 2 * POOL_HALO, P), jnp.float32)],
    )
    return pl.pallas_call(
        functools.partial(_pool_kernel, tp=tp),
        grid_spec=grid_spec,
        out_shape=jax.ShapeDtypeStruct((T, P), jnp.bfloat16),
        compiler_params=_cparams("arbitrary"),
        name="multiscale_pool",
    )(seq_lo, seq_hi, u, u, u, w_pool, pool_scale.reshape(1, P))


def _mix_kernel(h_ref, wga_ref, wgb_ref, oa_ref, wa_ref, ob_ref, wb_ref, o_ref):
    h = h_ref[...]
    g_a = jnp.dot(h, wga_ref[...], preferred_element_type=jnp.float32)
    g_b = jnp.dot(h, wgb_ref[...], preferred_element_type=jnp.float32)
    a = jnp.dot(oa_ref[...], wa_ref[...], preferred_element_type=jnp.float32)
    b = jnp.dot(ob_ref[...], wb_ref[...], preferred_element_type=jnp.float32)
    o_ref[...] = (jax.nn.sigmoid(g_a) * a + jax.nn.sigmoid(g_b) * b).astype(o_ref.dtype)


def _mix(h, w_in, ga_col0, o_attn, w_a, o_pool, w_b, tm=512, tn=512):
    T, D = h.shape
    cb_a = ga_col0 // tn
    cb_b = (ga_col0 + D) // tn
    A = o_attn.shape[1]
    P = o_pool.shape[1]
    return pl.pallas_call(
        _mix_kernel,
        grid=(T // tm, D // tn),
        in_specs=[
            pl.BlockSpec((tm, D), lambda i, j: (i, 0)),
            pl.BlockSpec((D, tn), lambda i, j: (0, cb_a + j)),
            pl.BlockSpec((D, tn), lambda i, j: (0, cb_b + j)),
            pl.BlockSpec((tm, A), lambda i, j: (i, 0)),
            pl.BlockSpec((A, tn), lambda i, j: (0, j)),
            pl.BlockSpec((tm, P), lambda i, j: (i, 0)),
            pl.BlockSpec((P, tn), lambda i, j: (0, j)),
        ],
        out_specs=pl.BlockSpec((tm, tn), lambda i, j: (i, j)),
        out_shape=jax.ShapeDtypeStruct((T, D), jnp.bfloat16),
        compiler_params=_cparams("arbitrary", "arbitrary"),
        name="gated_mix",
    )(h, w_in, w_in, o_attn, w_a, o_pool, w_b)


def _out_proj_kernel(m_ref, w_ref, xa_ref, xb_ref, o_ref, *, n_first):
    i = pl.program_id(0)

    def body(x_ref):
        w = w_ref[...]
        half = m_ref.shape[0] // 2
        for rows in (slice(0, half), slice(half, 2 * half)):
            o_ref[rows, :] = x_ref[rows, :] + jnp.dot(m_ref[rows, :], w, preferred_element_type=jnp.float32)

    pl.when(i < n_first)(lambda: body(xa_ref))
    pl.when(i >= n_first)(lambda: body(xb_ref))


def _out_proj(mix, w_out, xa, xb, tm=PROJ_ROWS, tn=512):
    T, D = mix.shape
    tn = min(tn, D)
    nj = D // tn
    n_first = xa.shape[0] // tm
    return pl.pallas_call(
        functools.partial(_out_proj_kernel, n_first=n_first),
        grid=(T // tm, nj),
        in_specs=[
            pl.BlockSpec((tm, D), lambda i, j: (i, 0)),
            pl.BlockSpec((D, tn), lambda i, j: (0, j)),
            pl.BlockSpec((tm, tn), lambda i, j: (jnp.minimum(i, n_first - 1), jnp.where(i < n_first, j, nj - 1))),
            pl.BlockSpec((tm, tn), lambda i, j: (jnp.maximum(i - n_first, 0), jnp.where(i >= n_first, j, 0))),
        ],
        out_specs=pl.BlockSpec((tm, tn), lambda i, j: (i, j)),
        out_shape=jax.ShapeDtypeStruct((T, D), jnp.float32),
        compiler_params=_cparams("arbitrary", "arbitrary"),
        name="out_proj_residual",
    )(mix, w_out, xa, xb)


def _router_kernel(x_ref, g_ref, wr_ref, br_ref, hp_ref, idx_ref, gate_ref):
    x = x_ref[...]
    ms = jnp.mean(x * x, axis=-1, keepdims=True)
    h = x * lax.rsqrt(ms + RMS_EPS) * g_ref[...]
    half = h.shape[1] // 2
    hp_ref[...] = _pack_halves(h[:, :half], h[:, half:])
    logits = jnp.dot(h.astype(jnp.bfloat16), wr_ref[...], preferred_element_type=jnp.float32) + br_ref[...]
    n_exp = logits.shape[1]
    lane = lax.broadcasted_iota(jnp.int32, logits.shape, 1)
    vals, idxs = [], []
    for _ in range(TOP_K):
        m = jnp.max(logits, axis=-1, keepdims=True)
        idx = jnp.min(jnp.where(logits == m, lane, n_exp), axis=-1, keepdims=True)
        vals.append(m)
        idxs.append(idx)
        logits = jnp.where(lane == idx, -jnp.inf, logits)
    exps = [jnp.exp(v - vals[0]) for v in vals]
    total = sum(exps)
    for k in range(TOP_K):
        idx_ref[:, k:k + 1] = idxs[k]
        gate_ref[:, k:k + 1] = exps[k] / total


def _router(x1, gain, w_router, b_router, tr=256):
    T, D = x1.shape
    E = w_router.shape[1]
    return pl.pallas_call(
        _router_kernel,
        grid=(T // tr,),
        in_specs=[
            pl.BlockSpec((tr, D), lambda i: (i, 0)),
            pl.BlockSpec((1, D), lambda i: (0, 0)),
            pl.BlockSpec((D, E), lambda i: (0, 0)),
            pl.BlockSpec((1, E), lambda i: (0, 0)),
        ],
        out_specs=[
            pl.BlockSpec((tr, D // 2), lambda i: (i, 0)),
            pl.BlockSpec((tr, TOP_K), lambda i: (i, 0)),
            pl.BlockSpec((tr, TOP_K), lambda i: (i, 0)),
        ],
        out_shape=[
            jax.ShapeDtypeStruct((T, D // 2), jnp.uint32),
            jax.ShapeDtypeStruct((T, TOP_K), jnp.int32),
            jax.ShapeDtypeStruct((T, TOP_K), jnp.float32),
        ],
        compiler_params=_cparams("arbitrary"),
        name="router_topk",
    )(x1, gain.reshape(1, D), w_router, b_router.reshape(1, E))


def _routing_tables(top_idx, n_experts):
    e = top_idx.reshape(-1)
    onehot = (e[:, None] == jnp.arange(n_experts, dtype=jnp.int32)[None, :]).astype(jnp.int32)
    csum = jnp.cumsum(onehot, axis=0)
    counts = csum[-1]
    tiles_per_expert = (counts + EXPERT_TILE - 1) // EXPERT_TILE
    padded = tiles_per_expert * EXPERT_TILE
    pend = jnp.cumsum(padded)
    pstart = pend - padded
    pos = jnp.sum(onehot * (csum - 1 + pstart[None, :]), axis=1).astype(jnp.int32)
    pad_lo = (pstart + counts).astype(jnp.int32)
    return (pos, pad_lo, pend.astype(jnp.int32), counts.astype(jnp.int32), tiles_per_expert.astype(jnp.int32),
            (pstart // EXPERT_TILE).astype(jnp.int32))


def _dispatch_kernel(pos_ref, pad_lo_ref, pad_hi_ref, hp_ref, xs_ref, zero_sc, sem, *, n_experts):
    step = pl.program_id(0)
    tt = hp_ref.shape[0]

    def row_copy(tok, dst_row):
        return pltpu.make_async_copy(hp_ref.at[pl.ds(tok, 1)], xs_ref.at[pl.ds(dst_row, 1)], sem)

    def issue(tok, carry):
        for k in range(TOP_K):
            row_copy(tok, pos_ref[0, 0, tok * TOP_K + k]).start()
        return carry

    lax.fori_loop(0, tt, issue, 0, unroll=4)

    def drain(tok, carry):
        for k in range(TOP_K):
            row_copy(0, 0).wait()
        return carry

    lax.fori_loop(0, tt, drain, 0, unroll=4)

    @pl.when(step == pl.num_programs(0) - 1)
    def _():
        zero_sc[...] = jnp.zeros_like(zero_sc)

        def pad_copy(dst_row):
            return pltpu.make_async_copy(zero_sc.at[pl.ds(0, 1)], xs_ref.at[pl.ds(dst_row, 1)], sem)

        for e in range(n_experts):
            lo = pad_lo_ref[e]
            hi = pad_hi_ref[e]

            def fill(p, carry):
                pad_copy(p).start()
                return carry

            lax.fori_loop(lo, hi, fill, 0)

            def fill_wait(p, carry):
                pad_copy(0).wait()
                return carry

            lax.fori_loop(lo, hi, fill_wait, 0)


def _dispatch(hp, pos, pad_lo, pad_hi, n_slots, tt=DISPATCH_TILE):
    T, W = hp.shape
    n_experts = pad_lo.shape[0]
    grid_spec = pltpu.PrefetchScalarGridSpec(
        num_scalar_prefetch=0,
        grid=(T // tt,),
        in_specs=[
            pl.BlockSpec((1, 1, tt * TOP_K), lambda s: (s, 0, 0), memory_space=pltpu.SMEM),
            pl.BlockSpec(memory_space=pltpu.SMEM),
            pl.BlockSpec(memory_space=pltpu.SMEM),
            pl.BlockSpec((tt, W), lambda s: (s, 0)),
        ],
        out_specs=pl.BlockSpec(memory_space=pl.ANY),
        scratch_shapes=[pltpu.VMEM((8, W), jnp.uint32), pltpu.SemaphoreType.DMA(())],
    )
    return pl.pallas_call(
        functools.partial(_dispatch_kernel, n_experts=n_experts),
        grid_spec=grid_spec,
        out_shape=jax.ShapeDtypeStruct((n_slots, W), jnp.uint32),
        compiler_params=_cparams("arbitrary"),
        name="moe_dispatch",
    )(pos.reshape(T // tt, 1, tt * TOP_K), pad_lo, pad_hi, hp)


def _expert_steps(counts, tiles_per_expert, tile_base, nj, n_tiles):
    n_steps = n_tiles * nj
    steps_e = tiles_per_expert * nj
    send = jnp.cumsum(steps_e)
    total = send[-1]
    step = jnp.arange(n_steps, dtype=jnp.int32)
    s = jnp.minimum(step, total - 1)
    e = jnp.sum((send[None, :] <= s[:, None]).astype(jnp.int32), axis=1)
    local = s - (send - steps_e)[e]
    nt = jnp.maximum(tiles_per_expert[e], 1)
    j = local // nt
    t_local = local - j * nt
    tile = tile_base[e] + t_local
    used = step < total
    first = used & (t_local == 0)
    slot = (jnp.cumsum(first.astype(jnp.int32)) - 1) % 2
    s_next = s - t_local + nt
    has_next = first & (s_next < total)
    s_next = jnp.minimum(s_next, total - 1)
    half_only = counts[e] - t_local * EXPERT_TILE <= EXPERT_TILE // 2
    i32 = lambda a: a.astype(jnp.int32)
    return (i32(e), i32(j), i32(tile), i32(used), i32(first), i32(slot), i32(e[s_next]), i32(j[s_next]),
            i32(has_next), i32(half_only))


def _dot_f32_weight(x_parts, w_ref, part_rows):
    acc = None
    for p, x in enumerate(x_parts):
        for k0 in range(0, part_rows, WEIGHT_CAST_ROWS):
            w = w_ref[p * part_rows + k0:p * part_rows + k0 + WEIGHT_CAST_ROWS, :].astype(jnp.bfloat16)
            d = jnp.dot(x[:, k0:k0 + WEIGHT_CAST_ROWS], w, preferred_element_type=jnp.float32)
            acc = d if acc is None else acc + d
    return acc


def _weight_prefetch(tabs, w_hbm, wbuf, sems, col_offsets, tn):
    e_ref, j_ref, _, _, first_ref, slot_ref, ne_ref, nj_ref, has_next_ref, _ = tabs
    s = pl.program_id(0)
    slot = slot_ref[s]

    def copies(e, j, sl):
        return [pltpu.make_async_copy(w_hbm.at[e, :, pl.ds(pl.multiple_of(off + j * tn, tn), tn)],
                                      wbuf.at[sl, m], sems.at[sl])
                for m, off in enumerate(col_offsets)]

    @pl.when(s == 0)
    def _():
        for c in copies(e_ref[0], j_ref[0], 0):
            c.start()

    @pl.when(first_ref[s] == 1)
    def _():
        for c in copies(e_ref[s], j_ref[s], slot):
            c.wait()

        @pl.when(has_next_ref[s] == 1)
        def _():
            for c in copies(ne_ref[s], nj_ref[s], 1 - slot):
                c.start()

    return slot


N_STEP_TABLES = 10


def _run_tile_rows(tabs, body):
    s = pl.program_id(0)
    used = tabs[3][s] == 1
    half_only = tabs[9][s] == 1
    top = slice(0, EXPERT_TILE // 2)
    pl.when(used & jnp.logical_not(half_only))(lambda: body(slice(0, EXPERT_TILE)))
    pl.when(used & half_only)(lambda: body(top))


def _gate_up_kernel(*refs, tn, up_offset):
    tabs = refs[:N_STEP_TABLES]
    xs_ref, w_hbm, bg_ref, bu_ref, o_ref, wbuf, sems = refs[N_STEP_TABLES:]
    slot = _weight_prefetch(tabs, w_hbm, wbuf, sems, (0, up_offset), tn)
    half = xs_ref.shape[1]

    def body(rows):
        lo, hi = _unpack_halves(xs_ref[rows, :])
        x_parts = (lo.astype(jnp.bfloat16), hi.astype(jnp.bfloat16))
        gate = jnp.minimum(_dot_f32_weight(x_parts, wbuf.at[slot, 0], half) + bg_ref[0], SWIGLU_LIMIT)
        lin = jnp.clip(_dot_f32_weight(x_parts, wbuf.at[slot, 1], half) + bu_ref[0], -SWIGLU_LIMIT, SWIGLU_LIMIT)
        o_ref[rows, :] = (gate * jax.nn.sigmoid(SWIGLU_ALPHA * gate) * (lin + 1.0)).astype(o_ref.dtype)
        if rows.stop < EXPERT_TILE:
            o_ref[rows.stop:, :] = jnp.zeros((EXPERT_TILE - rows.stop, o_ref.shape[1]), o_ref.dtype)

    _run_tile_rows(tabs, body)


def _expert_gate_up(xs, w_gu, b_gu, counts, tiles_per_expert, tile_base, tn=512):
    n_slots, half = xs.shape
    E, D, F2 = w_gu.shape
    F = F2 // 2
    tn = min(tn, F)
    nj = F // tn
    n_tiles = n_slots // EXPERT_TILE
    tables = _expert_steps(counts, tiles_per_expert, tile_base, nj, n_tiles)

    def bmap(off):
        return lambda s, e, j, *_: (e[s], 0, off + j[s])

    grid_spec = pltpu.PrefetchScalarGridSpec(
        num_scalar_prefetch=N_STEP_TABLES,
        grid=(n_tiles * nj,),
        in_specs=[
            pl.BlockSpec((EXPERT_TILE, half), lambda s, e, j, t, *_: (t[s], 0)),
            pl.BlockSpec(memory_space=pl.ANY),
            pl.BlockSpec((1, 1, tn), bmap(0)),
            pl.BlockSpec((1, 1, tn), bmap(nj)),
        ],
        out_specs=pl.BlockSpec((EXPERT_TILE, tn), lambda s, e, j, t, *_: (t[s], j[s])),
        scratch_shapes=[pltpu.VMEM((2, 2, D, tn), jnp.float32), pltpu.SemaphoreType.DMA((2,))],
    )
    return pl.pallas_call(
        functools.partial(_gate_up_kernel, tn=tn, up_offset=F),
        grid_spec=grid_spec,
        out_shape=jax.ShapeDtypeStruct((n_slots, F), jnp.bfloat16),
        compiler_params=_cparams("arbitrary"),
        name="expert_gate_up",
    )(*tables, xs, w_gu, b_gu.reshape(E, 1, F2), b_gu.reshape(E, 1, F2))


def _down_kernel(*refs, tn, hi_offset):
    tabs = refs[:N_STEP_TABLES]
    a_ref, w_hbm, blo_ref, bhi_ref, o_ref, wbuf, sems = refs[N_STEP_TABLES:]
    slot = _weight_prefetch(tabs, w_hbm, wbuf, sems, (0, hi_offset), tn)
    k_rows = a_ref.shape[1]

    def body(rows):
        a = (a_ref[rows, :],)
        y_lo = _dot_f32_weight(a, wbuf.at[slot, 0], k_rows) + blo_ref[0]
        y_hi = _dot_f32_weight(a, wbuf.at[slot, 1], k_rows) + bhi_ref[0]
        o_ref[rows, :] = _pack_halves(y_lo, y_hi)
        if rows.stop < EXPERT_TILE:
            o_ref[rows.stop:, :] = jnp.zeros((EXPERT_TILE - rows.stop, o_ref.shape[1]), o_ref.dtype)

    _run_tile_rows(tabs, body)


def _expert_down(act, w_down, b_down, counts, tiles_per_expert, tile_base, tn=512):
    n_slots, F = act.shape
    E, _, D = w_down.shape
    half = D // 2
    tn = min(tn, half)
    nj = half // tn
    n_tiles = n_slots // EXPERT_TILE
    tables = _expert_steps(counts, tiles_per_expert, tile_base, nj, n_tiles)

    def bmap(off):
        return lambda s, e, j, *_: (e[s], 0, off + j[s])

    grid_spec = pltpu.PrefetchScalarGridSpec(
        num_scalar_prefetch=N_STEP_TABLES,
        grid=(n_tiles * nj,),
        in_specs=[
            pl.BlockSpec((EXPERT_TILE, F), lambda s, e, j, t, *_: (t[s], 0)),
            pl.BlockSpec(memory_space=pl.ANY),
            pl.BlockSpec((1, 1, tn), bmap(0)),
            pl.BlockSpec((1, 1, tn), bmap(nj)),
        ],
        out_specs=pl.BlockSpec((EXPERT_TILE, tn), lambda s, e, j, t, *_: (t[s], j[s])),
        scratch_shapes=[pltpu.VMEM((2, 2, F, tn), jnp.float32), pltpu.SemaphoreType.DMA((2,))],
    )
    return pl.pallas_call(
        functools.partial(_down_kernel, tn=tn, hi_offset=half),
        grid_spec=grid_spec,
        out_shape=jax.ShapeDtypeStruct((n_slots, half), jnp.uint32),
        compiler_params=_cparams("arbitrary"),
        name="expert_down",
    )(*tables, act, w_down, b_down.reshape(E, 1, D), b_down.reshape(E, 1, D))


def _combine_kernel(pos_ref, pos_next_ref, x_ref, gate_ref, ys_ref, o_ref, buf, sems, *, tt):
    i = pl.program_id(0)
    slot = i % 2

    def row_copy(src_row, s, k, tok):
        return pltpu.make_async_copy(ys_ref.at[pl.ds(src_row, 1)], buf.at[s, k, pl.ds(tok, 1)], sems.at[s])

    def fetch(p_ref, s):
        def issue(tok, carry):
            for k in range(TOP_K):
                row_copy(p_ref[0, 0, tok * TOP_K + k], s, k, tok).start()
            return carry
        lax.fori_loop(0, tt, issue, 0, unroll=4)

    pl.when(i == 0)(lambda: fetch(pos_ref, 0))
    pl.when(i + 1 < pl.num_programs(0))(lambda: fetch(pos_next_ref, 1 - slot))

    def drain(tok, carry):
        for k in range(TOP_K):
            row_copy(0, slot, k, 0).wait()
        return carry

    lax.fori_loop(0, tt, drain, 0, unroll=4)

    half = buf.shape[3]
    acc_lo = x_ref[:, :half]
    acc_hi = x_ref[:, half:]
    for k in range(TOP_K):
        lo, hi = _unpack_halves(buf[slot, k])
        gk = gate_ref[:, k:k + 1]
        acc_lo = acc_lo + gk * lo
        acc_hi = acc_hi + gk * hi
    o_ref[:, :half] = acc_lo
    o_ref[:, half:] = acc_hi


def _combine(x1, gates, pos, ys, row0, n_rows, tt=COMBINE_TILE):
    T, D = x1.shape
    half = D // 2
    b0 = row0 // tt
    n_steps = n_rows // tt
    pos_blocks = pos.reshape(T // tt, 1, tt * TOP_K)
    grid_spec = pltpu.PrefetchScalarGridSpec(
        num_scalar_prefetch=0,
        grid=(n_steps,),
        in_specs=[
            pl.BlockSpec((1, 1, tt * TOP_K), lambda i: (b0 + i, 0, 0), memory_space=pltpu.SMEM),
            pl.BlockSpec((1, 1, tt * TOP_K), lambda i: (b0 + jnp.minimum(i + 1, n_steps - 1), 0, 0),
                         memory_space=pltpu.SMEM),
            pl.BlockSpec((tt, D), lambda i: (b0 + i, 0)),
            pl.BlockSpec((tt, TOP_K), lambda i: (b0 + i, 0)),
            pl.BlockSpec(memory_space=pl.ANY),
        ],
        out_specs=pl.BlockSpec((tt, D), lambda i: (i, 0)),
        scratch_shapes=[pltpu.VMEM((2, TOP_K, tt, half), jnp.uint32), pltpu.SemaphoreType.DMA((2,))],
    )
    return pl.pallas_call(
        functools.partial(_combine_kernel, tt=tt),
        grid_spec=grid_spec,
        out_shape=jax.ShapeDtypeStruct((n_rows, D), jnp.float32),
        compiler_params=_cparams("arbitrary"),
        name="moe_combine",
    )(pos_blocks, pos_blocks, x1, gates, ys)


def _sequence_tables(seq_lens, tile):
    lo, hi, start = [], [], 0
    for n in seq_lens:
        assert n % tile == 0
        lo += [start] * (n // tile)
        hi += [start + n] * (n // tile)
        start += n
    return jnp.asarray(np.array(lo, np.int32)), jnp.asarray(np.array(hi, np.int32))


def _rope_tables(seq_lens):
    half = HEAD_DIM // 2
    inv_freq = ROPE_THETA ** (-jnp.arange(half, dtype=jnp.float32) / half)
    pos = jnp.concatenate([jnp.arange(n, dtype=jnp.float32) for n in seq_lens])
    ang = pos[:, None] * inv_freq[None, :]
    cos, sin = jnp.cos(ang), jnp.sin(ang)
    return jnp.concatenate([cos, cos], axis=1), jnp.concatenate([-sin, sin], axis=1)


def _encoder_layer(xa, xb, seq_lens, norm_mix, w_in, q_norm, k_norm, w_pool, pool_scale, w_branch_a,
                   w_branch_b, w_out, norm_ffn, w_router, b_router, w_gate_up, b_gate_up, w_down, b_down):
    Ta, D = xa.shape
    Tb = xb.shape[0]
    T = Ta + Tb
    bf16 = jnp.bfloat16
    A_out = w_branch_a.shape[0]
    H = A_out // HEAD_DIM
    A = N_GROUPS * A_out
    P = w_branch_b.shape[0]
    E = w_router.shape[1]

    w_in_b = w_in.astype(bf16)
    gain_cols = jnp.concatenate([jnp.tile(q_norm, (1, H)).reshape(1, A),
                                 jnp.tile(k_norm, (1, H)).reshape(1, A)], axis=1)
    cos_t, sin_t = _rope_tables(seq_lens)

    h = _rmsnorm(xa, xb, norm_mix)
    qk = _qk_proj(h, w_in_b, gain_cols, cos_t, sin_t, 2 * A)
    v = _v_proj(h, w_in_b, 2 * A, A)
    u = _u_proj(h, w_in_b, 3 * A, P)

    lo_a, hi_a = _sequence_tables(seq_lens, ATTN_CHUNK)
    o_attn = _attention(qk, v, lo_a, hi_a, H)
    tp = 512
    lo_p, hi_p = _sequence_tables(seq_lens, tp)
    o_pool = _pool(u, w_pool.astype(bf16), pool_scale, lo_p, hi_p, tp)

    mix = _mix(h, w_in_b, 3 * A + P, o_attn, w_branch_a.astype(bf16), o_pool, w_branch_b.astype(bf16))
    x1 = _out_proj(mix, w_out.astype(bf16), xa, xb)

    hp, top_idx, gates = _router(x1, norm_ffn, w_router.astype(bf16), b_router)
    n_assign = T * TOP_K
    n_tiles = n_assign // EXPERT_TILE + E
    pos, pad_lo, pad_hi, counts, tiles_per_expert, tile_base = _routing_tables(top_idx, E)
    xs = _dispatch(hp, pos, pad_lo, pad_hi, n_tiles * EXPERT_TILE)
    act = _expert_gate_up(xs, w_gate_up, b_gate_up, counts, tiles_per_expert, tile_base)
    ys = _expert_down(act, w_down, b_down, counts, tiles_per_expert, tile_base)
    return _combine(x1, gates, pos, ys, 0, Ta), _combine(x1, gates, pos, ys, Ta, Tb)


def kernel(x_prompt, x_sample, norm_mix, w_in, q_norm, k_norm, w_pool, pool_scale, w_branch_a, w_branch_b,
           w_out, norm_ffn, w_router, b_router, w_gate_up, b_gate_up, w_down, b_down):
    depth = norm_mix.shape[0]
    D = x_prompt.shape[-1]
    seq_lens = (x_prompt.shape[1],) * x_prompt.shape[0] + (x_sample.shape[1],) * x_sample.shape[0]
    xa, xb = x_prompt.reshape(-1, D), x_sample.reshape(-1, D)
    for layer in range(depth):
        xa, xb = _encoder_layer(xa, xb, seq_lens, norm_mix[layer], w_in[layer], q_norm[layer], k_norm[layer],
                                w_pool[layer], pool_scale[layer], w_branch_a[layer], w_branch_b[layer],
                                w_out[layer], norm_ffn[layer], w_router[layer], b_router[layer],
                                w_gate_up[layer], b_gate_up[layer], w_down[layer], b_down[layer])
    return xa.reshape(x_prompt.shape), xb.reshape(x_sample.shape)
```

```python
import functools

import numpy as np
import jax
import jax.numpy as jnp
from jax import lax
from jax.experimental import pallas as pl
from jax.experimental.pallas import tpu as pltpu

HEAD_DIM = 128
DILATION_PAIRS = ((128, 1), (512, 4), (2048, 16))
N_GROUPS = len(DILATION_PAIRS)
ATTN_SIDE = 64
ROPE_THETA = 10000.0
POOL_WINDOWS = (2, 4, 8, 16)
POOL_HALO = 8
TOP_K = 4
SWIGLU_LIMIT = 7.0
SWIGLU_ALPHA = 1.702
RMS_EPS = 1e-6

VMEM_LIMIT_BYTES = 56 * 1024 * 1024
PROJ_ROWS = 1024
PROJ_ROW_SPLIT = 512
QK_ROW_SPLIT = 128

ATTN_CHUNK = 2048
ATTN_QBLK = 128
ATTN_WIN = 256
EXPERT_TILE = 512
EXPERT_ROW_BLOCK = 128
WEIGHT_CAST_ROWS = 512
DISPATCH_TILE = 512
COMBINE_TILE = 256


def _cparams(*sem):
    return pltpu.CompilerParams(dimension_semantics=sem, vmem_limit_bytes=VMEM_LIMIT_BYTES)


def _pack_halves(lo, hi):
    lo_bits = lax.bitcast_convert_type(lo.astype(jnp.bfloat16).astype(jnp.float32), jnp.uint32)
    hi_bits = lax.bitcast_convert_type(hi.astype(jnp.bfloat16).astype(jnp.float32), jnp.uint32)
    return (lo_bits >> 16) | (hi_bits & jnp.uint32(0xFFFF0000))


def _unpack_halves(packed):
    lo = lax.bitcast_convert_type(packed << 16, jnp.float32)
    hi = lax.bitcast_convert_type(packed & jnp.uint32(0xFFFF0000), jnp.float32)
    return lo, hi


def _rmsnorm_kernel(xa_ref, xb_ref, g_ref, o_ref, *, n_first):
    def body(x_ref):
        x = x_ref[...]
        ms = jnp.mean(x * x, axis=-1, keepdims=True)
        o_ref[...] = (x * lax.rsqrt(ms + RMS_EPS) * g_ref[...]).astype(o_ref.dtype)

    pl.when(pl.program_id(0) < n_first)(lambda: body(xa_ref))
    pl.when(pl.program_id(0) >= n_first)(lambda: body(xb_ref))


def _rmsnorm(xa, xb, gain, tm=256):
    D = xa.shape[1]
    T = xa.shape[0] + xb.shape[0]
    n_first = xa.shape[0] // tm
    return pl.pallas_call(
        functools.partial(_rmsnorm_kernel, n_first=n_first),
        grid=(T // tm,),
        in_specs=[pl.BlockSpec((tm, D), lambda i: (jnp.minimum(i, n_first - 1), 0)),
                  pl.BlockSpec((tm, D), lambda i: (jnp.maximum(i - n_first, 0), 0)),
                  pl.BlockSpec((1, D), lambda i: (0, 0))],
        out_specs=pl.BlockSpec((tm, D), lambda i: (i, 0)),
        out_shape=jax.ShapeDtypeStruct((T, D), jnp.bfloat16),
        compiler_params=_cparams("arbitrary"),
        name="rmsnorm",
    )(xa, xb, gain.reshape(1, D))


def _qk_proj_kernel(h_ref, w_ref, gain_ref, cos_ref, sin_ref, o_ref):
    w = w_ref[...]
    tm = h_ref.shape[0]
    for r0 in range(0, tm, QK_ROW_SPLIT):
        rows = slice(r0, r0 + QK_ROW_SPLIT)
        acc = jnp.dot(h_ref[rows, :], w, preferred_element_type=jnp.float32)
        cos = cos_ref[rows, :]
        sin = sin_ref[rows, :]
        for c in range(acc.shape[1] // HEAD_DIM):
            t = acc[:, c * HEAD_DIM:(c + 1) * HEAD_DIM]
            ms = jnp.mean(t * t, axis=-1, keepdims=True)
            t = t * lax.rsqrt(ms + RMS_EPS) * gain_ref[:, c * HEAD_DIM:(c + 1) * HEAD_DIM]
            o_ref[c, rows, :] = t * cos + pltpu.roll(t, HEAD_DIM // 2, axis=1) * sin


def _qk_proj(h, w_in, gain_cols, cos_t, sin_t, ncols, tm=PROJ_ROWS, tn=512):
    T, D = h.shape
    n_heads = ncols // HEAD_DIM
    return pl.pallas_call(
        _qk_proj_kernel,
        grid=(T // tm, ncols // tn),
        in_specs=[
            pl.BlockSpec((tm, D), lambda i, j: (i, 0)),
            pl.BlockSpec((D, tn), lambda i, j: (0, j)),
            pl.BlockSpec((1, tn), lambda i, j: (0, j)),
            pl.BlockSpec((tm, HEAD_DIM), lambda i, j: (i, 0)),
            pl.BlockSpec((tm, HEAD_DIM), lambda i, j: (i, 0)),
        ],
        out_specs=pl.BlockSpec((tn // HEAD_DIM, tm, HEAD_DIM), lambda i, j: (j, i, 0)),
        out_shape=jax.ShapeDtypeStruct((n_heads, T, HEAD_DIM), jnp.float32),
        compiler_params=_cparams("arbitrary", "arbitrary"),
        name="qk_proj",
    )(h, w_in, gain_cols, cos_t, sin_t)


def _v_proj_kernel(h_ref, w_ref, o_ref):
    w = w_ref[...]
    for r0 in range(0, h_ref.shape[0], PROJ_ROW_SPLIT):
        rows = slice(r0, r0 + PROJ_ROW_SPLIT)
        acc = jnp.dot(h_ref[rows, :], w, preferred_element_type=jnp.float32)
        for c in range(acc.shape[1] // HEAD_DIM):
            o_ref[c, rows, :] = acc[:, c * HEAD_DIM:(c + 1) * HEAD_DIM]


def _v_proj(h, w_in, col0, ncols, tm=PROJ_ROWS, tn=512):
    T, D = h.shape
    cb0 = col0 // tn
    return pl.pallas_call(
        _v_proj_kernel,
        grid=(T // tm, ncols // tn),
        in_specs=[
            pl.BlockSpec((tm, D), lambda i, j: (i, 0)),
            pl.BlockSpec((D, tn), lambda i, j: (0, cb0 + j)),
        ],
        out_specs=pl.BlockSpec((tn // HEAD_DIM, tm, HEAD_DIM), lambda i, j: (j, i, 0)),
        out_shape=jax.ShapeDtypeStruct((ncols // HEAD_DIM, T, HEAD_DIM), jnp.float32),
        compiler_params=_cparams("arbitrary", "arbitrary"),
        name="v_proj",
    )(h, w_in)


def _u_proj_kernel(h_ref, w_ref, o_ref):
    w = w_ref[...]
    for r0 in range(0, h_ref.shape[0], PROJ_ROW_SPLIT):
        rows = slice(r0, r0 + PROJ_ROW_SPLIT)
        o_ref[rows, :] = jnp.dot(h_ref[rows, :], w, preferred_element_type=jnp.float32)


def _u_proj(h, w_in, col0, ncols, tm=PROJ_ROWS, tn=512):
    T, D = h.shape
    cb0 = col0 // tn
    return pl.pallas_call(
        _u_proj_kernel,
        grid=(T // tm, ncols // tn),
        in_specs=[
            pl.BlockSpec((tm, D), lambda i, j: (i, 0)),
            pl.BlockSpec((D, tn), lambda i, j: (0, cb0 + j)),
        ],
        out_specs=pl.BlockSpec((tm, tn), lambda i, j: (i, j)),
        out_shape=jax.ShapeDtypeStruct((T, ncols), jnp.float32),
        compiler_params=_cparams("arbitrary", "arbitrary"),
        name="u_proj",
    )(h, w_in)


def _attn_kernel(lo_ref, hi_ref, *refs, chunk):
    in_refs = refs[:7 * N_GROUPS]
    o_ref = refs[7 * N_GROUPS]
    o_sc, lse_sc = refs[7 * N_GROUPS + 1:]
    c = pl.program_id(0)
    t0 = c * chunk
    seq_lo = lo_ref[c]
    seq_hi = hi_ref[c]
    win = ATTN_WIN
    slack = win - ATTN_QBLK - 2 * ATTN_SIDE
    qq = lax.broadcasted_iota(jnp.int32, (ATTN_QBLK, win), 0)
    kk = lax.broadcasted_iota(jnp.int32, (ATTN_QBLK, win), 1)
    band_bias = jnp.where(jnp.abs(kk - ATTN_SIDE - qq) <= ATTN_SIDE, 0.0, -jnp.inf)
    k_row = lax.broadcasted_iota(jnp.int32, (1, win), 1)
    scale = HEAD_DIM ** -0.5

    for g, (_, d) in enumerate(DILATION_PAIRS):
        q_ref, kc_ref, kl_ref, kr_ref, vc_ref, vl_ref, vr_ref = in_refs[7 * g:7 * g + 7]
        shift = d.bit_length() - 1
        rows = chunk // d
        m0 = t0 >> shift
        m_lo = seq_lo >> shift
        m_hi = seq_hi >> shift
        for r in range(d):
            def sub(ref, n, r=r, d=d):
                return ref[0, pl.ds(r, n, stride=d), :]
            q_r = sub(q_ref, rows).astype(jnp.bfloat16)
            def padded(l_ref, c_ref, r_ref):
                parts = [sub(l_ref, ATTN_SIDE), sub(c_ref, rows), sub(r_ref, ATTN_SIDE)]
                if slack:
                    parts.append(sub(r_ref, slack))
                return jnp.concatenate(parts, axis=0).astype(jnp.bfloat16)
            k_r = padded(kl_ref, kc_ref, kr_ref)
            v_r = padded(vl_ref, vc_ref, vr_ref)
            for b in range(rows // ATTN_QBLK):
                qs = q_r[b * ATTN_QBLK:(b + 1) * ATTN_QBLK]
                ks = k_r[b * ATTN_QBLK:b * ATTN_QBLK + win]
                vs = v_r[b * ATTN_QBLK:b * ATTN_QBLK + win]
                key_m = k_row + (m0 + (b * ATTN_QBLK - ATTN_SIDE))
                ends_bias = jnp.where((key_m >= m_lo) & (key_m < m_hi), 0.0, -jnp.inf)
                s = lax.dot_general(qs, ks, (((1,), (1,)), ((), ())),
                                    preferred_element_type=jnp.float32) * scale + band_bias + ends_bias
                m = jnp.max(s, axis=-1, keepdims=True)
                p = jnp.exp(s - m)
                denom = jnp.sum(p, axis=-1, keepdims=True)
                o = jnp.dot(p.astype(jnp.bfloat16), vs, preferred_element_type=jnp.float32) / denom
                lse = m + jnp.log(denom)
                dst = pl.ds(b * ATTN_QBLK * d + r, ATTN_QBLK, stride=d)
                o_sc[g, dst, :] = o
                lse_sc[g, dst, :] = jnp.broadcast_to(lse, (ATTN_QBLK, HEAD_DIM))

    lses = [lse_sc[g] for g in range(N_GROUPS)]
    top = functools.reduce(jnp.maximum, lses)
    ws = [jnp.exp(l - top) for l in lses]
    num = sum(w * o_sc[g] for g, w in enumerate(ws))
    o_ref[...] = (num / sum(ws)).astype(o_ref.dtype)


def _attention(qk, v, seq_lo, seq_hi, heads_per_group):
    n_qk, T, _ = qk.shape
    H = heads_per_group
    k_base = n_qk // 2
    chunk = ATTN_CHUNK
    n_chunks = T // chunk
    in_specs = []
    operands = []
    for g, (_, d) in enumerate(DILATION_PAIRS):
        halo = ATTN_SIDE * d
        per = chunk // halo
        n_halo = T // halo

        def center(base, g=g):
            return pl.BlockSpec((1, chunk, HEAD_DIM), lambda c, h, lo, hi: (base + g * H + h, c, 0))

        def left(base, g=g, per=per):
            return pl.BlockSpec((1, halo, HEAD_DIM),
                                lambda c, h, lo, hi: (base + g * H + h, jnp.maximum(c * per - 1, 0), 0))

        def right(base, g=g, per=per, n_halo=n_halo):
            return pl.BlockSpec((1, halo, HEAD_DIM),
                                lambda c, h, lo, hi: (base + g * H + h, jnp.minimum((c + 1) * per, n_halo - 1), 0))

        in_specs += [center(0), center(k_base), left(k_base), right(k_base), center(0), left(0), right(0)]
        operands += [qk, qk, qk, qk, v, v, v]
    grid_spec = pltpu.PrefetchScalarGridSpec(
        num_scalar_prefetch=2,
        grid=(n_chunks, H),
        in_specs=in_specs,
        out_specs=pl.BlockSpec((chunk, HEAD_DIM), lambda c, h, lo, hi: (c, h)),
        scratch_shapes=[pltpu.VMEM((N_GROUPS, chunk, HEAD_DIM), jnp.float32),
                        pltpu.VMEM((N_GROUPS, chunk, HEAD_DIM), jnp.float32)],
    )
    return pl.pallas_call(
        functools.partial(_attn_kernel, chunk=chunk),
        grid_spec=grid_spec,
        out_shape=jax.ShapeDtypeStruct((T, H * HEAD_DIM), jnp.bfloat16),
        compiler_params=_cparams("arbitrary", "arbitrary"),
        name="dilated_attention",
    )(seq_lo, seq_hi, *operands)


def _pool_kernel(lo_ref, hi_ref, uc_ref, ul_ref, ur_ref, w_ref, scale_ref, o_ref, ext_sc, *, tp):
    i = pl.program_id(0)
    t0 = i * tp
    seq_lo = lo_ref[i]
    seq_hi = hi_ref[i]
    gw = w_ref.shape[1]
    pos_l = t0 - POOL_HALO + lax.broadcasted_iota(jnp.int32, (POOL_HALO, 1), 0)
    pos_r = t0 + tp + lax.broadcasted_iota(jnp.int32, (POOL_HALO, 1), 0)
    ext_sc[0:POOL_HALO, :] = jnp.where(pos_l >= seq_lo, ul_ref[...], 0.0)
    ext_sc[POOL_HALO:POOL_HALO + tp, :] = uc_ref[...]
    ext_sc[POOL_HALO + tp:, :] = jnp.where(pos_r < seq_hi, ur_ref[...], 0.0)
    pos = t0 + lax.broadcasted_iota(jnp.int32, (tp, 1), 0)
    for gi, w in enumerate(POOL_WINDOWS):
        cols = slice(gi * gw, (gi + 1) * gw)
        total = None
        for j in range(-(w // 2), w - w // 2):
            piece = ext_sc[POOL_HALO + j:POOL_HALO + j + tp, cols]
            total = piece if total is None else total + piece
        cnt = jnp.minimum(pos + (w - w // 2), seq_hi) - jnp.maximum(pos - w // 2, seq_lo)
        pooled = total / cnt.astype(jnp.float32) - uc_ref[:, cols]
        mixed = jnp.dot(pooled.astype(jnp.bfloat16), w_ref[gi], preferred_element_type=jnp.float32)
        o_ref[:, cols] = (mixed * scale_ref[:, cols]).astype(o_ref.dtype)


def _pool(u, w_pool, pool_scale, seq_lo, seq_hi, tp):
    T, P = u.shape
    per = tp // POOL_HALO
    n_halo = T // POOL_HALO
    grid_spec = pltpu.PrefetchScalarGridSpec(
        num_scalar_prefetch=2,
        grid=(T // tp,),
        in_specs=[
            pl.BlockSpec((tp, P), lambda i, lo, hi: (i, 0)),
            pl.BlockSpec((POOL_HALO, P), lambda i, lo, hi: (jnp.maximum(i * per - 1, 0), 0)),
            pl.BlockSpec((POOL_HALO, P), lambda i, lo, hi: (jnp.minimum((i + 1) * per, n_halo - 1), 0)),
            pl.BlockSpec(w_pool.shape, lambda i, lo, hi: (0, 0, 0)),
            pl.BlockSpec((1, P), lambda i, lo, hi: (0, 0)),
        ],
        out_specs=pl.BlockSpec((tp, P), lambda i, lo, hi: (i, 0)),
        scratch_shapes=[pltpu.VMEM((tp + 2 * POOL_HALO, P), jnp.float32)],
    )
    return pl.pallas_call(
        functools.partial(_pool_kernel, tp=tp),
        grid_spec=grid_spec,
        out_shape=jax.ShapeDtypeStruct((T, P), jnp.bfloat16),
        compiler_params=_cparams("arbitrary"),
        name="multiscale_pool",
    )(seq_lo, seq_hi, u, u, u, w_pool, pool_scale.reshape(1, P))


def _mix_kernel(h_ref, wga_ref, wgb_ref, oa_ref, wa_ref, ob_ref, wb_ref, o_ref):
    h = h_ref[...]
    g_a = jnp.dot(h, wga_ref[...], preferred_element_type=jnp.float32)
    g_b = jnp.dot(h, wgb_ref[...], preferred_element_type=jnp.float32)
    a = jnp.dot(oa_ref[...], wa_ref[...], preferred_element_type=jnp.float32)
    b = jnp.dot(ob_ref[...], wb_ref[...], preferred_element_type=jnp.float32)
    o_ref[...] = (jax.nn.sigmoid(g_a) * a + jax.nn.sigmoid(g_b) * b).astype(o_ref.dtype)


def _mix(h, w_in, ga_col0, o_attn, w_a, o_pool, w_b, tm=512, tn=512):
    T, D = h.shape
    cb_a = ga_col0 // tn
    cb_b = (ga_col0 + D) // tn
    A = o_attn.shape[1]
    P = o_pool.shape[1]
    return pl.pallas_call(
        _mix_kernel,
        grid=(T // tm, D // tn),
        in_specs=[
            pl.BlockSpec((tm, D), lambda i, j: (i, 0)),
            pl.BlockSpec((D, tn), lambda i, j: (0, cb_a + j)),
            pl.BlockSpec((D, tn), lambda i, j: (0, cb_b + j)),
            pl.BlockSpec((tm, A), lambda i, j: (i, 0)),
            pl.BlockSpec((A, tn), lambda i, j: (0, j)),
            pl.BlockSpec((tm, P), lambda i, j: (i, 0)),
            pl.BlockSpec((P, tn), lambda i, j: (0, j)),
        ],
        out_specs=pl.BlockSpec((tm, tn), lambda i, j: (i, j)),
        out_shape=jax.ShapeDtypeStruct((T, D), jnp.bfloat16),
        compiler_params=_cparams("arbitrary", "arbitrary"),
        name="gated_mix",
    )(h, w_in, w_in, o_attn, w_a, o_pool, w_b)


def _out_proj_kernel(m_ref, w_ref, xa_ref, xb_ref, o_ref, *, n_first):
    i = pl.program_id(0)

    def body(x_ref):
        w = w_ref[...]
        half = m_ref.shape[0] // 2
        for rows in (slice(0, half), slice(half, 2 * half)):
            o_ref[rows, :] = x_ref[rows, :] + jnp.dot(m_ref[rows, :], w, preferred_element_type=jnp.float32)

    pl.when(i < n_first)(lambda: body(xa_ref))
    pl.when(i >= n_first)(lambda: body(xb_ref))


def _out_proj(mix, w_out, xa, xb, tm=PROJ_ROWS, tn=512):
    T, D = mix.shape
    tn = min(tn, D)
    nj = D // tn
    n_first = xa.shape[0] // tm
    return pl.pallas_call(
        functools.partial(_out_proj_kernel, n_first=n_first),
        grid=(T // tm, nj),
        in_specs=[
            pl.BlockSpec((tm, D), lambda i, j: (i, 0)),
            pl.BlockSpec((D, tn), lambda i, j: (0, j)),
            pl.BlockSpec((tm, tn), lambda i, j: (jnp.minimum(i, n_first - 1), jnp.where(i < n_first, j, nj - 1))),
            pl.BlockSpec((tm, tn), lambda i, j: (jnp.maximum(i - n_first, 0), jnp.where(i >= n_first, j, 0))),
        ],
        out_specs=pl.BlockSpec((tm, tn), lambda i, j: (i, j)),
        out_shape=jax.ShapeDtypeStruct((T, D), jnp.float32),
        compiler_params=_cparams("arbitrary", "arbitrary"),
        name="out_proj_residual",
    )(mix, w_out, xa, xb)


def _router_kernel(x_ref, g_ref, wr_ref, br_ref, hp_ref, idx_ref, gate_ref):
    x = x_ref[...]
    ms = jnp.mean(x * x, axis=-1, keepdims=True)
    h = x * lax.rsqrt(ms + RMS_EPS) * g_ref[...]
    half = h.shape[1] // 2
    hp_ref[...] = _pack_halves(h[:, :half], h[:, half:])
    logits = jnp.dot(h.astype(jnp.bfloat16), wr_ref[...], preferred_element_type=jnp.float32) + br_ref[...]
    n_exp = logits.shape[1]
    lane = lax.broadcasted_iota(jnp.int32, logits.shape, 1)
    vals, idxs = [], []
    for _ in range(TOP_K):
        m = jnp.max(logits, axis=-1, keepdims=True)
        idx = jnp.min(jnp.where(logits == m, lane, n_exp), axis=-1, keepdims=True)
        vals.append(m)
        idxs.append(idx)
        logits = jnp.where(lane == idx, -jnp.inf, logits)
    exps = [jnp.exp(v - vals[0]) for v in vals]
    total = sum(exps)
    for k in range(TOP_K):
        idx_ref[:, k:k + 1] = idxs[k]
        gate_ref[:, k:k + 1] = exps[k] / total


def _router(x1, gain, w_router, b_router, tr=256):
    T, D = x1.shape
    E = w_router.shape[1]
    return pl.pallas_call(
        _router_kernel,
        grid=(T // tr,),
        in_specs=[
            pl.BlockSpec((tr, D), lambda i: (i, 0)),
            pl.BlockSpec((1, D), lambda i: (0, 0)),
            pl.BlockSpec((D, E), lambda i: (0, 0)),
            pl.BlockSpec((1, E), lambda i: (0, 0)),
        ],
        out_specs=[
            pl.BlockSpec((tr, D // 2), lambda i: (i, 0)),
            pl.BlockSpec((tr, TOP_K), lambda i: (i, 0)),
            pl.BlockSpec((tr, TOP_K), lambda i: (i, 0)),
        ],
        out_shape=[
            jax.ShapeDtypeStruct((T, D // 2), jnp.uint32),
            jax.ShapeDtypeStruct((T, TOP_K), jnp.int32),
            jax.ShapeDtypeStruct((T, TOP_K), jnp.float32),
        ],
        compiler_params=_cparams("arbitrary"),
        name="router_topk",
    )(x1, gain.reshape(1, D), w_router, b_router.reshape(1, E))


def _routing_tables(top_idx, n_experts):
    e = top_idx.reshape(-1)
    onehot = (e[:, None] == jnp.arange(n_experts, dtype=jnp.int32)[None, :]).astype(jnp.int32)
    csum = jnp.cumsum(onehot, axis=0)
    counts = csum[-1]
    tiles_per_expert = (counts + EXPERT_TILE - 1) // EXPERT_TILE
    padded = tiles_per_expert * EXPERT_TILE
    pend = jnp.cumsum(padded)
    pstart = pend - padded
    pos = jnp.sum(onehot * (csum - 1 + pstart[None, :]), axis=1).astype(jnp.int32)
    pad_lo = (pstart + counts).astype(jnp.int32)
    return (pos, pad_lo, pend.astype(jnp.int32), counts.astype(jnp.int32), tiles_per_expert.astype(jnp.int32),
            (pstart // EXPERT_TILE).astype(jnp.int32))


def _dispatch_kernel(pos_ref, pad_lo_ref, pad_hi_ref, hp_ref, xs_ref, zero_sc, sem, *, n_experts):
    step = pl.program_id(0)
    tt = hp_ref.shape[0]

    def row_copy(tok, dst_row):
        return pltpu.make_async_copy(hp_ref.at[pl.ds(tok, 1)], xs_ref.at[pl.ds(dst_row, 1)], sem)

    def issue(tok, carry):
        for k in range(TOP_K):
            row_copy(tok, pos_ref[0, 0, tok * TOP_K + k]).start()
        return carry

    lax.fori_loop(0, tt, issue, 0, unroll=4)

    def drain(tok, carry):
        for k in range(TOP_K):
            row_copy(0, 0).wait()
        return carry

    lax.fori_loop(0, tt, drain, 0, unroll=4)

    @pl.when(step == pl.num_programs(0) - 1)
    def _():
        zero_sc[...] = jnp.zeros_like(zero_sc)

        def pad_copy(dst_row):
            return pltpu.make_async_copy(zero_sc.at[pl.ds(0, 1)], xs_ref.at[pl.ds(dst_row, 1)], sem)

        for e in range(n_experts):
            lo = pad_lo_ref[e]
            hi = pad_hi_ref[e]

            def fill(p, carry):
                pad_copy(p).start()
                return carry

            lax.fori_loop(lo, hi, fill, 0)

            def fill_wait(p, carry):
                pad_copy(0).wait()
                return carry

            lax.fori_loop(lo, hi, fill_wait, 0)


def _dispatch(hp, pos, pad_lo, pad_hi, n_slots, tt=DISPATCH_TILE):
    T, W = hp.shape
    n_experts = pad_lo.shape[0]
    grid_spec = pltpu.PrefetchScalarGridSpec(
        num_scalar_prefetch=0,
        grid=(T // tt,),
        in_specs=[
            pl.BlockSpec((1, 1, tt * TOP_K), lambda s: (s, 0, 0), memory_space=pltpu.SMEM),
            pl.BlockSpec(memory_space=pltpu.SMEM),
            pl.BlockSpec(memory_space=pltpu.SMEM),
            pl.BlockSpec((tt, W), lambda s: (s, 0)),
        ],
        out_specs=pl.BlockSpec(memory_space=pl.ANY),
        scratch_shapes=[pltpu.VMEM((8, W), jnp.uint32), pltpu.SemaphoreType.DMA(())],
    )
    return pl.pallas_call(
        functools.partial(_dispatch_kernel, n_experts=n_experts),
        grid_spec=grid_spec,
        out_shape=jax.ShapeDtypeStruct((n_slots, W), jnp.uint32),
        compiler_params=_cparams("arbitrary"),
        name="moe_dispatch",
    )(pos.reshape(T // tt, 1, tt * TOP_K), pad_lo, pad_hi, hp)


def _expert_steps(counts, tiles_per_expert, tile_base, nj, n_tiles):
    n_steps = n_tiles * nj
    steps_e = tiles_per_expert * nj
    send = jnp.cumsum(steps_e)
    total = send[-1]
    step = jnp.arange(n_steps, dtype=jnp.int32)
    s = jnp.minimum(step, total - 1)
    e = jnp.sum((send[None, :] <= s[:, None]).astype(jnp.int32), axis=1)
    local = s - (send - steps_e)[e]
    nt = jnp.maximum(tiles_per_expert[e], 1)
    j = local // nt
    t_local = local - j * nt
    tile = tile_base[e] + t_local
    used = step < total
    first = used & (t_local == 0)
    slot = (jnp.cumsum(first.astype(jnp.int32)) - 1) % 2
    s_next = s - t_local + nt
    has_next = first & (s_next < total)
    s_next = jnp.minimum(s_next, total - 1)
    real_rows = jnp.clip(counts[e] - t_local * EXPERT_TILE, 1, EXPERT_TILE)
    row_blocks = (real_rows + EXPERT_ROW_BLOCK - 1) // EXPERT_ROW_BLOCK
    i32 = lambda a: a.astype(jnp.int32)
    return (i32(e), i32(j), i32(tile), i32(used), i32(first), i32(slot), i32(e[s_next]), i32(j[s_next]),
            i32(has_next), i32(row_blocks))


def _dot_f32_weight(x_parts, w_ref, part_rows):
    acc = None
    for p, x in enumerate(x_parts):
        for k0 in range(0, part_rows, WEIGHT_CAST_ROWS):
            w = w_ref[p * part_rows + k0:p * part_rows + k0 + WEIGHT_CAST_ROWS, :].astype(jnp.bfloat16)
            d = jnp.dot(x[:, k0:k0 + WEIGHT_CAST_ROWS], w, preferred_element_type=jnp.float32)
            acc = d if acc is None else acc + d
    return acc


def _weight_prefetch(tabs, w_hbm, wbuf, sems, col_offsets, tn):
    e_ref, j_ref, _, _, first_ref, slot_ref, ne_ref, nj_ref, has_next_ref, _ = tabs
    s = pl.program_id(0)
    slot = slot_ref[s]

    def copies(e, j, sl):
        return [pltpu.make_async_copy(w_hbm.at[e, :, pl.ds(pl.multiple_of(off + j * tn, tn), tn)],
                                      wbuf.at[sl, m], sems.at[sl])
                for m, off in enumerate(col_offsets)]

    @pl.when(s == 0)
    def _():
        for c in copies(e_ref[0], j_ref[0], 0):
            c.start()

    @pl.when(first_ref[s] == 1)
    def _():
        for c in copies(e_ref[s], j_ref[s], slot):
            c.wait()

        @pl.when(has_next_ref[s] == 1)
        def _():
            for c in copies(ne_ref[s], nj_ref[s], 1 - slot):
                c.start()

    return slot


N_STEP_TABLES = 10


def _run_tile_rows(tabs, body):
    s = pl.program_id(0)
    used = tabs[3][s] == 1
    n_blocks = tabs[9][s]
    for nb in range(1, EXPERT_TILE // EXPERT_ROW_BLOCK + 1):
        pl.when(used & (n_blocks == nb))(functools.partial(body, slice(0, nb * EXPERT_ROW_BLOCK)))


def _gate_up_kernel(*refs, tn, up_offset):
    tabs = refs[:N_STEP_TABLES]
    xs_ref, w_hbm, bg_ref, bu_ref, o_ref, wbuf, sems = refs[N_STEP_TABLES:]
    slot = _weight_prefetch(tabs, w_hbm, wbuf, sems, (0, up_offset), tn)
    half = xs_ref.shape[1]

    def body(rows):
        lo, hi = _unpack_halves(xs_ref[rows, :])
        x_parts = (lo.astype(jnp.bfloat16), hi.astype(jnp.bfloat16))
        gate = jnp.minimum(_dot_f32_weight(x_parts, wbuf.at[slot, 0], half) + bg_ref[0], SWIGLU_LIMIT)
        lin = jnp.clip(_dot_f32_weight(x_parts, wbuf.at[slot, 1], half) + bu_ref[0], -SWIGLU_LIMIT, SWIGLU_LIMIT)
        o_ref[rows, :] = (gate * jax.nn.sigmoid(SWIGLU_ALPHA * gate) * (lin + 1.0)).astype(o_ref.dtype)
        if rows.stop < EXPERT_TILE:
            o_ref[rows.stop:, :] = jnp.zeros((EXPERT_TILE - rows.stop, o_ref.shape[1]), o_ref.dtype)

    _run_tile_rows(tabs, body)


def _expert_gate_up(xs, w_gu, b_gu, counts, tiles_per_expert, tile_base, tn=512):
    n_slots, half = xs.shape
    E, D, F2 = w_gu.shape
    F = F2 // 2
    tn = min(tn, F)
    nj = F // tn
    n_tiles = n_slots // EXPERT_TILE
    tables = _expert_steps(counts, tiles_per_expert, tile_base, nj, n_tiles)

    def bmap(off):
        return lambda s, e, j, *_: (e[s], 0, off + j[s])

    grid_spec = pltpu.PrefetchScalarGridSpec(
        num_scalar_prefetch=N_STEP_TABLES,
        grid=(n_tiles * nj,),
        in_specs=[
            pl.BlockSpec((EXPERT_TILE, half), lambda s, e, j, t, *_: (t[s], 0)),
            pl.BlockSpec(memory_space=pl.ANY),
            pl.BlockSpec((1, 1, tn), bmap(0)),
            pl.BlockSpec((1, 1, tn), bmap(nj)),
        ],
        out_specs=pl.BlockSpec((EXPERT_TILE, tn), lambda s, e, j, t, *_: (t[s], j[s])),
        scratch_shapes=[pltpu.VMEM((2, 2, D, tn), jnp.float32), pltpu.SemaphoreType.DMA((2,))],
    )
    return pl.pallas_call(
        functools.partial(_gate_up_kernel, tn=tn, up_offset=F),
        grid_spec=grid_spec,
        out_shape=jax.ShapeDtypeStruct((n_slots, F), jnp.bfloat16),
        compiler_params=_cparams("arbitrary"),
        name="expert_gate_up",
    )(*tables, xs, w_gu, b_gu.reshape(E, 1, F2), b_gu.reshape(E, 1, F2))


def _down_kernel(*refs, tn, hi_offset):
    tabs = refs[:N_STEP_TABLES]
    a_ref, w_hbm, blo_ref, bhi_ref, o_ref, wbuf, sems = refs[N_STEP_TABLES:]
    slot = _weight_prefetch(tabs, w_hbm, wbuf, sems, (0, hi_offset), tn)
    k_rows = a_ref.shape[1]

    def body(rows):
        a = (a_ref[rows, :],)
        y_lo = _dot_f32_weight(a, wbuf.at[slot, 0], k_rows) + blo_ref[0]
        y_hi = _dot_f32_weight(a, wbuf.at[slot, 1], k_rows) + bhi_ref[0]
        o_ref[rows, :] = _pack_halves(y_lo, y_hi)
        if rows.stop < EXPERT_TILE:
            o_ref[rows.stop:, :] = jnp.zeros((EXPERT_TILE - rows.stop, o_ref.shape[1]), o_ref.dtype)

    _run_tile_rows(tabs, body)


def _expert_down(act, w_down, b_down, counts, tiles_per_expert, tile_base, tn=512):
    n_slots, F = act.shape
    E, _, D = w_down.shape
    half = D // 2
    tn = min(tn, half)
    nj = half // tn
    n_tiles = n_slots // EXPERT_TILE
    tables = _expert_steps(counts, tiles_per_expert, tile_base, nj, n_tiles)

    def bmap(off):
        return lambda s, e, j, *_: (e[s], 0, off + j[s])

    grid_spec = pltpu.PrefetchScalarGridSpec(
        num_scalar_prefetch=N_STEP_TABLES,
        grid=(n_tiles * nj,),
        in_specs=[
            pl.BlockSpec((EXPERT_TILE, F), lambda s, e, j, t, *_: (t[s], 0)),
            pl.BlockSpec(memory_space=pl.ANY),
            pl.BlockSpec((1, 1, tn), bmap(0)),
            pl.BlockSpec((1, 1, tn), bmap(nj)),
        ],
        out_specs=pl.BlockSpec((EXPERT_TILE, tn), lambda s, e, j, t, *_: (t[s], j[s])),
        scratch_shapes=[pltpu.VMEM((2, 2, F, tn), jnp.float32), pltpu.SemaphoreType.DMA((2,))],
    )
    return pl.pallas_call(
        functools.partial(_down_kernel, tn=tn, hi_offset=half),
        grid_spec=grid_spec,
        out_shape=jax.ShapeDtypeStruct((n_slots, half), jnp.uint32),
        compiler_params=_cparams("arbitrary"),
        name="expert_down",
    )(*tables, act, w_down, b_down.reshape(E, 1, D), b_down.reshape(E, 1, D))


def _combine_kernel(pos_ref, pos_next_ref, x_ref, gate_ref, ys_ref, o_ref, buf, sems, *, tt):
    i = pl.program_id(0)
    slot = i % 2

    def row_copy(src_row, s, k, tok):
        return pltpu.make_async_copy(ys_ref.at[pl.ds(src_row, 1)], buf.at[s, k, pl.ds(tok, 1)], sems.at[s])

    def fetch(p_ref, s):
        def issue(tok, carry):
            for k in range(TOP_K):
                row_copy(p_ref[0, 0, tok * TOP_K + k], s, k, tok).start()
            return carry
        lax.fori_loop(0, tt, issue, 0, unroll=4)

    pl.when(i == 0)(lambda: fetch(pos_ref, 0))
    pl.when(i + 1 < pl.num_programs(0))(lambda: fetch(pos_next_ref, 1 - slot))

    def drain(tok, carry):
        for k in range(TOP_K):
            row_copy(0, slot, k, 0).wait()
        return carry

    lax.fori_loop(0, tt, drain, 0, unroll=4)

    half = buf.shape[3]
    acc_lo = x_ref[:, :half]
    acc_hi = x_ref[:, half:]
    for k in range(TOP_K):
        lo, hi = _unpack_halves(buf[slot, k])
        gk = gate_ref[:, k:k + 1]
        acc_lo = acc_lo + gk * lo
        acc_hi = acc_hi + gk * hi
    o_ref[:, :half] = acc_lo
    o_ref[:, half:] = acc_hi


def _combine(x1, gates, pos, ys, row0, n_rows, tt=COMBINE_TILE):
    T, D = x1.shape
    half = D // 2
    b0 = row0 // tt
    n_steps = n_rows // tt
    pos_blocks = pos.reshape(T // tt, 1, tt * TOP_K)
    grid_spec = pltpu.PrefetchScalarGridSpec(
        num_scalar_prefetch=0,
        grid=(n_steps,),
        in_specs=[
            pl.BlockSpec((1, 1, tt * TOP_K), lambda i: (b0 + i, 0, 0), memory_space=pltpu.SMEM),
            pl.BlockSpec((1, 1, tt * TOP_K), lambda i: (b0 + jnp.minimum(i + 1, n_steps - 1), 0, 0),
                         memory_space=pltpu.SMEM),
            pl.BlockSpec((tt, D), lambda i: (b0 + i, 0)),
            pl.BlockSpec((tt, TOP_K), lambda i: (b0 + i, 0)),
            pl.BlockSpec(memory_space=pl.ANY),
        ],
        out_specs=pl.BlockSpec((tt, D), lambda i: (i, 0)),
        scratch_shapes=[pltpu.VMEM((2, TOP_K, tt, half), jnp.uint32), pltpu.SemaphoreType.DMA((2,))],
    )
    return pl.pallas_call(
        functools.partial(_combine_kernel, tt=tt),
        grid_spec=grid_spec,
        out_shape=jax.ShapeDtypeStruct((n_rows, D), jnp.float32),
        compiler_params=_cparams("arbitrary"),
        name="moe_combine",
    )(pos_blocks, pos_blocks, x1, gates, ys)


def _sequence_tables(seq_lens, tile):
    lo, hi, start = [], [], 0
    for n in seq_lens:
        assert n % tile == 0
        lo += [start] * (n // tile)
        hi += [start + n] * (n // tile)
        start += n
    return jnp.asarray(np.array(lo, np.int32)), jnp.asarray(np.array(hi, np.int32))


def _rope_tables(seq_lens):
    half = HEAD_DIM // 2
    inv_freq = ROPE_THETA ** (-jnp.arange(half, dtype=jnp.float32) / half)
    pos = jnp.concatenate([jnp.arange(n, dtype=jnp.float32) for n in seq_lens])
    ang = pos[:, None] * inv_freq[None, :]
    cos, sin = jnp.cos(ang), jnp.sin(ang)
    return jnp.concatenate([cos, cos], axis=1), jnp.concatenate([-sin, sin], axis=1)


def _encoder_layer(xa, xb, seq_lens, norm_mix, w_in, q_norm, k_norm, w_pool, pool_scale, w_branch_a,
                   w_branch_b, w_out, norm_ffn, w_router, b_router, w_gate_up, b_gate_up, w_down, b_down):
    Ta, D = xa.shape
    Tb = xb.shape[0]
    T = Ta + Tb
    bf16 = jnp.bfloat16
    A_out = w_branch_a.shape[0]
    H = A_out // HEAD_DIM
    A = N_GROUPS * A_out
    P = w_branch_b.shape[0]
    E = w_router.shape[1]

    w_in_b = w_in.astype(bf16)
    gain_cols = jnp.concatenate([jnp.tile(q_norm, (1, H)).reshape(1, A),
                                 jnp.tile(k_norm, (1, H)).reshape(1, A)], axis=1)
    cos_t, sin_t = _rope_tables(seq_lens)

    h = _rmsnorm(xa, xb, norm_mix)
    qk = _qk_proj(h, w_in_b, gain_cols, cos_t, sin_t, 2 * A)
    v = _v_proj(h, w_in_b, 2 * A, A)
    u = _u_proj(h, w_in_b, 3 * A, P)

    lo_a, hi_a = _sequence_tables(seq_lens, ATTN_CHUNK)
    o_attn = _attention(qk, v, lo_a, hi_a, H)
    tp = 512
    lo_p, hi_p = _sequence_tables(seq_lens, tp)
    o_pool = _pool(u, w_pool.astype(bf16), pool_scale, lo_p, hi_p, tp)

    mix = _mix(h, w_in_b, 3 * A + P, o_attn, w_branch_a.astype(bf16), o_pool, w_branch_b.astype(bf16))
    x1 = _out_proj(mix, w_out.astype(bf16), xa, xb)

    hp, top_idx, gates = _router(x1, norm_ffn, w_router.astype(bf16), b_router)
    n_assign = T * TOP_K
    n_tiles = n_assign // EXPERT_TILE + E
    pos, pad_lo, pad_hi, counts, tiles_per_expert, tile_base = _routing_tables(top_idx, E)
    xs = _dispatch(hp, pos, pad_lo, pad_hi, n_tiles * EXPERT_TILE)
    act = _expert_gate_up(xs, w_gate_up, b_gate_up, counts, tiles_per_expert, tile_base)
    ys = _expert_down(act, w_down, b_down, counts, tiles_per_expert, tile_base)
    return _combine(x1, gates, pos, ys, 0, Ta), _combine(x1, gates, pos, ys, Ta, Tb)


def kernel(x_prompt, x_sample, norm_mix, w_in, q_norm, k_norm, w_pool, pool_scale, w_branch_a, w_branch_b,
           w_out, norm_ffn, w_router, b_router, w_gate_up, b_gate_up, w_down, b_down):
    depth = norm_mix.shape[0]
    D = x_prompt.shape[-1]
    seq_lens = (x_prompt.shape[1],) * x_prompt.shape[0] + (x_sample.shape[1],) * x_sample.shape[0]
    xa, xb = x_prompt.reshape(-1, D), x_sample.reshape(-1, D)
    for layer in range(depth):
        xa, xb = _encoder_layer(xa, xb, seq_lens, norm_mix[layer], w_in[layer], q_norm[layer], k_norm[layer],
                                w_pool[layer], pool_scale[layer], w_branch_a[layer], w_branch_b[layer],
                                w_out[layer], norm_ffn[layer], w_router[layer], b_router[layer],
                                w_gate_up[layer], b_gate_up[layer], w_down[layer], b_down[layer])
    return xa.reshape(x_prompt.shape), xb.reshape(x_sample.shape)
```

```python
import functools

import numpy as np
import jax
import jax.numpy as jnp
from jax import lax
from jax.experimental import pallas as pl
from jax.experimental.pallas import tpu as pltpu

HEAD_DIM = 128
DILATION_PAIRS = ((128, 1), (512, 4), (2048, 16))
N_GROUPS = len(DILATION_PAIRS)
ATTN_SIDE = 64
ROPE_THETA = 10000.0
POOL_WINDOWS = (2, 4, 8, 16)
POOL_HALO = 8
TOP_K = 4
SWIGLU_LIMIT = 7.0
SWIGLU_ALPHA = 1.702
RMS_EPS = 1e-6

VMEM_LIMIT_BYTES = 56 * 1024 * 1024
PROJ_ROWS = 1024
PROJ_ROW_SPLIT = 512
QK_ROW_SPLIT = 128

ATTN_CHUNK = 2048
ATTN_QBLK = 128
ATTN_WIN = 256
EXPERT_TILE = 512
EXPERT_ROW_BLOCK = 256
WEIGHT_CAST_ROWS = 512
DISPATCH_TILE = 512
COMBINE_TILE = 256


def _cparams(*sem):
    return pltpu.CompilerParams(dimension_semantics=sem, vmem_limit_bytes=VMEM_LIMIT_BYTES)


def _pack_halves(lo, hi):
    lo_bits = lax.bitcast_convert_type(lo.astype(jnp.bfloat16).astype(jnp.float32), jnp.uint32)
    hi_bits = lax.bitcast_convert_type(hi.astype(jnp.bfloat16).astype(jnp.float32), jnp.uint32)
    return (lo_bits >> 16) | (hi_bits & jnp.uint32(0xFFFF0000))


def _unpack_halves(packed):
    lo = lax.bitcast_convert_type(packed << 16, jnp.float32)
    hi = lax.bitcast_convert_type(packed & jnp.uint32(0xFFFF0000), jnp.float32)
    return lo, hi


def _rmsnorm_kernel(xa_ref, xb_ref, g_ref, o_ref, *, n_first):
    def body(x_ref):
        x = x_ref[...]
        ms = jnp.mean(x * x, axis=-1, keepdims=True)
        o_ref[...] = (x * lax.rsqrt(ms + RMS_EPS) * g_ref[...]).astype(o_ref.dtype)

    pl.when(pl.program_id(0) < n_first)(lambda: body(xa_ref))
    pl.when(pl.program_id(0) >= n_first)(lambda: body(xb_ref))


def _rmsnorm(xa, xb, gain, tm=256):
    D = xa.shape[1]
    T = xa.shape[0] + xb.shape[0]
    n_first = xa.shape[0] // tm
    return pl.pallas_call(
        functools.partial(_rmsnorm_kernel, n_first=n_first),
        grid=(T // tm,),
        in_specs=[pl.BlockSpec((tm, D), lambda i: (jnp.minimum(i, n_first - 1), 0)),
                  pl.BlockSpec((tm, D), lambda i: (jnp.maximum(i - n_first, 0), 0)),
                  pl.BlockSpec((1, D), lambda i: (0, 0))],
        out_specs=pl.BlockSpec((tm, D), lambda i: (i, 0)),
        out_shape=jax.ShapeDtypeStruct((T, D), jnp.bfloat16),
        compiler_params=_cparams("arbitrary"),
        name="rmsnorm",
    )(xa, xb, gain.reshape(1, D))


def _qk_proj_kernel(h_ref, w_ref, gain_ref, cos_ref, sin_ref, o_ref):
    w = w_ref[...]
    tm = h_ref.shape[0]
    for r0 in range(0, tm, QK_ROW_SPLIT):
        rows = slice(r0, r0 + QK_ROW_SPLIT)
        acc = jnp.dot(h_ref[rows, :], w, preferred_element_type=jnp.float32)
        cos = cos_ref[rows, :]
        sin = sin_ref[rows, :]
        for c in range(acc.shape[1] // HEAD_DIM):
            t = acc[:, c * HEAD_DIM:(c + 1) * HEAD_DIM]
            ms = jnp.mean(t * t, axis=-1, keepdims=True)
            t = t * lax.rsqrt(ms + RMS_EPS) * gain_ref[:, c * HEAD_DIM:(c + 1) * HEAD_DIM]
            o_ref[c, rows, :] = t * cos + pltpu.roll(t, HEAD_DIM // 2, axis=1) * sin


def _qk_proj(h, w_in, gain_cols, cos_t, sin_t, ncols, tm=PROJ_ROWS, tn=512):
    T, D = h.shape
    n_heads = ncols // HEAD_DIM
    return pl.pallas_call(
        _qk_proj_kernel,
        grid=(T // tm, ncols // tn),
        in_specs=[
            pl.BlockSpec((tm, D), lambda i, j: (i, 0)),
            pl.BlockSpec((D, tn), lambda i, j: (0, j)),
            pl.BlockSpec((1, tn), lambda i, j: (0, j)),
            pl.BlockSpec((tm, HEAD_DIM), lambda i, j: (i, 0)),
            pl.BlockSpec((tm, HEAD_DIM), lambda i, j: (i, 0)),
        ],
        out_specs=pl.BlockSpec((tn // HEAD_DIM, tm, HEAD_DIM), lambda i, j: (j, i, 0)),
        out_shape=jax.ShapeDtypeStruct((n_heads, T, HEAD_DIM), jnp.float32),
        compiler_params=_cparams("arbitrary", "arbitrary"),
        name="qk_proj",
    )(h, w_in, gain_cols, cos_t, sin_t)


def _v_proj_kernel(h_ref, w_ref, o_ref):
    w = w_ref[...]
    for r0 in range(0, h_ref.shape[0], PROJ_ROW_SPLIT):
        rows = slice(r0, r0 + PROJ_ROW_SPLIT)
        acc = jnp.dot(h_ref[rows, :], w, preferred_element_type=jnp.float32)
        for c in range(acc.shape[1] // HEAD_DIM):
            o_ref[c, rows, :] = acc[:, c * HEAD_DIM:(c + 1) * HEAD_DIM]


def _v_proj(h, w_in, col0, ncols, tm=PROJ_ROWS, tn=512):
    T, D = h.shape
    cb0 = col0 // tn
    return pl.pallas_call(
        _v_proj_kernel,
        grid=(T // tm, ncols // tn),
        in_specs=[
            pl.BlockSpec((tm, D), lambda i, j: (i, 0)),
            pl.BlockSpec((D, tn), lambda i, j: (0, cb0 + j)),
        ],
        out_specs=pl.BlockSpec((tn // HEAD_DIM, tm, HEAD_DIM), lambda i, j: (j, i, 0)),
        out_shape=jax.ShapeDtypeStruct((ncols // HEAD_DIM, T, HEAD_DIM), jnp.float32),
        compiler_params=_cparams("arbitrary", "arbitrary"),
        name="v_proj",
    )(h, w_in)


def _u_proj_kernel(h_ref, w_ref, o_ref):
    w = w_ref[...]
    for r0 in range(0, h_ref.shape[0], PROJ_ROW_SPLIT):
        rows = slice(r0, r0 + PROJ_ROW_SPLIT)
        o_ref[rows, :] = jnp.dot(h_ref[rows, :], w, preferred_element_type=jnp.float32)


def _u_proj(h, w_in, col0, ncols, tm=PROJ_ROWS, tn=512):
    T, D = h.shape
    cb0 = col0 // tn
    return pl.pallas_call(
        _u_proj_kernel,
        grid=(T // tm, ncols // tn),
        in_specs=[
            pl.BlockSpec((tm, D), lambda i, j: (i, 0)),
            pl.BlockSpec((D, tn), lambda i, j: (0, cb0 + j)),
        ],
        out_specs=pl.BlockSpec((tm, tn), lambda i, j: (i, j)),
        out_shape=jax.ShapeDtypeStruct((T, ncols), jnp.float32),
        compiler_params=_cparams("arbitrary", "arbitrary"),
        name="u_proj",
    )(h, w_in)


def _attn_kernel(lo_ref, hi_ref, *refs, chunk):
    in_refs = refs[:7 * N_GROUPS]
    o_ref = refs[7 * N_GROUPS]
    o_sc, lse_sc = refs[7 * N_GROUPS + 1:]
    c = pl.program_id(0)
    t0 = c * chunk
    seq_lo = lo_ref[c]
    seq_hi = hi_ref[c]
    win = ATTN_WIN
    slack = win - ATTN_QBLK - 2 * ATTN_SIDE
    qq = lax.broadcasted_iota(jnp.int32, (ATTN_QBLK, win), 0)
    kk = lax.broadcasted_iota(jnp.int32, (ATTN_QBLK, win), 1)
    band_bias = jnp.where(jnp.abs(kk - ATTN_SIDE - qq) <= ATTN_SIDE, 0.0, -jnp.inf)
    k_row = lax.broadcasted_iota(jnp.int32, (1, win), 1)
    scale = HEAD_DIM ** -0.5

    for g, (_, d) in enumerate(DILATION_PAIRS):
        q_ref, kc_ref, kl_ref, kr_ref, vc_ref, vl_ref, vr_ref = in_refs[7 * g:7 * g + 7]
        shift = d.bit_length() - 1
        rows = chunk // d
        m0 = t0 >> shift
        m_lo = seq_lo >> shift
        m_hi = seq_hi >> shift
        for r in range(d):
            def sub(ref, n, r=r, d=d):
                return ref[0, pl.ds(r, n, stride=d), :]
            q_r = sub(q_ref, rows).astype(jnp.bfloat16)
            def padded(l_ref, c_ref, r_ref):
                parts = [sub(l_ref, ATTN_SIDE), sub(c_ref, rows), sub(r_ref, ATTN_SIDE)]
                if slack:
                    parts.append(sub(r_ref, slack))
                return jnp.concatenate(parts, axis=0).astype(jnp.bfloat16)
            k_r = padded(kl_ref, kc_ref, kr_ref)
            v_r = padded(vl_ref, vc_ref, vr_ref)
            for b in range(rows // ATTN_QBLK):
                qs = q_r[b * ATTN_QBLK:(b + 1) * ATTN_QBLK]
                ks = k_r[b * ATTN_QBLK:b * ATTN_QBLK + win]
                vs = v_r[b * ATTN_QBLK:b * ATTN_QBLK + win]
                key_m = k_row + (m0 + (b * ATTN_QBLK - ATTN_SIDE))
                ends_bias = jnp.where((key_m >= m_lo) & (key_m < m_hi), 0.0, -jnp.inf)
                s = lax.dot_general(qs, ks, (((1,), (1,)), ((), ())),
                                    preferred_element_type=jnp.float32) * scale + band_bias + ends_bias
                m = jnp.max(s, axis=-1, keepdims=True)
                p = jnp.exp(s - m)
                denom = jnp.sum(p, axis=-1, keepdims=True)
                o = jnp.dot(p.astype(jnp.bfloat16), vs, preferred_element_type=jnp.float32) / denom
                lse = m + jnp.log(denom)
                dst = pl.ds(b * ATTN_QBLK * d + r, ATTN_QBLK, stride=d)
                o_sc[g, dst, :] = o
                lse_sc[g, dst, :] = jnp.broadcast_to(lse, (ATTN_QBLK, HEAD_DIM))

    lses = [lse_sc[g] for g in range(N_GROUPS)]
    top = functools.reduce(jnp.maximum, lses)
    ws = [jnp.exp(l - top) for l in lses]
    num = sum(w * o_sc[g] for g, w in enumerate(ws))
    o_ref[...] = (num / sum(ws)).astype(o_ref.dtype)


def _attention(qk, v, seq_lo, seq_hi, heads_per_group):
    n_qk, T, _ = qk.shape
    H = heads_per_group
    k_base = n_qk // 2
    chunk = ATTN_CHUNK
    n_chunks = T // chunk
    in_specs = []
    operands = []
    for g, (_, d) in enumerate(DILATION_PAIRS):
        halo = ATTN_SIDE * d
        per = chunk // halo
        n_halo = T // halo

        def center(base, g=g):
            return pl.BlockSpec((1, chunk, HEAD_DIM), lambda c, h, lo, hi: (base + g * H + h, c, 0))

        def left(base, g=g, per=per):
            return pl.BlockSpec((1, halo, HEAD_DIM),
                                lambda c, h, lo, hi: (base + g * H + h, jnp.maximum(c * per - 1, 0), 0))

        def right(base, g=g, per=per, n_halo=n_halo):
            return pl.BlockSpec((1, halo, HEAD_DIM),
                                lambda c, h, lo, hi: (base + g * H + h, jnp.minimum((c + 1) * per, n_halo - 1), 0))

        in_specs += [center(0), center(k_base), left(k_base), right(k_base), center(0), left(0), right(0)]
        operands += [qk, qk, qk, qk, v, v, v]
    grid_spec = pltpu.PrefetchScalarGridSpec(
        num_scalar_prefetch=2,
        grid=(n_chunks, H),
        in_specs=in_specs,
        out_specs=pl.BlockSpec((chunk, HEAD_DIM), lambda c, h, lo, hi: (c, h)),
        scratch_shapes=[pltpu.VMEM((N_GROUPS, chunk, HEAD_DIM), jnp.float32),
                        pltpu.VMEM((N_GROUPS, chunk, HEAD_DIM), jnp.float32)],
    )
    return pl.pallas_call(
        functools.partial(_attn_kernel, chunk=chunk),
        grid_spec=grid_spec,
        out_shape=jax.ShapeDtypeStruct((T, H * HEAD_DIM), jnp.bfloat16),
        compiler_params=_cparams("arbitrary", "arbitrary"),
        name="dilated_attention",
    )(seq_lo, seq_hi, *operands)


def _pool_kernel(lo_ref, hi_ref, uc_ref, ul_ref, ur_ref, w_ref, scale_ref, o_ref, ext_sc, *, tp):
    i = pl.program_id(0)
    t0 = i * tp
    seq_lo = lo_ref[i]
    seq_hi = hi_ref[i]
    gw = w_ref.shape[1]
    pos_l = t0 - POOL_HALO + lax.broadcasted_iota(jnp.int32, (POOL_HALO, 1), 0)
    pos_r = t0 + tp + lax.broadcasted_iota(jnp.int32, (POOL_HALO, 1), 0)
    ext_sc[0:POOL_HALO, :] = jnp.where(pos_l >= seq_lo, ul_ref[...], 0.0)
    ext_sc[POOL_HALO:POOL_HALO + tp, :] = uc_ref[...]
    ext_sc[POOL_HALO + tp:, :] = jnp.where(pos_r < seq_hi, ur_ref[...], 0.0)
    pos = t0 + lax.broadcasted_iota(jnp.int32, (tp, 1), 0)
    for gi, w in enumerate(POOL_WINDOWS):
        cols = slice(gi * gw, (gi + 1) * gw)
        total = None
        for j in range(-(w // 2), w - w // 2):
            piece = ext_sc[POOL_HALO + j:POOL_HALO + j + tp, cols]
            total = piece if total is None else total + piece
        cnt = jnp.minimum(pos + (w - w // 2), seq_hi) - jnp.maximum(pos - w // 2, seq_lo)
        pooled = total / cnt.astype(jnp.float32) - uc_ref[:, cols]
        mixed = jnp.dot(pooled.astype(jnp.bfloat16), w_ref[gi], preferred_element_type=jnp.float32)
        o_ref[:, cols] = (mixed * scale_ref[:, cols]).astype(o_ref.dtype)


def _pool(u, w_pool, pool_scale, seq_lo, seq_hi, tp):
    T, P = u.shape
    per = tp // POOL_HALO
    n_halo = T // POOL_HALO
    grid_spec = pltpu.PrefetchScalarGridSpec(
        num_scalar_prefetch=2,
        grid=(T // tp,),
        in_specs=[
            pl.BlockSpec((tp, P), lambda i, lo, hi: (i, 0)),
            pl.BlockSpec((POOL_HALO, P), lambda i, lo, hi: (jnp.maximum(i * per - 1, 0), 0)),
            pl.BlockSpec((POOL_HALO, P), lambda i, lo, hi: (jnp.minimum((i + 1) * per, n_halo - 1), 0)),
            pl.BlockSpec(w_pool.shape, lambda i, lo, hi: (0, 0, 0)),
            pl.BlockSpec((1, P), lambda i, lo, hi: (0, 0)),
        ],
        out_specs=pl.BlockSpec((tp, P), lambda i, lo, hi: (i, 0)),
        scratch_shapes=[pltpu.VMEM((tp + 2 * POOL_HALO, P), jnp.float32)],
    )
    return pl.pallas_call(
        functools.partial(_pool_kernel, tp=tp),
        grid_spec=grid_spec,
        out_shape=jax.ShapeDtypeStruct((T, P), jnp.bfloat16),
        compiler_params=_cparams("arbitrary"),
        name="multiscale_pool",
    )(seq_lo, seq_hi, u, u, u, w_pool, pool_scale.reshape(1, P))


def _mix_kernel(h_ref, wga_ref, wgb_ref, oa_ref, wa_ref, ob_ref, wb_ref, o_ref):
    h = h_ref[...]
    g_a = jnp.dot(h, wga_ref[...], preferred_element_type=jnp.float32)
    g_b = jnp.dot(h, wgb_ref[...], preferred_element_type=jnp.float32)
    a = jnp.dot(oa_ref[...], wa_ref[...], preferred_element_type=jnp.float32)
    b = jnp.dot(ob_ref[...], wb_ref[...], preferred_element_type=jnp.float32)
    o_ref[...] = (jax.nn.sigmoid(g_a) * a + jax.nn.sigmoid(g_b) * b).astype(o_ref.dtype)


def _mix(h, w_in, ga_col0, o_attn, w_a, o_pool, w_b, tm=512, tn=512):
    T, D = h.shape
    cb_a = ga_col0 // tn
    cb_b = (ga_col0 + D) // tn
    A = o_attn.shape[1]
    P = o_pool.shape[1]
    return pl.pallas_call(
        _mix_kernel,
        grid=(T // tm, D // tn),
        in_specs=[
            pl.BlockSpec((tm, D), lambda i, j: (i, 0)),
            pl.BlockSpec((D, tn), lambda i, j: (0, cb_a + j)),
            pl.BlockSpec((D, tn), lambda i, j: (0, cb_b + j)),
            pl.BlockSpec((tm, A), lambda i, j: (i, 0)),
            pl.BlockSpec((A, tn), lambda i, j: (0, j)),
            pl.BlockSpec((tm, P), lambda i, j: (i, 0)),
            pl.BlockSpec((P, tn), lambda i, j: (0, j)),
        ],
        out_specs=pl.BlockSpec((tm, tn), lambda i, j: (i, j)),
        out_shape=jax.ShapeDtypeStruct((T, D), jnp.bfloat16),
        compiler_params=_cparams("arbitrary", "arbitrary"),
        name="gated_mix",
    )(h, w_in, w_in, o_attn, w_a, o_pool, w_b)


def _out_proj_kernel(m_ref, w_ref, xa_ref, xb_ref, o_ref, *, n_first):
    i = pl.program_id(0)

    def body(x_ref):
        w = w_ref[...]
        half = m_ref.shape[0] // 2
        for rows in (slice(0, half), slice(half, 2 * half)):
            o_ref[rows, :] = x_ref[rows, :] + jnp.dot(m_ref[rows, :], w, preferred_element_type=jnp.float32)

    pl.when(i < n_first)(lambda: body(xa_ref))
    pl.when(i >= n_first)(lambda: body(xb_ref))


def _out_proj(mix, w_out, xa, xb, tm=PROJ_ROWS, tn=512):
    T, D = mix.shape
    tn = min(tn, D)
    nj = D // tn
    n_first = xa.shape[0] // tm
    return pl.pallas_call(
        functools.partial(_out_proj_kernel, n_first=n_first),
        grid=(T // tm, nj),
        in_specs=[
            pl.BlockSpec((tm, D), lambda i, j: (i, 0)),
            pl.BlockSpec((D, tn), lambda i, j: (0, j)),
            pl.BlockSpec((tm, tn), lambda i, j: (jnp.minimum(i, n_first - 1), jnp.where(i < n_first, j, nj - 1))),
            pl.BlockSpec((tm, tn), lambda i, j: (jnp.maximum(i - n_first, 0), jnp.where(i >= n_first, j, 0))),
        ],
        out_specs=pl.BlockSpec((tm, tn), lambda i, j: (i, j)),
        out_shape=jax.ShapeDtypeStruct((T, D), jnp.float32),
        compiler_params=_cparams("arbitrary", "arbitrary"),
        name="out_proj_residual",
    )(mix, w_out, xa, xb)


def _router_kernel(x_ref, g_ref, wr_ref, br_ref, hp_ref, idx_ref, gate_ref):
    x = x_ref[...]
    ms = jnp.mean(x * x, axis=-1, keepdims=True)
    h = x * lax.rsqrt(ms + RMS_EPS) * g_ref[...]
    half = h.shape[1] // 2
    hp_ref[...] = _pack_halves(h[:, :half], h[:, half:])
    logits = jnp.dot(h.astype(jnp.bfloat16), wr_ref[...], preferred_element_type=jnp.float32) + br_ref[...]
    n_exp = logits.shape[1]
    lane = lax.broadcasted_iota(jnp.int32, logits.shape, 1)
    vals, idxs = [], []
    for _ in range(TOP_K):
        m = jnp.max(logits, axis=-1, keepdims=True)
        idx = jnp.min(jnp.where(logits == m, lane, n_exp), axis=-1, keepdims=True)
        vals.append(m)
        idxs.append(idx)
        logits = jnp.where(lane == idx, -jnp.inf, logits)
    exps = [jnp.exp(v - vals[0]) for v in vals]
    total = sum(exps)
    for k in range(TOP_K):
        idx_ref[:, k:k + 1] = idxs[k]
        gate_ref[:, k:k + 1] = exps[k] / total


def _router(x1, gain, w_router, b_router, tr=256):
    T, D = x1.shape
    E = w_router.shape[1]
    return pl.pallas_call(
        _router_kernel,
        grid=(T // tr,),
        in_specs=[
            pl.BlockSpec((tr, D), lambda i: (i, 0)),
            pl.BlockSpec((1, D), lambda i: (0, 0)),
            pl.BlockSpec((D, E), lambda i: (0, 0)),
            pl.BlockSpec((1, E), lambda i: (0, 0)),
        ],
        out_specs=[
            pl.BlockSpec((tr, D // 2), lambda i: (i, 0)),
            pl.BlockSpec((tr, TOP_K), lambda i: (i, 0)),
            pl.BlockSpec((tr, TOP_K), lambda i: (i, 0)),
        ],
        out_shape=[
            jax.ShapeDtypeStruct((T, D // 2), jnp.uint32),
            jax.ShapeDtypeStruct((T, TOP_K), jnp.int32),
            jax.ShapeDtypeStruct((T, TOP_K), jnp.float32),
        ],
        compiler_params=_cparams("arbitrary"),
        name="router_topk",
    )(x1, gain.reshape(1, D), w_router, b_router.reshape(1, E))


def _routing_tables(top_idx, n_experts):
    e = top_idx.reshape(-1)
    onehot = (e[:, None] == jnp.arange(n_experts, dtype=jnp.int32)[None, :]).astype(jnp.int32)
    csum = jnp.cumsum(onehot, axis=0)
    counts = csum[-1]
    tiles_per_expert = (counts + EXPERT_TILE - 1) // EXPERT_TILE
    padded = tiles_per_expert * EXPERT_TILE
    pend = jnp.cumsum(padded)
    pstart = pend - padded
    pos = jnp.sum(onehot * (csum - 1 + pstart[None, :]), axis=1).astype(jnp.int32)
    pad_lo = (pstart + counts).astype(jnp.int32)
    return (pos, pad_lo, pend.astype(jnp.int32), counts.astype(jnp.int32), tiles_per_expert.astype(jnp.int32),
            (pstart // EXPERT_TILE).astype(jnp.int32))


def _dispatch_kernel(pos_ref, pad_lo_ref, pad_hi_ref, hp_ref, xs_ref, zero_sc, sem, *, n_experts):
    step = pl.program_id(0)
    tt = hp_ref.shape[0]

    def row_copy(tok, dst_row):
        return pltpu.make_async_copy(hp_ref.at[pl.ds(tok, 1)], xs_ref.at[pl.ds(dst_row, 1)], sem)

    def issue(tok, carry):
        for k in range(TOP_K):
            row_copy(tok, pos_ref[0, 0, tok * TOP_K + k]).start(priority=k % 2)
        return carry

    lax.fori_loop(0, tt, issue, 0, unroll=4)

    def drain(tok, carry):
        for k in range(TOP_K):
            row_copy(0, 0).wait()
        return carry

    lax.fori_loop(0, tt, drain, 0, unroll=4)

    @pl.when(step == pl.num_programs(0) - 1)
    def _():
        zero_sc[...] = jnp.zeros_like(zero_sc)

        def pad_copy(dst_row):
            return pltpu.make_async_copy(zero_sc.at[pl.ds(0, 1)], xs_ref.at[pl.ds(dst_row, 1)], sem)

        for e in range(n_experts):
            lo = pad_lo_ref[e]
            hi = pad_hi_ref[e]

            def fill(p, carry):
                pad_copy(p).start()
                return carry

            lax.fori_loop(lo, hi, fill, 0)

            def fill_wait(p, carry):
                pad_copy(0).wait()
                return carry

            lax.fori_loop(lo, hi, fill_wait, 0)


def _dispatch(hp, pos, pad_lo, pad_hi, n_slots, tt=DISPATCH_TILE):
    T, W = hp.shape
    n_experts = pad_lo.shape[0]
    grid_spec = pltpu.PrefetchScalarGridSpec(
        num_scalar_prefetch=0,
        grid=(T // tt,),
        in_specs=[
            pl.BlockSpec((1, 1, tt * TOP_K), lambda s: (s, 0, 0), memory_space=pltpu.SMEM),
            pl.BlockSpec(memory_space=pltpu.SMEM),
            pl.BlockSpec(memory_space=pltpu.SMEM),
            pl.BlockSpec((tt, W), lambda s: (s, 0)),
        ],
        out_specs=pl.BlockSpec(memory_space=pl.ANY),
        scratch_shapes=[pltpu.VMEM((8, W), jnp.uint32), pltpu.SemaphoreType.DMA(())],
    )
    return pl.pallas_call(
        functools.partial(_dispatch_kernel, n_experts=n_experts),
        grid_spec=grid_spec,
        out_shape=jax.ShapeDtypeStruct((n_slots, W), jnp.uint32),
        compiler_params=_cparams("arbitrary"),
        name="moe_dispatch",
    )(pos.reshape(T // tt, 1, tt * TOP_K), pad_lo, pad_hi, hp)


def _expert_steps(counts, tiles_per_expert, tile_base, nj, n_tiles):
    n_steps = n_tiles * nj
    steps_e = tiles_per_expert * nj
    send = jnp.cumsum(steps_e)
    total = send[-1]
    step = jnp.arange(n_steps, dtype=jnp.int32)
    s = jnp.minimum(step, total - 1)
    e = jnp.sum((send[None, :] <= s[:, None]).astype(jnp.int32), axis=1)
    local = s - (send - steps_e)[e]
    nt = jnp.maximum(tiles_per_expert[e], 1)
    j = local // nt
    t_local = local - j * nt
    tile = tile_base[e] + t_local
    used = step < total
    first = used & (t_local == 0)
    slot = (jnp.cumsum(first.astype(jnp.int32)) - 1) % 2
    s_next = s - t_local + nt
    has_next = first & (s_next < total)
    s_next = jnp.minimum(s_next, total - 1)
    real_rows = jnp.clip(counts[e] - t_local * EXPERT_TILE, 1, EXPERT_TILE)
    row_blocks = (real_rows + EXPERT_ROW_BLOCK - 1) // EXPERT_ROW_BLOCK
    i32 = lambda a: a.astype(jnp.int32)
    return (i32(e), i32(j), i32(tile), i32(used), i32(first), i32(slot), i32(e[s_next]), i32(j[s_next]),
            i32(has_next), i32(row_blocks))


def _dot_f32_weight(x_parts, w_ref, part_rows):
    acc = None
    for p, x in enumerate(x_parts):
        for k0 in range(0, part_rows, WEIGHT_CAST_ROWS):
            w = w_ref[p * part_rows + k0:p * part_rows + k0 + WEIGHT_CAST_ROWS, :].astype(jnp.bfloat16)
            d = jnp.dot(x[:, k0:k0 + WEIGHT_CAST_ROWS], w, preferred_element_type=jnp.float32)
            acc = d if acc is None else acc + d
    return acc


def _weight_prefetch(tabs, w_hbm, wbuf, sems, col_offsets, tn):
    e_ref, j_ref, _, _, first_ref, slot_ref, ne_ref, nj_ref, has_next_ref, _ = tabs
    s = pl.program_id(0)
    slot = slot_ref[s]

    def copies(e, j, sl):
        return [pltpu.make_async_copy(w_hbm.at[e, :, pl.ds(pl.multiple_of(off + j * tn, tn), tn)],
                                      wbuf.at[sl, m], sems.at[sl])
                for m, off in enumerate(col_offsets)]

    @pl.when(s == 0)
    def _():
        for c in copies(e_ref[0], j_ref[0], 0):
            c.start(priority=1)

    @pl.when(first_ref[s] == 1)
    def _():
        for c in copies(e_ref[s], j_ref[s], slot):
            c.wait()

        @pl.when(has_next_ref[s] == 1)
        def _():
            for c in copies(ne_ref[s], nj_ref[s], 1 - slot):
                c.start(priority=1)

    return slot


N_STEP_TABLES = 10


def _run_tile_rows(tabs, body):
    s = pl.program_id(0)
    used = tabs[3][s] == 1
    n_blocks = tabs[9][s]
    for nb in range(1, EXPERT_TILE // EXPERT_ROW_BLOCK + 1):
        pl.when(used & (n_blocks == nb))(functools.partial(body, slice(0, nb * EXPERT_ROW_BLOCK)))


def _gate_up_kernel(*refs, tn, up_offset):
    tabs = refs[:N_STEP_TABLES]
    xs_ref, w_hbm, bg_ref, bu_ref, o_ref, wbuf, sems = refs[N_STEP_TABLES:]
    slot = _weight_prefetch(tabs, w_hbm, wbuf, sems, (0, up_offset), tn)
    half = xs_ref.shape[1]

    def body(rows):
        lo, hi = _unpack_halves(xs_ref[rows, :])
        x_parts = (lo.astype(jnp.bfloat16), hi.astype(jnp.bfloat16))
        gate = jnp.minimum(_dot_f32_weight(x_parts, wbuf.at[slot, 0], half) + bg_ref[0], SWIGLU_LIMIT)
        lin = jnp.clip(_dot_f32_weight(x_parts, wbuf.at[slot, 1], half) + bu_ref[0], -SWIGLU_LIMIT, SWIGLU_LIMIT)
        o_ref[rows, :] = (gate * jax.nn.sigmoid(SWIGLU_ALPHA * gate) * (lin + 1.0)).astype(o_ref.dtype)
        if rows.stop < EXPERT_TILE:
            o_ref[rows.stop:, :] = jnp.zeros((EXPERT_TILE - rows.stop, o_ref.shape[1]), o_ref.dtype)

    _run_tile_rows(tabs, body)


def _expert_gate_up(xs, w_gu, b_gu, counts, tiles_per_expert, tile_base, tn=512):
    n_slots, half = xs.shape
    E, D, F2 = w_gu.shape
    F = F2 // 2
    tn = min(tn, F)
    nj = F // tn
    n_tiles = n_slots // EXPERT_TILE
    tables = _expert_steps(counts, tiles_per_expert, tile_base, nj, n_tiles)

    def bmap(off):
        return lambda s, e, j, *_: (e[s], 0, off + j[s])

    grid_spec = pltpu.PrefetchScalarGridSpec(
        num_scalar_prefetch=N_STEP_TABLES,
        grid=(n_tiles * nj,),
        in_specs=[
            pl.BlockSpec((EXPERT_TILE, half), lambda s, e, j, t, *_: (t[s], 0)),
            pl.BlockSpec(memory_space=pl.ANY),
            pl.BlockSpec((1, 1, tn), bmap(0)),
            pl.BlockSpec((1, 1, tn), bmap(nj)),
        ],
        out_specs=pl.BlockSpec((EXPERT_TILE, tn), lambda s, e, j, t, *_: (t[s], j[s])),
        scratch_shapes=[pltpu.VMEM((2, 2, D, tn), jnp.float32), pltpu.SemaphoreType.DMA((2,))],
    )
    return pl.pallas_call(
        functools.partial(_gate_up_kernel, tn=tn, up_offset=F),
        grid_spec=grid_spec,
        out_shape=jax.ShapeDtypeStruct((n_slots, F), jnp.bfloat16),
        compiler_params=_cparams("arbitrary"),
        name="expert_gate_up",
    )(*tables, xs, w_gu, b_gu.reshape(E, 1, F2), b_gu.reshape(E, 1, F2))


def _down_kernel(*refs, tn, hi_offset):
    tabs = refs[:N_STEP_TABLES]
    a_ref, w_hbm, blo_ref, bhi_ref, o_ref, wbuf, sems = refs[N_STEP_TABLES:]
    slot = _weight_prefetch(tabs, w_hbm, wbuf, sems, (0, hi_offset), tn)
    k_rows = a_ref.shape[1]

    def body(rows):
        a = (a_ref[rows, :],)
        y_lo = _dot_f32_weight(a, wbuf.at[slot, 0], k_rows) + blo_ref[0]
        y_hi = _dot_f32_weight(a, wbuf.at[slot, 1], k_rows) + bhi_ref[0]
        o_ref[rows, :] = _pack_halves(y_lo, y_hi)
        if rows.stop < EXPERT_TILE:
            o_ref[rows.stop:, :] = jnp.zeros((EXPERT_TILE - rows.stop, o_ref.shape[1]), o_ref.dtype)

    _run_tile_rows(tabs, body)


def _expert_down(act, w_down, b_down, counts, tiles_per_expert, tile_base, tn=512):
    n_slots, F = act.shape
    E, _, D = w_down.shape
    half = D // 2
    tn = min(tn, half)
    nj = half // tn
    n_tiles = n_slots // EXPERT_TILE
    tables = _expert_steps(counts, tiles_per_expert, tile_base, nj, n_tiles)

    def bmap(off):
        return lambda s, e, j, *_: (e[s], 0, off + j[s])

    grid_spec = pltpu.PrefetchScalarGridSpec(
        num_scalar_prefetch=N_STEP_TABLES,
        grid=(n_tiles * nj,),
        in_specs=[
            pl.BlockSpec((EXPERT_TILE, F), lambda s, e, j, t, *_: (t[s], 0)),
            pl.BlockSpec(memory_space=pl.ANY),
            pl.BlockSpec((1, 1, tn), bmap(0)),
            pl.BlockSpec((1, 1, tn), bmap(nj)),
        ],
        out_specs=pl.BlockSpec((EXPERT_TILE, tn), lambda s, e, j, t, *_: (t[s], j[s])),
        scratch_shapes=[pltpu.VMEM((2, 2, F, tn), jnp.float32), pltpu.SemaphoreType.DMA((2,))],
    )
    return pl.pallas_call(
        functools.partial(_down_kernel, tn=tn, hi_offset=half),
        grid_spec=grid_spec,
        out_shape=jax.ShapeDtypeStruct((n_slots, half), jnp.uint32),
        compiler_params=_cparams("arbitrary"),
        name="expert_down",
    )(*tables, act, w_down, b_down.reshape(E, 1, D), b_down.reshape(E, 1, D))


def _combine_kernel(pos_ref, pos_next_ref, x_ref, gate_ref, ys_ref, o_ref, buf, sems, *, tt):
    i = pl.program_id(0)
    slot = i % 2

    def row_copy(src_row, s, k, tok):
        return pltpu.make_async_copy(ys_ref.at[pl.ds(src_row, 1)], buf.at[s, k, pl.ds(tok, 1)], sems.at[s])

    def fetch(p_ref, s):
        def issue(tok, carry):
            for k in range(TOP_K):
                row_copy(p_ref[0, 0, tok * TOP_K + k], s, k, tok).start(priority=k % 2)
            return carry
        lax.fori_loop(0, tt, issue, 0, unroll=4)

    pl.when(i == 0)(lambda: fetch(pos_ref, 0))
    pl.when(i + 1 < pl.num_programs(0))(lambda: fetch(pos_next_ref, 1 - slot))

    def drain(tok, carry):
        for k in range(TOP_K):
            row_copy(0, slot, k, 0).wait()
        return carry

    lax.fori_loop(0, tt, drain, 0, unroll=4)

    half = buf.shape[3]
    acc_lo = x_ref[:, :half]
    acc_hi = x_ref[:, half:]
    for k in range(TOP_K):
        lo, hi = _unpack_halves(buf[slot, k])
        gk = gate_ref[:, k:k + 1]
        acc_lo = acc_lo + gk * lo
        acc_hi = acc_hi + gk * hi
    o_ref[:, :half] = acc_lo
    o_ref[:, half:] = acc_hi


def _combine(x1, gates, pos, ys, row0, n_rows, tt=COMBINE_TILE):
    T, D = x1.shape
    half = D // 2
    b0 = row0 // tt
    n_steps = n_rows // tt
    pos_blocks = pos.reshape(T // tt, 1, tt * TOP_K)
    grid_spec = pltpu.PrefetchScalarGridSpec(
        num_scalar_prefetch=0,
        grid=(n_steps,),
        in_specs=[
            pl.BlockSpec((1, 1, tt * TOP_K), lambda i: (b0 + i, 0, 0), memory_space=pltpu.SMEM),
            pl.BlockSpec((1, 1, tt * TOP_K), lambda i: (b0 + jnp.minimum(i + 1, n_steps - 1), 0, 0),
                         memory_space=pltpu.SMEM),
            pl.BlockSpec((tt, D), lambda i: (b0 + i, 0)),
            pl.BlockSpec((tt, TOP_K), lambda i: (b0 + i, 0)),
            pl.BlockSpec(memory_space=pl.ANY),
        ],
        out_specs=pl.BlockSpec((tt, D), lambda i: (i, 0)),
        scratch_shapes=[pltpu.VMEM((2, TOP_K, tt, half), jnp.uint32), pltpu.SemaphoreType.DMA((2,))],
    )
    return pl.pallas_call(
        functools.partial(_combine_kernel, tt=tt),
        grid_spec=grid_spec,
        out_shape=jax.ShapeDtypeStruct((n_rows, D), jnp.float32),
        compiler_params=_cparams("arbitrary"),
        name="moe_combine",
    )(pos_blocks, pos_blocks, x1, gates, ys)


def _sequence_tables(seq_lens, tile):
    lo, hi, start = [], [], 0
    for n in seq_lens:
        assert n % tile == 0
        lo += [start] * (n // tile)
        hi += [start + n] * (n // tile)
        start += n
    return jnp.asarray(np.array(lo, np.int32)), jnp.asarray(np.array(hi, np.int32))


def _rope_tables(seq_lens):
    half = HEAD_DIM // 2
    inv_freq = ROPE_THETA ** (-jnp.arange(half, dtype=jnp.float32) / half)
    pos = jnp.concatenate([jnp.arange(n, dtype=jnp.float32) for n in seq_lens])
    ang = pos[:, None] * inv_freq[None, :]
    cos, sin = jnp.cos(ang), jnp.sin(ang)
    return jnp.concatenate([cos, cos], axis=1), jnp.concatenate([-sin, sin], axis=1)


def _encoder_layer(xa, xb, seq_lens, norm_mix, w_in, q_norm, k_norm, w_pool, pool_scale, w_branch_a,
                   w_branch_b, w_out, norm_ffn, w_router, b_router, w_gate_up, b_gate_up, w_down, b_down):
    Ta, D = xa.shape
    Tb = xb.shape[0]
    T = Ta + Tb
    bf16 = jnp.bfloat16
    A_out = w_branch_a.shape[0]
    H = A_out // HEAD_DIM
    A = N_GROUPS * A_out
    P = w_branch_b.shape[0]
    E = w_router.shape[1]

    w_in_b = w_in.astype(bf16)
    gain_cols = jnp.concatenate([jnp.tile(q_norm, (1, H)).reshape(1, A),
                                 jnp.tile(k_norm, (1, H)).reshape(1, A)], axis=1)
    cos_t, sin_t = _rope_tables(seq_lens)

    h = _rmsnorm(xa, xb, norm_mix)
    qk = _qk_proj(h, w_in_b, gain_cols, cos_t, sin_t, 2 * A)
    v = _v_proj(h, w_in_b, 2 * A, A)
    u = _u_proj(h, w_in_b, 3 * A, P)

    lo_a, hi_a = _sequence_tables(seq_lens, ATTN_CHUNK)
    o_attn = _attention(qk, v, lo_a, hi_a, H)
    tp = 512
    lo_p, hi_p = _sequence_tables(seq_lens, tp)
    o_pool = _pool(u, w_pool.astype(bf16), pool_scale, lo_p, hi_p, tp)

    mix = _mix(h, w_in_b, 3 * A + P, o_attn, w_branch_a.astype(bf16), o_pool, w_branch_b.astype(bf16))
    x1 = _out_proj(mix, w_out.astype(bf16), xa, xb)

    hp, top_idx, gates = _router(x1, norm_ffn, w_router.astype(bf16), b_router)
    n_assign = T * TOP_K
    n_tiles = n_assign // EXPERT_TILE + E
    pos, pad_lo, pad_hi, counts, tiles_per_expert, tile_base = _routing_tables(top_idx, E)
    xs = _dispatch(hp, pos, pad_lo, pad_hi, n_tiles * EXPERT_TILE)
    act = _expert_gate_up(xs, w_gate_up, b_gate_up, counts, tiles_per_expert, tile_base)
    ys = _expert_down(act, w_down, b_down, counts, tiles_per_expert, tile_base)
    return _combine(x1, gates, pos, ys, 0, Ta), _combine(x1, gates, pos, ys, Ta, Tb)


def kernel(x_prompt, x_sample, norm_mix, w_in, q_norm, k_norm, w_pool, pool_scale, w_branch_a, w_branch_b,
           w_out, norm_ffn, w_router, b_router, w_gate_up, b_gate_up, w_down, b_down):
    depth = norm_mix.shape[0]
    D = x_prompt.shape[-1]
    seq_lens = (x_prompt.shape[1],) * x_prompt.shape[0] + (x_sample.shape[1],) * x_sample.shape[0]
    xa, xb = x_prompt.reshape(-1, D), x_sample.reshape(-1, D)
    for layer in range(depth):
        xa, xb = _encoder_layer(xa, xb, seq_lens, norm_mix[layer], w_in[layer], q_norm[layer], k_norm[layer],
                                w_pool[layer], pool_scale[layer], w_branch_a[layer], w_branch_b[layer],
                                w_out[layer], norm_ffn[layer], w_router[layer], b_router[layer],
                                w_gate_up[layer], b_gate_up[layer], w_down[layer], b_down[layer])
    return xa.reshape(x_prompt.shape), xb.reshape(x_sample.shape)
```

```python
import functools

import numpy as np
import jax
import jax.numpy as jnp
from jax import lax
from jax.experimental import pallas as pl
from jax.experimental.pallas import tpu as pltpu

HEAD_DIM = 128
DILATION_PAIRS = ((128, 1), (512, 4), (2048, 16))
N_GROUPS = len(DILATION_PAIRS)
ATTN_SIDE = 64
ROPE_THETA = 10000.0
POOL_WINDOWS = (2, 4, 8, 16)
POOL_HALO = 8
TOP_K = 4
SWIGLU_LIMIT = 7.0
SWIGLU_ALPHA = 1.702
RMS_EPS = 1e-6

VMEM_LIMIT_BYTES = 62 * 1024 * 1024
PROJ_ROWS = 1024
PROJ_ROW_SPLIT = 512
QK_ROW_SPLIT = 128

ATTN_CHUNK = 2048
ATTN_QBLK = 128
ATTN_WIN = 256
EXPERT_TILE = 1024
EXPERT_ROW_BLOCK = 256
WEIGHT_CAST_ROWS = 512
DISPATCH_TILE = 512
COMBINE_TILE = 256


def _cparams(*sem):
    return pltpu.CompilerParams(dimension_semantics=sem, vmem_limit_bytes=VMEM_LIMIT_BYTES)


def _pack_halves(lo, hi):
    lo_bits = lax.bitcast_convert_type(lo.astype(jnp.bfloat16).astype(jnp.float32), jnp.uint32)
    hi_bits = lax.bitcast_convert_type(hi.astype(jnp.bfloat16).astype(jnp.float32), jnp.uint32)
    return (lo_bits >> 16) | (hi_bits & jnp.uint32(0xFFFF0000))


def _unpack_halves(packed):
    lo = lax.bitcast_convert_type(packed << 16, jnp.float32)
    hi = lax.bitcast_convert_type(packed & jnp.uint32(0xFFFF0000), jnp.float32)
    return lo, hi


def _rmsnorm_kernel(xa_ref, xb_ref, g_ref, o_ref, *, n_first):
    def body(x_ref):
        x = x_ref[...]
        ms = jnp.mean(x * x, axis=-1, keepdims=True)
        o_ref[...] = (x * lax.rsqrt(ms + RMS_EPS) * g_ref[...]).astype(o_ref.dtype)

    pl.when(pl.program_id(0) < n_first)(lambda: body(xa_ref))
    pl.when(pl.program_id(0) >= n_first)(lambda: body(xb_ref))


def _rmsnorm(xa, xb, gain, tm=256):
    D = xa.shape[1]
    T = xa.shape[0] + xb.shape[0]
    n_first = xa.shape[0] // tm
    return pl.pallas_call(
        functools.partial(_rmsnorm_kernel, n_first=n_first),
        grid=(T // tm,),
        in_specs=[pl.BlockSpec((tm, D), lambda i: (jnp.minimum(i, n_first - 1), 0)),
                  pl.BlockSpec((tm, D), lambda i: (jnp.maximum(i - n_first, 0), 0)),
                  pl.BlockSpec((1, D), lambda i: (0, 0))],
        out_specs=pl.BlockSpec((tm, D), lambda i: (i, 0)),
        out_shape=jax.ShapeDtypeStruct((T, D), jnp.bfloat16),
        compiler_params=_cparams("arbitrary"),
        name="rmsnorm",
    )(xa, xb, gain.reshape(1, D))


def _qk_proj_kernel(h_ref, w_ref, gain_ref, cos_ref, sin_ref, o_ref):
    w = w_ref[...]
    tm = h_ref.shape[0]
    for r0 in range(0, tm, QK_ROW_SPLIT):
        rows = slice(r0, r0 + QK_ROW_SPLIT)
        acc = jnp.dot(h_ref[rows, :], w, preferred_element_type=jnp.float32)
        cos = cos_ref[rows, :]
        sin = sin_ref[rows, :]
        for c in range(acc.shape[1] // HEAD_DIM):
            t = acc[:, c * HEAD_DIM:(c + 1) * HEAD_DIM]
            ms = jnp.mean(t * t, axis=-1, keepdims=True)
            t = t * lax.rsqrt(ms + RMS_EPS) * gain_ref[:, c * HEAD_DIM:(c + 1) * HEAD_DIM]
            o_ref[c, rows, :] = t * cos + pltpu.roll(t, HEAD_DIM // 2, axis=1) * sin


def _qk_proj(h, w_in, gain_cols, cos_t, sin_t, ncols, tm=PROJ_ROWS, tn=512):
    T, D = h.shape
    n_heads = ncols // HEAD_DIM
    return pl.pallas_call(
        _qk_proj_kernel,
        grid=(T // tm, ncols // tn),
        in_specs=[
            pl.BlockSpec((tm, D), lambda i, j: (i, 0)),
            pl.BlockSpec((D, tn), lambda i, j: (0, j)),
            pl.BlockSpec((1, tn), lambda i, j: (0, j)),
            pl.BlockSpec((tm, HEAD_DIM), lambda i, j: (i, 0)),
            pl.BlockSpec((tm, HEAD_DIM), lambda i, j: (i, 0)),
        ],
        out_specs=pl.BlockSpec((tn // HEAD_DIM, tm, HEAD_DIM), lambda i, j: (j, i, 0)),
        out_shape=jax.ShapeDtypeStruct((n_heads, T, HEAD_DIM), jnp.float32),
        compiler_params=_cparams("arbitrary", "arbitrary"),
        name="qk_proj",
    )(h, w_in, gain_cols, cos_t, sin_t)


def _v_proj_kernel(h_ref, w_ref, o_ref):
    w = w_ref[...]
    for r0 in range(0, h_ref.shape[0], PROJ_ROW_SPLIT):
        rows = slice(r0, r0 + PROJ_ROW_SPLIT)
        acc = jnp.dot(h_ref[rows, :], w, preferred_element_type=jnp.float32)
        for c in range(acc.shape[1] // HEAD_DIM):
            o_ref[c, rows, :] = acc[:, c * HEAD_DIM:(c + 1) * HEAD_DIM]


def _v_proj(h, w_in, col0, ncols, tm=PROJ_ROWS, tn=512):
    T, D = h.shape
    cb0 = col0 // tn
    return pl.pallas_call(
        _v_proj_kernel,
        grid=(T // tm, ncols // tn),
        in_specs=[
            pl.BlockSpec((tm, D), lambda i, j: (i, 0)),
            pl.BlockSpec((D, tn), lambda i, j: (0, cb0 + j)),
        ],
        out_specs=pl.BlockSpec((tn // HEAD_DIM, tm, HEAD_DIM), lambda i, j: (j, i, 0)),
        out_shape=jax.ShapeDtypeStruct((ncols // HEAD_DIM, T, HEAD_DIM), jnp.float32),
        compiler_params=_cparams("arbitrary", "arbitrary"),
        name="v_proj",
    )(h, w_in)


def _u_proj_kernel(h_ref, w_ref, o_ref):
    w = w_ref[...]
    for r0 in range(0, h_ref.shape[0], PROJ_ROW_SPLIT):
        rows = slice(r0, r0 + PROJ_ROW_SPLIT)
        o_ref[rows, :] = jnp.dot(h_ref[rows, :], w, preferred_element_type=jnp.float32)


def _u_proj(h, w_in, col0, ncols, tm=PROJ_ROWS, tn=512):
    T, D = h.shape
    cb0 = col0 // tn
    return pl.pallas_call(
        _u_proj_kernel,
        grid=(T // tm, ncols // tn),
        in_specs=[
            pl.BlockSpec((tm, D), lambda i, j: (i, 0)),
            pl.BlockSpec((D, tn), lambda i, j: (0, cb0 + j)),
        ],
        out_specs=pl.BlockSpec((tm, tn), lambda i, j: (i, j)),
        out_shape=jax.ShapeDtypeStruct((T, ncols), jnp.float32),
        compiler_params=_cparams("arbitrary", "arbitrary"),
        name="u_proj",
    )(h, w_in)


def _attn_kernel(lo_ref, hi_ref, *refs, chunk):
    in_refs = refs[:7 * N_GROUPS]
    o_ref = refs[7 * N_GROUPS]
    o_sc, lse_sc = refs[7 * N_GROUPS + 1:]
    c = pl.program_id(0)
    t0 = c * chunk
    seq_lo = lo_ref[c]
    seq_hi = hi_ref[c]
    win = ATTN_WIN
    slack = win - ATTN_QBLK - 2 * ATTN_SIDE
    qq = lax.broadcasted_iota(jnp.int32, (ATTN_QBLK, win), 0)
    kk = lax.broadcasted_iota(jnp.int32, (ATTN_QBLK, win), 1)
    band_bias = jnp.where(jnp.abs(kk - ATTN_SIDE - qq) <= ATTN_SIDE, 0.0, -jnp.inf)
    k_row = lax.broadcasted_iota(jnp.int32, (1, win), 1)
    scale = HEAD_DIM ** -0.5

    for g, (_, d) in enumerate(DILATION_PAIRS):
        q_ref, kc_ref, kl_ref, kr_ref, vc_ref, vl_ref, vr_ref = in_refs[7 * g:7 * g + 7]
        shift = d.bit_length() - 1
        rows = chunk // d
        m0 = t0 >> shift
        m_lo = seq_lo >> shift
        m_hi = seq_hi >> shift
        for r in range(d):
            def sub(ref, n, r=r, d=d):
                return ref[0, pl.ds(r, n, stride=d), :]
            q_r = sub(q_ref, rows).astype(jnp.bfloat16)
            def padded(l_ref, c_ref, r_ref):
                parts = [sub(l_ref, ATTN_SIDE), sub(c_ref, rows), sub(r_ref, ATTN_SIDE)]
                if slack:
                    parts.append(sub(r_ref, slack))
                return jnp.concatenate(parts, axis=0).astype(jnp.bfloat16)
            k_r = padded(kl_ref, kc_ref, kr_ref)
            v_r = padded(vl_ref, vc_ref, vr_ref)
            for b in range(rows // ATTN_QBLK):
                qs = q_r[b * ATTN_QBLK:(b + 1) * ATTN_QBLK]
                ks = k_r[b * ATTN_QBLK:b * ATTN_QBLK + win]
                vs = v_r[b * ATTN_QBLK:b * ATTN_QBLK + win]
                key_m = k_row + (m0 + (b * ATTN_QBLK - ATTN_SIDE))
                ends_bias = jnp.where((key_m >= m_lo) & (key_m < m_hi), 0.0, -jnp.inf)
                s = lax.dot_general(qs, ks, (((1,), (1,)), ((), ())),
                                    preferred_element_type=jnp.float32) * scale + band_bias + ends_bias
                m = jnp.max(s, axis=-1, keepdims=True)
                p = jnp.exp(s - m)
                denom = jnp.sum(p, axis=-1, keepdims=True)
                o = jnp.dot(p.astype(jnp.bfloat16), vs, preferred_element_type=jnp.float32) / denom
                lse = m + jnp.log(denom)
                dst = pl.ds(b * ATTN_QBLK * d + r, ATTN_QBLK, stride=d)
                o_sc[g, dst, :] = o
                lse_sc[g, dst, :] = jnp.broadcast_to(lse, (ATTN_QBLK, HEAD_DIM))

    lses = [lse_sc[g] for g in range(N_GROUPS)]
    top = functools.reduce(jnp.maximum, lses)
    ws = [jnp.exp(l - top) for l in lses]
    num = sum(w * o_sc[g] for g, w in enumerate(ws))
    o_ref[...] = (num / sum(ws)).astype(o_ref.dtype)


def _attention(qk, v, seq_lo, seq_hi, heads_per_group):
    n_qk, T, _ = qk.shape
    H = heads_per_group
    k_base = n_qk // 2
    chunk = ATTN_CHUNK
    n_chunks = T // chunk
    in_specs = []
    operands = []
    for g, (_, d) in enumerate(DILATION_PAIRS):
        halo = ATTN_SIDE * d
        per = chunk // halo
        n_halo = T // halo

        def center(base, g=g):
            return pl.BlockSpec((1, chunk, HEAD_DIM), lambda c, h, lo, hi: (base + g * H + h, c, 0))

        def left(base, g=g, per=per):
            return pl.BlockSpec((1, halo, HEAD_DIM),
                                lambda c, h, lo, hi: (base + g * H + h, jnp.maximum(c * per - 1, 0), 0))

        def right(base, g=g, per=per, n_halo=n_halo):
            return pl.BlockSpec((1, halo, HEAD_DIM),
                                lambda c, h, lo, hi: (base + g * H + h, jnp.minimum((c + 1) * per, n_halo - 1), 0))

        in_specs += [center(0), center(k_base), left(k_base), right(k_base), center(0), left(0), right(0)]
        operands += [qk, qk, qk, qk, v, v, v]
    grid_spec = pltpu.PrefetchScalarGridSpec(
        num_scalar_prefetch=2,
        grid=(n_chunks, H),
        in_specs=in_specs,
        out_specs=pl.BlockSpec((chunk, HEAD_DIM), lambda c, h, lo, hi: (c, h)),
        scratch_shapes=[pltpu.VMEM((N_GROUPS, chunk, HEAD_DIM), jnp.float32),
                        pltpu.VMEM((N_GROUPS, chunk, HEAD_DIM), jnp.float32)],
    )
    return pl.pallas_call(
        functools.partial(_attn_kernel, chunk=chunk),
        grid_spec=grid_spec,
        out_shape=jax.ShapeDtypeStruct((T, H * HEAD_DIM), jnp.bfloat16),
        compiler_params=_cparams("arbitrary", "arbitrary"),
        name="dilated_attention",
    )(seq_lo, seq_hi, *operands)


def _pool_kernel(lo_ref, hi_ref, uc_ref, ul_ref, ur_ref, w_ref, scale_ref, o_ref, ext_sc, *, tp):
    i = pl.program_id(0)
    t0 = i * tp
    seq_lo = lo_ref[i]
    seq_hi = hi_ref[i]
    gw = w_ref.shape[1]
    pos_l = t0 - POOL_HALO + lax.broadcasted_iota(jnp.int32, (POOL_HALO, 1), 0)
    pos_r = t0 + tp + lax.broadcasted_iota(jnp.int32, (POOL_HALO, 1), 0)
    ext_sc[0:POOL_HALO, :] = jnp.where(pos_l >= seq_lo, ul_ref[...], 0.0)
    ext_sc[POOL_HALO:POOL_HALO + tp, :] = uc_ref[...]
    ext_sc[POOL_HALO + tp:, :] = jnp.where(pos_r < seq_hi, ur_ref[...], 0.0)
    pos = t0 + lax.broadcasted_iota(jnp.int32, (tp, 1), 0)
    for gi, w in enumerate(POOL_WINDOWS):
        cols = slice(gi * gw, (gi + 1) * gw)
        total = None
        for j in range(-(w // 2), w - w // 2):
            piece = ext_sc[POOL_HALO + j:POOL_HALO + j + tp, cols]
            total = piece if total is None else total + piece
        cnt = jnp.minimum(pos + (w - w // 2), seq_hi) - jnp.maximum(pos - w // 2, seq_lo)
        pooled = total / cnt.astype(jnp.float32) - uc_ref[:, cols]
        mixed = jnp.dot(pooled.astype(jnp.bfloat16), w_ref[gi], preferred_element_type=jnp.float32)
        o_ref[:, cols] = (mixed * scale_ref[:, cols]).astype(o_ref.dtype)


def _pool(u, w_pool, pool_scale, seq_lo, seq_hi, tp):
    T, P = u.shape
    per = tp // POOL_HALO
    n_halo = T // POOL_HALO
    grid_spec = pltpu.PrefetchScalarGridSpec(
        num_scalar_prefetch=2,
        grid=(T // tp,),
        in_specs=[
            pl.BlockSpec((tp, P), lambda i, lo, hi: (i, 0)),
            pl.BlockSpec((POOL_HALO, P), lambda i, lo, hi: (jnp.maximum(i * per - 1, 0), 0)),
            pl.BlockSpec((POOL_HALO, P), lambda i, lo, hi: (jnp.minimum((i + 1) * per, n_halo - 1), 0)),
            pl.BlockSpec(w_pool.shape, lambda i, lo, hi: (0, 0, 0)),
            pl.BlockSpec((1, P), lambda i, lo, hi: (0, 0)),
        ],
        out_specs=pl.BlockSpec((tp, P), lambda i, lo, hi: (i, 0)),
        scratch_shapes=[pltpu.VMEM((tp + 2 * POOL_HALO, P), jnp.float32)],
    )
    return pl.pallas_call(
        functools.partial(_pool_kernel, tp=tp),
        grid_spec=grid_spec,
        out_shape=jax.ShapeDtypeStruct((T, P), jnp.bfloat16),
        compiler_params=_cparams("arbitrary"),
        name="multiscale_pool",
    )(seq_lo, seq_hi, u, u, u, w_pool, pool_scale.reshape(1, P))


def _mix_kernel(h_ref, wga_ref, wgb_ref, oa_ref, wa_ref, ob_ref, wb_ref, o_ref):
    h = h_ref[...]
    g_a = jnp.dot(h, wga_ref[...], preferred_element_type=jnp.float32)
    g_b = jnp.dot(h, wgb_ref[...], preferred_element_type=jnp.float32)
    a = jnp.dot(oa_ref[...], wa_ref[...], preferred_element_type=jnp.float32)
    b = jnp.dot(ob_ref[...], wb_ref[...], preferred_element_type=jnp.float32)
    o_ref[...] = (jax.nn.sigmoid(g_a) * a + jax.nn.sigmoid(g_b) * b).astype(o_ref.dtype)


def _mix(h, w_in, ga_col0, o_attn, w_a, o_pool, w_b, tm=512, tn=512):
    T, D = h.shape
    cb_a = ga_col0 // tn
    cb_b = (ga_col0 + D) // tn
    A = o_attn.shape[1]
    P = o_pool.shape[1]
    return pl.pallas_call(
        _mix_kernel,
        grid=(T // tm, D // tn),
        in_specs=[
            pl.BlockSpec((tm, D), lambda i, j: (i, 0)),
            pl.BlockSpec((D, tn), lambda i, j: (0, cb_a + j)),
            pl.BlockSpec((D, tn), lambda i, j: (0, cb_b + j)),
            pl.BlockSpec((tm, A), lambda i, j: (i, 0)),
            pl.BlockSpec((A, tn), lambda i, j: (0, j)),
            pl.BlockSpec((tm, P), lambda i, j: (i, 0)),
            pl.BlockSpec((P, tn), lambda i, j: (0, j)),
        ],
        out_specs=pl.BlockSpec((tm, tn), lambda i, j: (i, j)),
        out_shape=jax.ShapeDtypeStruct((T, D), jnp.bfloat16),
        compiler_params=_cparams("arbitrary", "arbitrary"),
        name="gated_mix",
    )(h, w_in, w_in, o_attn, w_a, o_pool, w_b)


def _out_proj_kernel(m_ref, w_ref, xa_ref, xb_ref, o_ref, *, n_first):
    i = pl.program_id(0)

    def body(x_ref):
        w = w_ref[...]
        half = m_ref.shape[0] // 2
        for rows in (slice(0, half), slice(half, 2 * half)):
            o_ref[rows, :] = x_ref[rows, :] + jnp.dot(m_ref[rows, :], w, preferred_element_type=jnp.float32)

    pl.when(i < n_first)(lambda: body(xa_ref))
    pl.when(i >= n_first)(lambda: body(xb_ref))


def _out_proj(mix, w_out, xa, xb, tm=PROJ_ROWS, tn=512):
    T, D = mix.shape
    tn = min(tn, D)
    nj = D // tn
    n_first = xa.shape[0] // tm
    return pl.pallas_call(
        functools.partial(_out_proj_kernel, n_first=n_first),
        grid=(T // tm, nj),
        in_specs=[
            pl.BlockSpec((tm, D), lambda i, j: (i, 0)),
            pl.BlockSpec((D, tn), lambda i, j: (0, j)),
            pl.BlockSpec((tm, tn), lambda i, j: (jnp.minimum(i, n_first - 1), jnp.where(i < n_first, j, nj - 1))),
            pl.BlockSpec((tm, tn), lambda i, j: (jnp.maximum(i - n_first, 0), jnp.where(i >= n_first, j, 0))),
        ],
        out_specs=pl.BlockSpec((tm, tn), lambda i, j: (i, j)),
        out_shape=jax.ShapeDtypeStruct((T, D), jnp.float32),
        compiler_params=_cparams("arbitrary", "arbitrary"),
        name="out_proj_residual",
    )(mix, w_out, xa, xb)


def _router_kernel(x_ref, g_ref, wr_ref, br_ref, hp_ref, idx_ref, gate_ref):
    x = x_ref[...]
    ms = jnp.mean(x * x, axis=-1, keepdims=True)
    h = x * lax.rsqrt(ms + RMS_EPS) * g_ref[...]
    half = h.shape[1] // 2
    hp_ref[...] = _pack_halves(h[:, :half], h[:, half:])
    logits = jnp.dot(h.astype(jnp.bfloat16), wr_ref[...], preferred_element_type=jnp.float32) + br_ref[...]
    n_exp = logits.shape[1]
    lane = lax.broadcasted_iota(jnp.int32, logits.shape, 1)
    vals, idxs = [], []
    for _ in range(TOP_K):
        m = jnp.max(logits, axis=-1, keepdims=True)
        idx = jnp.min(jnp.where(logits == m, lane, n_exp), axis=-1, keepdims=True)
        vals.append(m)
        idxs.append(idx)
        logits = jnp.where(lane == idx, -jnp.inf, logits)
    exps = [jnp.exp(v - vals[0]) for v in vals]
    total = sum(exps)
    for k in range(TOP_K):
        idx_ref[:, k:k + 1] = idxs[k]
        gate_ref[:, k:k + 1] = exps[k] / total


def _router(x1, gain, w_router, b_router, tr=256):
    T, D = x1.shape
    E = w_router.shape[1]
    return pl.pallas_call(
        _router_kernel,
        grid=(T // tr,),
        in_specs=[
            pl.BlockSpec((tr, D), lambda i: (i, 0)),
            pl.BlockSpec((1, D), lambda i: (0, 0)),
            pl.BlockSpec((D, E), lambda i: (0, 0)),
            pl.BlockSpec((1, E), lambda i: (0, 0)),
        ],
        out_specs=[
            pl.BlockSpec((tr, D // 2), lambda i: (i, 0)),
            pl.BlockSpec((tr, TOP_K), lambda i: (i, 0)),
            pl.BlockSpec((tr, TOP_K), lambda i: (i, 0)),
        ],
        out_shape=[
            jax.ShapeDtypeStruct((T, D // 2), jnp.uint32),
            jax.ShapeDtypeStruct((T, TOP_K), jnp.int32),
            jax.ShapeDtypeStruct((T, TOP_K), jnp.float32),
        ],
        compiler_params=_cparams("arbitrary"),
        name="router_topk",
    )(x1, gain.reshape(1, D), w_router, b_router.reshape(1, E))


def _routing_tables(top_idx, n_experts):
    e = top_idx.reshape(-1)
    onehot = (e[:, None] == jnp.arange(n_experts, dtype=jnp.int32)[None, :]).astype(jnp.int32)
    csum = jnp.cumsum(onehot, axis=0)
    counts = csum[-1]
    tiles_per_expert = (counts + EXPERT_TILE - 1) // EXPERT_TILE
    padded = tiles_per_expert * EXPERT_TILE
    pend = jnp.cumsum(padded)
    pstart = pend - padded
    pos = jnp.sum(onehot * (csum - 1 + pstart[None, :]), axis=1).astype(jnp.int32)
    pad_lo = (pstart + counts).astype(jnp.int32)
    pad_hi = pstart + (counts + EXPERT_ROW_BLOCK - 1) // EXPERT_ROW_BLOCK * EXPERT_ROW_BLOCK
    return (pos, pad_lo, pad_hi.astype(jnp.int32), counts.astype(jnp.int32), tiles_per_expert.astype(jnp.int32),
            (pstart // EXPERT_TILE).astype(jnp.int32))


def _dispatch_kernel(pos_ref, pad_lo_ref, pad_hi_ref, hp_ref, xs_ref, zero_sc, sem, *, n_experts):
    step = pl.program_id(0)
    tt = hp_ref.shape[0]

    def row_copy(tok, dst_row):
        return pltpu.make_async_copy(hp_ref.at[pl.ds(tok, 1)], xs_ref.at[pl.ds(dst_row, 1)], sem)

    def issue(tok, carry):
        for k in range(TOP_K):
            row_copy(tok, pos_ref[0, 0, tok * TOP_K + k]).start(priority=k % 2)
        return carry

    lax.fori_loop(0, tt, issue, 0, unroll=4)

    def drain(tok, carry):
        for k in range(TOP_K):
            row_copy(0, 0).wait()
        return carry

    lax.fori_loop(0, tt, drain, 0, unroll=4)

    @pl.when(step == pl.num_programs(0) - 1)
    def _():
        zero_sc[...] = jnp.zeros_like(zero_sc)

        def pad_copy(dst_row):
            return pltpu.make_async_copy(zero_sc.at[pl.ds(0, 1)], xs_ref.at[pl.ds(dst_row, 1)], sem)

        for e in range(n_experts):
            lo = pad_lo_ref[e]
            hi = pad_hi_ref[e]

            def fill(p, carry):
                pad_copy(p).start()
                return carry

            lax.fori_loop(lo, hi, fill, 0)

            def fill_wait(p, carry):
                pad_copy(0).wait()
                return carry

            lax.fori_loop(lo, hi, fill_wait, 0)


def _dispatch(hp, pos, pad_lo, pad_hi, n_slots, tt=DISPATCH_TILE):
    T, W = hp.shape
    n_experts = pad_lo.shape[0]
    grid_spec = pltpu.PrefetchScalarGridSpec(
        num_scalar_prefetch=0,
        grid=(T // tt,),
        in_specs=[
            pl.BlockSpec((1, 1, tt * TOP_K), lambda s: (s, 0, 0), memory_space=pltpu.SMEM),
            pl.BlockSpec(memory_space=pltpu.SMEM),
            pl.BlockSpec(memory_space=pltpu.SMEM),
            pl.BlockSpec((tt, W), lambda s: (s, 0)),
        ],
        out_specs=pl.BlockSpec(memory_space=pl.ANY),
        scratch_shapes=[pltpu.VMEM((8, W), jnp.uint32), pltpu.SemaphoreType.DMA(())],
    )
    return pl.pallas_call(
        functools.partial(_dispatch_kernel, n_experts=n_experts),
        grid_spec=grid_spec,
        out_shape=jax.ShapeDtypeStruct((n_slots, W), jnp.uint32),
        compiler_params=_cparams("arbitrary"),
        name="moe_dispatch",
    )(pos.reshape(T // tt, 1, tt * TOP_K), pad_lo, pad_hi, hp)


def _expert_steps(counts, tiles_per_expert, tile_base, nj, n_tiles):
    n_steps = n_tiles * nj
    steps_e = tiles_per_expert * nj
    send = jnp.cumsum(steps_e)
    total = send[-1]
    step = jnp.arange(n_steps, dtype=jnp.int32)
    s = jnp.minimum(step, total - 1)
    e = jnp.sum((send[None, :] <= s[:, None]).astype(jnp.int32), axis=1)
    local = s - (send - steps_e)[e]
    nt = jnp.maximum(tiles_per_expert[e], 1)
    j = local // nt
    t_local = local - j * nt
    tile = tile_base[e] + t_local
    used = step < total
    first = used & (t_local == 0)
    slot = (jnp.cumsum(first.astype(jnp.int32)) - 1) % 2
    s_next = s - t_local + nt
    has_next = first & (s_next < total)
    s_next = jnp.minimum(s_next, total - 1)
    real_rows = jnp.clip(counts[e] - t_local * EXPERT_TILE, 1, EXPERT_TILE)
    row_blocks = (real_rows + EXPERT_ROW_BLOCK - 1) // EXPERT_ROW_BLOCK
    i32 = lambda a: a.astype(jnp.int32)
    return (i32(e), i32(j), i32(tile), i32(used), i32(first), i32(slot), i32(e[s_next]), i32(j[s_next]),
            i32(has_next), i32(row_blocks))


def _dot_f32_weight(x_parts, w_ref, part_rows):
    acc = None
    for p, x in enumerate(x_parts):
        for k0 in range(0, part_rows, WEIGHT_CAST_ROWS):
            w = w_ref[p * part_rows + k0:p * part_rows + k0 + WEIGHT_CAST_ROWS, :].astype(jnp.bfloat16)
            d = jnp.dot(x[:, k0:k0 + WEIGHT_CAST_ROWS], w, preferred_element_type=jnp.float32)
            acc = d if acc is None else acc + d
    return acc


def _weight_prefetch(tabs, w_hbm, wbuf, sems, col_offsets, tn):
    e_ref, j_ref, _, _, first_ref, slot_ref, ne_ref, nj_ref, has_next_ref, _ = tabs
    s = pl.program_id(0)
    slot = slot_ref[s]

    def copies(e, j, sl):
        return [pltpu.make_async_copy(w_hbm.at[e, :, pl.ds(pl.multiple_of(off + j * tn, tn), tn)],
                                      wbuf.at[sl, m], sems.at[sl])
                for m, off in enumerate(col_offsets)]

    @pl.when(s == 0)
    def _():
        for c in copies(e_ref[0], j_ref[0], 0):
            c.start(priority=1)

    @pl.when(first_ref[s] == 1)
    def _():
        for c in copies(e_ref[s], j_ref[s], slot):
            c.wait()

        @pl.when(has_next_ref[s] == 1)
        def _():
            for c in copies(ne_ref[s], nj_ref[s], 1 - slot):
                c.start(priority=1)

    return slot


N_STEP_TABLES = 10


def _run_tile_rows(tabs, body):
    s = pl.program_id(0)
    used = tabs[3][s] == 1
    n_blocks = tabs[9][s]
    for nb in range(1, EXPERT_TILE // EXPERT_ROW_BLOCK + 1):
        pl.when(used & (n_blocks == nb))(functools.partial(body, slice(0, nb * EXPERT_ROW_BLOCK)))


def _gate_up_kernel(*refs, tn, up_offset):
    tabs = refs[:N_STEP_TABLES]
    xs_ref, w_hbm, bg_ref, bu_ref, o_ref, wbuf, sems = refs[N_STEP_TABLES:]
    slot = _weight_prefetch(tabs, w_hbm, wbuf, sems, (0, up_offset), tn)
    half = xs_ref.shape[1]

    def body(rows):
        lo, hi = _unpack_halves(xs_ref[rows, :])
        x_parts = (lo.astype(jnp.bfloat16), hi.astype(jnp.bfloat16))
        gate = jnp.minimum(_dot_f32_weight(x_parts, wbuf.at[slot, 0], half) + bg_ref[0], SWIGLU_LIMIT)
        lin = jnp.clip(_dot_f32_weight(x_parts, wbuf.at[slot, 1], half) + bu_ref[0], -SWIGLU_LIMIT, SWIGLU_LIMIT)
        o_ref[rows, :] = (gate * jax.nn.sigmoid(SWIGLU_ALPHA * gate) * (lin + 1.0)).astype(o_ref.dtype)
        if rows.stop < EXPERT_TILE:
            o_ref[rows.stop:, :] = jnp.zeros((EXPERT_TILE - rows.stop, o_ref.shape[1]), o_ref.dtype)

    _run_tile_rows(tabs, body)


def _expert_gate_up(xs, w_gu, b_gu, counts, tiles_per_expert, tile_base, tn=512):
    n_slots, half = xs.shape
    E, D, F2 = w_gu.shape
    F = F2 // 2
    tn = min(tn, F)
    nj = F // tn
    n_tiles = n_slots // EXPERT_TILE
    tables = _expert_steps(counts, tiles_per_expert, tile_base, nj, n_tiles)

    def bmap(off):
        return lambda s, e, j, *_: (e[s], 0, off + j[s])

    grid_spec = pltpu.PrefetchScalarGridSpec(
        num_scalar_prefetch=N_STEP_TABLES,
        grid=(n_tiles * nj,),
        in_specs=[
            pl.BlockSpec((EXPERT_TILE, half), lambda s, e, j, t, *_: (t[s], 0)),
            pl.BlockSpec(memory_space=pl.ANY),
            pl.BlockSpec((1, 1, tn), bmap(0)),
            pl.BlockSpec((1, 1, tn), bmap(nj)),
        ],
        out_specs=pl.BlockSpec((EXPERT_TILE, tn), lambda s, e, j, t, *_: (t[s], j[s])),
        scratch_shapes=[pltpu.VMEM((2, 2, D, tn), jnp.float32), pltpu.SemaphoreType.DMA((2,))],
    )
    return pl.pallas_call(
        functools.partial(_gate_up_kernel, tn=tn, up_offset=F),
        grid_spec=grid_spec,
        out_shape=jax.ShapeDtypeStruct((n_slots, F), jnp.bfloat16),
        compiler_params=_cparams("arbitrary"),
        name="expert_gate_up",
    )(*tables, xs, w_gu, b_gu.reshape(E, 1, F2), b_gu.reshape(E, 1, F2))


def _down_kernel(*refs, tn, hi_offset):
    tabs = refs[:N_STEP_TABLES]
    a_ref, w_hbm, blo_ref, bhi_ref, o_ref, wbuf, sems = refs[N_STEP_TABLES:]
    slot = _weight_prefetch(tabs, w_hbm, wbuf, sems, (0, hi_offset), tn)
    k_rows = a_ref.shape[1]

    def body(rows):
        a = (a_ref[rows, :],)
        y_lo = _dot_f32_weight(a, wbuf.at[slot, 0], k_rows) + blo_ref[0]
        y_hi = _dot_f32_weight(a, wbuf.at[slot, 1], k_rows) + bhi_ref[0]
        o_ref[rows, :] = _pack_halves(y_lo, y_hi)
        if rows.stop < EXPERT_TILE:
            o_ref[rows.stop:, :] = jnp.zeros((EXPERT_TILE - rows.stop, o_ref.shape[1]), o_ref.dtype)

    _run_tile_rows(tabs, body)


def _expert_down(act, w_down, b_down, counts, tiles_per_expert, tile_base, tn=512):
    n_slots, F = act.shape
    E, _, D = w_down.shape
    half = D // 2
    tn = min(tn, half)
    nj = half // tn
    n_tiles = n_slots // EXPERT_TILE
    tables = _expert_steps(counts, tiles_per_expert, tile_base, nj, n_tiles)

    def bmap(off):
        return lambda s, e, j, *_: (e[s], 0, off + j[s])

    grid_spec = pltpu.PrefetchScalarGridSpec(
        num_scalar_prefetch=N_STEP_TABLES,
        grid=(n_tiles * nj,),
        in_specs=[
            pl.BlockSpec((EXPERT_TILE, F), lambda s, e, j, t, *_: (t[s], 0)),
            pl.BlockSpec(memory_space=pl.ANY),
            pl.BlockSpec((1, 1, tn), bmap(0)),
            pl.BlockSpec((1, 1, tn), bmap(nj)),
        ],
        out_specs=pl.BlockSpec((EXPERT_TILE, tn), lambda s, e, j, t, *_: (t[s], j[s])),
        scratch_shapes=[pltpu.VMEM((2, 2, F, tn), jnp.float32), pltpu.SemaphoreType.DMA((2,))],
    )
    return pl.pallas_call(
        functools.partial(_down_kernel, tn=tn, hi_offset=half),
        grid_spec=grid_spec,
        out_shape=jax.ShapeDtypeStruct((n_slots, half), jnp.uint32),
        compiler_params=_cparams("arbitrary"),
        name="expert_down",
    )(*tables, act, w_down, b_down.reshape(E, 1, D), b_down.reshape(E, 1, D))


def _combine_kernel(pos_ref, pos_next_ref, x_ref, gate_ref, ys_ref, o_ref, buf, sems, *, tt):
    i = pl.program_id(0)
    slot = i % 2

    def row_copy(src_row, s, k, tok):
        return pltpu.make_async_copy(ys_ref.at[pl.ds(src_row, 1)], buf.at[s, k, pl.ds(tok, 1)], sems.at[s])

    def fetch(p_ref, s):
        def issue(tok, carry):
            for k in range(TOP_K):
                row_copy(p_ref[0, 0, tok * TOP_K + k], s, k, tok).start(priority=k % 2)
            return carry
        lax.fori_loop(0, tt, issue, 0, unroll=4)

    pl.when(i == 0)(lambda: fetch(pos_ref, 0))
    pl.when(i + 1 < pl.num_programs(0))(lambda: fetch(pos_next_ref, 1 - slot))

    def drain(tok, carry):
        for k in range(TOP_K):
            row_copy(0, slot, k, 0).wait()
        return carry

    lax.fori_loop(0, tt, drain, 0, unroll=4)

    half = buf.shape[3]
    acc_lo = x_ref[:, :half]
    acc_hi = x_ref[:, half:]
    for k in range(TOP_K):
        lo, hi = _unpack_halves(buf[slot, k])
        gk = gate_ref[:, k:k + 1]
        acc_lo = acc_lo + gk * lo
        acc_hi = acc_hi + gk * hi
    o_ref[:, :half] = acc_lo
    o_ref[:, half:] = acc_hi


def _combine(x1, gates, pos, ys, row0, n_rows, tt=COMBINE_TILE):
    T, D = x1.shape
    half = D // 2
    b0 = row0 // tt
    n_steps = n_rows // tt
    pos_blocks = pos.reshape(T // tt, 1, tt * TOP_K)
    grid_spec = pltpu.PrefetchScalarGridSpec(
        num_scalar_prefetch=0,
        grid=(n_steps,),
        in_specs=[
            pl.BlockSpec((1, 1, tt * TOP_K), lambda i: (b0 + i, 0, 0), memory_space=pltpu.SMEM),
            pl.BlockSpec((1, 1, tt * TOP_K), lambda i: (b0 + jnp.minimum(i + 1, n_steps - 1), 0, 0),
                         memory_space=pltpu.SMEM),
            pl.BlockSpec((tt, D), lambda i: (b0 + i, 0)),
            pl.BlockSpec((tt, TOP_K), lambda i: (b0 + i, 0)),
            pl.BlockSpec(memory_space=pl.ANY),
        ],
        out_specs=pl.BlockSpec((tt, D), lambda i: (i, 0)),
        scratch_shapes=[pltpu.VMEM((2, TOP_K, tt, half), jnp.uint32), pltpu.SemaphoreType.DMA((2,))],
    )
    return pl.pallas_call(
        functools.partial(_combine_kernel, tt=tt),
        grid_spec=grid_spec,
        out_shape=jax.ShapeDtypeStruct((n_rows, D), jnp.float32),
        compiler_params=_cparams("arbitrary"),
        name="moe_combine",
    )(pos_blocks, pos_blocks, x1, gates, ys)


def _sequence_tables(seq_lens, tile):
    lo, hi, start = [], [], 0
    for n in seq_lens:
        assert n % tile == 0
        lo += [start] * (n // tile)
        hi += [start + n] * (n // tile)
        start += n
    return jnp.asarray(np.array(lo, np.int32)), jnp.asarray(np.array(hi, np.int32))


def _rope_tables(seq_lens):
    half = HEAD_DIM // 2
    inv_freq = ROPE_THETA ** (-jnp.arange(half, dtype=jnp.float32) / half)
    pos = jnp.concatenate([jnp.arange(n, dtype=jnp.float32) for n in seq_lens])
    ang = pos[:, None] * inv_freq[None, :]
    cos, sin = jnp.cos(ang), jnp.sin(ang)
    return jnp.concatenate([cos, cos], axis=1), jnp.concatenate([-sin, sin], axis=1)


def _encoder_layer(xa, xb, seq_lens, norm_mix, w_in, q_norm, k_norm, w_pool, pool_scale, w_branch_a,
                   w_branch_b, w_out, norm_ffn, w_router, b_router, w_gate_up, b_gate_up, w_down, b_down):
    Ta, D = xa.shape
    Tb = xb.shape[0]
    T = Ta + Tb
    bf16 = jnp.bfloat16
    A_out = w_branch_a.shape[0]
    H = A_out // HEAD_DIM
    A = N_GROUPS * A_out
    P = w_branch_b.shape[0]
    E = w_router.shape[1]

    w_in_b = w_in.astype(bf16)
    gain_cols = jnp.concatenate([jnp.tile(q_norm, (1, H)).reshape(1, A),
                                 jnp.tile(k_norm, (1, H)).reshape(1, A)], axis=1)
    cos_t, sin_t = _rope_tables(seq_lens)

    h = _rmsnorm(xa, xb, norm_mix)
    qk = _qk_proj(h, w_in_b, gain_cols, cos_t, sin_t, 2 * A)
    v = _v_proj(h, w_in_b, 2 * A, A)
    u = _u_proj(h, w_in_b, 3 * A, P)

    lo_a, hi_a = _sequence_tables(seq_lens, ATTN_CHUNK)
    o_attn = _attention(qk, v, lo_a, hi_a, H)
    tp = 512
    lo_p, hi_p = _sequence_tables(seq_lens, tp)
    o_pool = _pool(u, w_pool.astype(bf16), pool_scale, lo_p, hi_p, tp)

    mix = _mix(h, w_in_b, 3 * A + P, o_attn, w_branch_a.astype(bf16), o_pool, w_branch_b.astype(bf16))
    x1 = _out_proj(mix, w_out.astype(bf16), xa, xb)

    hp, top_idx, gates = _router(x1, norm_ffn, w_router.astype(bf16), b_router)
    n_assign = T * TOP_K
    n_tiles = n_assign // EXPERT_TILE + E
    pos, pad_lo, pad_hi, counts, tiles_per_expert, tile_base = _routing_tables(top_idx, E)
    xs = _dispatch(hp, pos, pad_lo, pad_hi, n_tiles * EXPERT_TILE)
    act = _expert_gate_up(xs, w_gate_up, b_gate_up, counts, tiles_per_expert, tile_base)
    ys = _expert_down(act, w_down, b_down, counts, tiles_per_expert, tile_base)
    return _combine(x1, gates, pos, ys, 0, Ta), _combine(x1, gates, pos, ys, Ta, Tb)


def kernel(x_prompt, x_sample, norm_mix, w_in, q_norm, k_norm, w_pool, pool_scale, w_branch_a, w_branch_b,
           w_out, norm_ffn, w_router, b_router, w_gate_up, b_gate_up, w_down, b_down):
    depth = norm_mix.shape[0]
    D = x_prompt.shape[-1]
    seq_lens = (x_prompt.shape[1],) * x_prompt.shape[0] + (x_sample.shape[1],) * x_sample.shape[0]
    xa, xb = x_prompt.reshape(-1, D), x_sample.reshape(-1, D)
    for layer in range(depth):
        xa, xb = _encoder_layer(xa, xb, seq_lens, norm_mix[layer], w_in[layer], q_norm[layer], k_norm[layer],
                                w_pool[layer], pool_scale[layer], w_branch_a[layer], w_branch_b[layer],
                                w_out[layer], norm_ffn[layer], w_router[layer], b_router[layer],
                                w_gate_up[layer], b_gate_up[layer], w_down[layer], b_down[layer])
    return xa.reshape(x_prompt.shape), xb.reshape(x_sample.shape)
```

```python
import functools

import numpy as np
import jax
import jax.numpy as jnp
from jax import lax
from jax.experimental import pallas as pl
from jax.experimental.pallas import tpu as pltpu

HEAD_DIM = 128
DILATION_PAIRS = ((128, 1), (512, 4), (2048, 16))
N_GROUPS = len(DILATION_PAIRS)
ATTN_SIDE = 64
ROPE_THETA = 10000.0
POOL_WINDOWS = (2, 4, 8, 16)
POOL_HALO = 8
TOP_K = 4
SWIGLU_LIMIT = 7.0
SWIGLU_ALPHA = 1.702
RMS_EPS = 1e-6

VMEM_LIMIT_BYTES = 62 * 1024 * 1024
PROJ_ROWS = 1024
PROJ_COL_TILES = (1024, 512)
PROJ_ROW_SPLIT = 512
QK_ROW_SPLIT = 128

ATTN_CHUNK = 2048
ATTN_QBLK = 128
ATTN_WIN = 256
EXPERT_TILE = 1024
EXPERT_ROW_BLOCK = 256
WEIGHT_CAST_ROWS = 512
DISPATCH_TILE = 512
COMBINE_TILE = 256


def _cparams(*sem):
    return pltpu.CompilerParams(dimension_semantics=sem, vmem_limit_bytes=VMEM_LIMIT_BYTES)


def _pack_halves(lo, hi):
    lo_bits = lax.bitcast_convert_type(lo.astype(jnp.bfloat16).astype(jnp.float32), jnp.uint32)
    hi_bits = lax.bitcast_convert_type(hi.astype(jnp.bfloat16).astype(jnp.float32), jnp.uint32)
    return (lo_bits >> 16) | (hi_bits & jnp.uint32(0xFFFF0000))


def _unpack_halves(packed):
    lo = lax.bitcast_convert_type(packed << 16, jnp.float32)
    hi = lax.bitcast_convert_type(packed & jnp.uint32(0xFFFF0000), jnp.float32)
    return lo, hi


def _rmsnorm_kernel(xa_ref, xb_ref, g_ref, o_ref, *, n_first):
    def body(x_ref):
        x = x_ref[...]
        ms = jnp.mean(x * x, axis=-1, keepdims=True)
        o_ref[...] = (x * lax.rsqrt(ms + RMS_EPS) * g_ref[...]).astype(o_ref.dtype)

    pl.when(pl.program_id(0) < n_first)(lambda: body(xa_ref))
    pl.when(pl.program_id(0) >= n_first)(lambda: body(xb_ref))


def _rmsnorm(xa, xb, gain, tm=256):
    D = xa.shape[1]
    T = xa.shape[0] + xb.shape[0]
    n_first = xa.shape[0] // tm
    return pl.pallas_call(
        functools.partial(_rmsnorm_kernel, n_first=n_first),
        grid=(T // tm,),
        in_specs=[pl.BlockSpec((tm, D), lambda i: (jnp.minimum(i, n_first - 1), 0)),
                  pl.BlockSpec((tm, D), lambda i: (jnp.maximum(i - n_first, 0), 0)),
                  pl.BlockSpec((1, D), lambda i: (0, 0))],
        out_specs=pl.BlockSpec((tm, D), lambda i: (i, 0)),
        out_shape=jax.ShapeDtypeStruct((T, D), jnp.bfloat16),
        compiler_params=_cparams("arbitrary"),
        name="rmsnorm",
    )(xa, xb, gain.reshape(1, D))


def _qk_proj_kernel(h_ref, w_ref, gain_ref, cos_ref, sin_ref, o_ref):
    w = w_ref[...]
    tm = h_ref.shape[0]
    for r0 in range(0, tm, QK_ROW_SPLIT):
        rows = slice(r0, r0 + QK_ROW_SPLIT)
        acc = jnp.dot(h_ref[rows, :], w, preferred_element_type=jnp.float32)
        cos = cos_ref[rows, :]
        sin = sin_ref[rows, :]
        for c in range(acc.shape[1] // HEAD_DIM):
            t = acc[:, c * HEAD_DIM:(c + 1) * HEAD_DIM]
            ms = jnp.mean(t * t, axis=-1, keepdims=True)
            t = t * lax.rsqrt(ms + RMS_EPS) * gain_ref[:, c * HEAD_DIM:(c + 1) * HEAD_DIM]
            o_ref[c, rows, :] = t * cos + pltpu.roll(t, HEAD_DIM // 2, axis=1) * sin


def _col_tile(ncols, col0=0):
    return next(t for t in PROJ_COL_TILES if ncols % t == 0 and col0 % t == 0)


def _qk_proj(h, w_in, gain_cols, cos_t, sin_t, ncols, tm=PROJ_ROWS):
    T, D = h.shape
    tn = _col_tile(ncols)
    n_heads = ncols // HEAD_DIM
    return pl.pallas_call(
        _qk_proj_kernel,
        grid=(T // tm, ncols // tn),
        in_specs=[
            pl.BlockSpec((tm, D), lambda i, j: (i, 0)),
            pl.BlockSpec((D, tn), lambda i, j: (0, j)),
            pl.BlockSpec((1, tn), lambda i, j: (0, j)),
            pl.BlockSpec((tm, HEAD_DIM), lambda i, j: (i, 0)),
            pl.BlockSpec((tm, HEAD_DIM), lambda i, j: (i, 0)),
        ],
        out_specs=pl.BlockSpec((tn // HEAD_DIM, tm, HEAD_DIM), lambda i, j: (j, i, 0)),
        out_shape=jax.ShapeDtypeStruct((n_heads, T, HEAD_DIM), jnp.float32),
        compiler_params=_cparams("arbitrary", "arbitrary"),
        name="qk_proj",
    )(h, w_in, gain_cols, cos_t, sin_t)


def _v_proj_kernel(h_ref, w_ref, o_ref):
    w = w_ref[...]
    for r0 in range(0, h_ref.shape[0], PROJ_ROW_SPLIT):
        rows = slice(r0, r0 + PROJ_ROW_SPLIT)
        acc = jnp.dot(h_ref[rows, :], w, preferred_element_type=jnp.float32)
        for c in range(acc.shape[1] // HEAD_DIM):
            o_ref[c, rows, :] = acc[:, c * HEAD_DIM:(c + 1) * HEAD_DIM]


def _v_proj(h, w_in, col0, ncols, tm=PROJ_ROWS):
    T, D = h.shape
    tn = _col_tile(ncols, col0)
    cb0 = col0 // tn
    return pl.pallas_call(
        _v_proj_kernel,
        grid=(T // tm, ncols // tn),
        in_specs=[
            pl.BlockSpec((tm, D), lambda i, j: (i, 0)),
            pl.BlockSpec((D, tn), lambda i, j: (0, cb0 + j)),
        ],
        out_specs=pl.BlockSpec((tn // HEAD_DIM, tm, HEAD_DIM), lambda i, j: (j, i, 0)),
        out_shape=jax.ShapeDtypeStruct((ncols // HEAD_DIM, T, HEAD_DIM), jnp.float32),
        compiler_params=_cparams("arbitrary", "arbitrary"),
        name="v_proj",
    )(h, w_in)


def _u_proj_kernel(h_ref, w_ref, o_ref):
    w = w_ref[...]
    for r0 in range(0, h_ref.shape[0], PROJ_ROW_SPLIT):
        rows = slice(r0, r0 + PROJ_ROW_SPLIT)
        o_ref[rows, :] = jnp.dot(h_ref[rows, :], w, preferred_element_type=jnp.float32)


def _u_proj(h, w_in, col0, ncols, tm=PROJ_ROWS):
    T, D = h.shape
    tn = _col_tile(ncols, col0)
    cb0 = col0 // tn
    return pl.pallas_call(
        _u_proj_kernel,
        grid=(T // tm, ncols // tn),
        in_specs=[
            pl.BlockSpec((tm, D), lambda i, j: (i, 0)),
            pl.BlockSpec((D, tn), lambda i, j: (0, cb0 + j)),
        ],
        out_specs=pl.BlockSpec((tm, tn), lambda i, j: (i, j)),
        out_shape=jax.ShapeDtypeStruct((T, ncols), jnp.float32),
        compiler_params=_cparams("arbitrary", "arbitrary"),
        name="u_proj",
    )(h, w_in)


def _attn_kernel(lo_ref, hi_ref, *refs, chunk):
    in_refs = refs[:7 * N_GROUPS]
    o_ref = refs[7 * N_GROUPS]
    o_sc, lse_sc = refs[7 * N_GROUPS + 1:]
    c = pl.program_id(0)
    t0 = c * chunk
    seq_lo = lo_ref[c]
    seq_hi = hi_ref[c]
    win = ATTN_WIN
    slack = win - ATTN_QBLK - 2 * ATTN_SIDE
    qq = lax.broadcasted_iota(jnp.int32, (ATTN_QBLK, win), 0)
    kk = lax.broadcasted_iota(jnp.int32, (ATTN_QBLK, win), 1)
    band_bias = jnp.where(jnp.abs(kk - ATTN_SIDE - qq) <= ATTN_SIDE, 0.0, -jnp.inf)
    k_row = lax.broadcasted_iota(jnp.int32, (1, win), 1)
    scale = HEAD_DIM ** -0.5

    for g, (_, d) in enumerate(DILATION_PAIRS):
        q_ref, kc_ref, kl_ref, kr_ref, vc_ref, vl_ref, vr_ref = in_refs[7 * g:7 * g + 7]
        shift = d.bit_length() - 1
        rows = chunk // d
        m0 = t0 >> shift
        m_lo = seq_lo >> shift
        m_hi = seq_hi >> shift
        for r in range(d):
            def sub(ref, n, r=r, d=d):
                return ref[0, pl.ds(r, n, stride=d), :]
            q_r = sub(q_ref, rows).astype(jnp.bfloat16)
            def padded(l_ref, c_ref, r_ref):
                parts = [sub(l_ref, ATTN_SIDE), sub(c_ref, rows), sub(r_ref, ATTN_SIDE)]
                if slack:
                    parts.append(sub(r_ref, slack))
                return jnp.concatenate(parts, axis=0).astype(jnp.bfloat16)
            k_r = padded(kl_ref, kc_ref, kr_ref)
            v_r = padded(vl_ref, vc_ref, vr_ref)
            for b in range(rows // ATTN_QBLK):
                qs = q_r[b * ATTN_QBLK:(b + 1) * ATTN_QBLK]
                ks = k_r[b * ATTN_QBLK:b * ATTN_QBLK + win]
                vs = v_r[b * ATTN_QBLK:b * ATTN_QBLK + win]
                key_m = k_row + (m0 + (b * ATTN_QBLK - ATTN_SIDE))
                ends_bias = jnp.where((key_m >= m_lo) & (key_m < m_hi), 0.0, -jnp.inf)
                s = lax.dot_general(qs, ks, (((1,), (1,)), ((), ())),
                                    preferred_element_type=jnp.float32) * scale + band_bias + ends_bias
                m = jnp.max(s, axis=-1, keepdims=True)
                p = jnp.exp(s - m)
                denom = jnp.sum(p, axis=-1, keepdims=True)
                o = jnp.dot(p.astype(jnp.bfloat16), vs, preferred_element_type=jnp.float32) / denom
                lse = m + jnp.log(denom)
                dst = pl.ds(b * ATTN_QBLK * d + r, ATTN_QBLK, stride=d)
                o_sc[g, dst, :] = o
                lse_sc[g, dst, :] = jnp.broadcast_to(lse, (ATTN_QBLK, HEAD_DIM))

    lses = [lse_sc[g] for g in range(N_GROUPS)]
    top = functools.reduce(jnp.maximum, lses)
    ws = [jnp.exp(l - top) for l in lses]
    num = sum(w * o_sc[g] for g, w in enumerate(ws))
    o_ref[...] = (num / sum(ws)).astype(o_ref.dtype)


def _attention(qk, v, seq_lo, seq_hi, heads_per_group):
    n_qk, T, _ = qk.shape
    H = heads_per_group
    k_base = n_qk // 2
    chunk = ATTN_CHUNK
    n_chunks = T // chunk
    in_specs = []
    operands = []
    for g, (_, d) in enumerate(DILATION_PAIRS):
        halo = ATTN_SIDE * d
        per = chunk // halo
        n_halo = T // halo

        def center(base, g=g):
            return pl.BlockSpec((1, chunk, HEAD_DIM), lambda c, h, lo, hi: (base + g * H + h, c, 0))

        def left(base, g=g, per=per):
            return pl.BlockSpec((1, halo, HEAD_DIM),
                                lambda c, h, lo, hi: (base + g * H + h, jnp.maximum(c * per - 1, 0), 0))

        def right(base, g=g, per=per, n_halo=n_halo):
            return pl.BlockSpec((1, halo, HEAD_DIM),
                                lambda c, h, lo, hi: (base + g * H + h, jnp.minimum((c + 1) * per, n_halo - 1), 0))

        in_specs += [center(0), center(k_base), left(k_base), right(k_base), center(0), left(0), right(0)]
        operands += [qk, qk, qk, qk, v, v, v]
    grid_spec = pltpu.PrefetchScalarGridSpec(
        num_scalar_prefetch=2,
        grid=(n_chunks, H),
        in_specs=in_specs,
        out_specs=pl.BlockSpec((chunk, HEAD_DIM), lambda c, h, lo, hi: (c, h)),
        scratch_shapes=[pltpu.VMEM((N_GROUPS, chunk, HEAD_DIM), jnp.float32),
                        pltpu.VMEM((N_GROUPS, chunk, HEAD_DIM), jnp.float32)],
    )
    return pl.pallas_call(
        functools.partial(_attn_kernel, chunk=chunk),
        grid_spec=grid_spec,
        out_shape=jax.ShapeDtypeStruct((T, H * HEAD_DIM), jnp.bfloat16),
        compiler_params=_cparams("arbitrary", "arbitrary"),
        name="dilated_attention",
    )(seq_lo, seq_hi, *operands)


def _pool_kernel(lo_ref, hi_ref, uc_ref, ul_ref, ur_ref, w_ref, scale_ref, o_ref, ext_sc, *, tp):
    i = pl.program_id(0)
    t0 = i * tp
    seq_lo = lo_ref[i]
    seq_hi = hi_ref[i]
    gw = w_ref.shape[1]
    pos_l = t0 - POOL_HALO + lax.broadcasted_iota(jnp.int32, (POOL_HALO, 1), 0)
    pos_r = t0 + tp + lax.broadcasted_iota(jnp.int32, (POOL_HALO, 1), 0)
    ext_sc[0:POOL_HALO, :] = jnp.where(pos_l >= seq_lo, ul_ref[...], 0.0)
    ext_sc[POOL_HALO:POOL_HALO + tp, :] = uc_ref[...]
    ext_sc[POOL_HALO + tp:, :] = jnp.where(pos_r < seq_hi, ur_ref[...], 0.0)
    pos = t0 + lax.broadcasted_iota(jnp.int32, (tp, 1), 0)
    for gi, w in enumerate(POOL_WINDOWS):
        cols = slice(gi * gw, (gi + 1) * gw)
        total = None
        for j in range(-(w // 2), w - w // 2):
            piece = ext_sc[POOL_HALO + j:POOL_HALO + j + tp, cols]
            total = piece if total is None else total + piece
        cnt = jnp.minimum(pos + (w - w // 2), seq_hi) - jnp.maximum(pos - w // 2, seq_lo)
        pooled = total / cnt.astype(jnp.float32) - uc_ref[:, cols]
        mixed = jnp.dot(pooled.astype(jnp.bfloat16), w_ref[gi], preferred_element_type=jnp.float32)
        o_ref[:, cols] = (mixed * scale_ref[:, cols]).astype(o_ref.dtype)


def _pool(u, w_pool, pool_scale, seq_lo, seq_hi, tp):
    T, P = u.shape
    per = tp // POOL_HALO
    n_halo = T // POOL_HALO
    grid_spec = pltpu.PrefetchScalarGridSpec(
        num_scalar_prefetch=2,
        grid=(T // tp,),
        in_specs=[
            pl.BlockSpec((tp, P), lambda i, lo, hi: (i, 0)),
            pl.BlockSpec((POOL_HALO, P), lambda i, lo, hi: (jnp.maximum(i * per - 1, 0), 0)),
            pl.BlockSpec((POOL_HALO, P), lambda i, lo, hi: (jnp.minimum((i + 1) * per, n_halo - 1), 0)),
            pl.BlockSpec(w_pool.shape, lambda i, lo, hi: (0, 0, 0)),
            pl.BlockSpec((1, P), lambda i, lo, hi: (0, 0)),
        ],
        out_specs=pl.BlockSpec((tp, P), lambda i, lo, hi: (i, 0)),
        scratch_shapes=[pltpu.VMEM((tp + 2 * POOL_HALO, P), jnp.float32)],
    )
    return pl.pallas_call(
        functools.partial(_pool_kernel, tp=tp),
        grid_spec=grid_spec,
        out_shape=jax.ShapeDtypeStruct((T, P), jnp.bfloat16),
        compiler_params=_cparams("arbitrary"),
        name="multiscale_pool",
    )(seq_lo, seq_hi, u, u, u, w_pool, pool_scale.reshape(1, P))


def _mix_kernel(h_ref, wga_ref, wgb_ref, oa_ref, wa_ref, ob_ref, wb_ref, o_ref):
    for r0 in range(0, h_ref.shape[0], PROJ_ROW_SPLIT):
        rows = slice(r0, r0 + PROJ_ROW_SPLIT)
        h = h_ref[rows, :]
        g_a = jnp.dot(h, wga_ref[...], preferred_element_type=jnp.float32)
        g_b = jnp.dot(h, wgb_ref[...], preferred_element_type=jnp.float32)
        a = jnp.dot(oa_ref[rows, :], wa_ref[...], preferred_element_type=jnp.float32)
        b = jnp.dot(ob_ref[rows, :], wb_ref[...], preferred_element_type=jnp.float32)
        o_ref[rows, :] = (jax.nn.sigmoid(g_a) * a + jax.nn.sigmoid(g_b) * b).astype(o_ref.dtype)


def _mix(h, w_in, ga_col0, o_attn, w_a, o_pool, w_b, tm=PROJ_ROWS, tn=512):
    T, D = h.shape
    cb_a = ga_col0 // tn
    cb_b = (ga_col0 + D) // tn
    A = o_attn.shape[1]
    P = o_pool.shape[1]
    return pl.pallas_call(
        _mix_kernel,
        grid=(T // tm, D // tn),
        in_specs=[
            pl.BlockSpec((tm, D), lambda i, j: (i, 0)),
            pl.BlockSpec((D, tn), lambda i, j: (0, cb_a + j)),
            pl.BlockSpec((D, tn), lambda i, j: (0, cb_b + j)),
            pl.BlockSpec((tm, A), lambda i, j: (i, 0)),
            pl.BlockSpec((A, tn), lambda i, j: (0, j)),
            pl.BlockSpec((tm, P), lambda i, j: (i, 0)),
            pl.BlockSpec((P, tn), lambda i, j: (0, j)),
        ],
        out_specs=pl.BlockSpec((tm, tn), lambda i, j: (i, j)),
        out_shape=jax.ShapeDtypeStruct((T, D), jnp.bfloat16),
        compiler_params=_cparams("arbitrary", "arbitrary"),
        name="gated_mix",
    )(h, w_in, w_in, o_attn, w_a, o_pool, w_b)


def _out_proj_kernel(m_ref, w_ref, xa_ref, xb_ref, o_ref, *, n_first):
    i = pl.program_id(0)

    def body(x_ref):
        w = w_ref[...]
        half = m_ref.shape[0] // 2
        for rows in (slice(0, half), slice(half, 2 * half)):
            o_ref[rows, :] = x_ref[rows, :] + jnp.dot(m_ref[rows, :], w, preferred_element_type=jnp.float32)

    pl.when(i < n_first)(lambda: body(xa_ref))
    pl.when(i >= n_first)(lambda: body(xb_ref))


def _out_proj(mix, w_out, xa, xb, tm=PROJ_ROWS, tn=512):
    T, D = mix.shape
    tn = min(tn, D)
    nj = D // tn
    n_first = xa.shape[0] // tm
    return pl.pallas_call(
        functools.partial(_out_proj_kernel, n_first=n_first),
        grid=(T // tm, nj),
        in_specs=[
            pl.BlockSpec((tm, D), lambda i, j: (i, 0)),
            pl.BlockSpec((D, tn), lambda i, j: (0, j)),
            pl.BlockSpec((tm, tn), lambda i, j: (jnp.minimum(i, n_first - 1), jnp.where(i < n_first, j, nj - 1))),
            pl.BlockSpec((tm, tn), lambda i, j: (jnp.maximum(i - n_first, 0), jnp.where(i >= n_first, j, 0))),
        ],
        out_specs=pl.BlockSpec((tm, tn), lambda i, j: (i, j)),
        out_shape=jax.ShapeDtypeStruct((T, D), jnp.float32),
        compiler_params=_cparams("arbitrary", "arbitrary"),
        name="out_proj_residual",
    )(mix, w_out, xa, xb)


def _router_kernel(x_ref, g_ref, wr_ref, br_ref, hp_ref, idx_ref, gate_ref):
    x = x_ref[...]
    ms = jnp.mean(x * x, axis=-1, keepdims=True)
    h = x * lax.rsqrt(ms + RMS_EPS) * g_ref[...]
    half = h.shape[1] // 2
    hp_ref[...] = _pack_halves(h[:, :half], h[:, half:])
    logits = jnp.dot(h.astype(jnp.bfloat16), wr_ref[...], preferred_element_type=jnp.float32) + br_ref[...]
    n_exp = logits.shape[1]
    lane = lax.broadcasted_iota(jnp.int32, logits.shape, 1)
    vals, idxs = [], []
    for _ in range(TOP_K):
        m = jnp.max(logits, axis=-1, keepdims=True)
        idx = jnp.min(jnp.where(logits == m, lane, n_exp), axis=-1, keepdims=True)
        vals.append(m)
        idxs.append(idx)
        logits = jnp.where(lane == idx, -jnp.inf, logits)
    exps = [jnp.exp(v - vals[0]) for v in vals]
    total = sum(exps)
    for k in range(TOP_K):
        idx_ref[:, k:k + 1] = idxs[k]
        gate_ref[:, k:k + 1] = exps[k] / total


def _router(x1, gain, w_router, b_router, tr=256):
    T, D = x1.shape
    E = w_router.shape[1]
    return pl.pallas_call(
        _router_kernel,
        grid=(T // tr,),
        in_specs=[
            pl.BlockSpec((tr, D), lambda i: (i, 0)),
            pl.BlockSpec((1, D), lambda i: (0, 0)),
            pl.BlockSpec((D, E), lambda i: (0, 0)),
            pl.BlockSpec((1, E), lambda i: (0, 0)),
        ],
        out_specs=[
            pl.BlockSpec((tr, D // 2), lambda i: (i, 0)),
            pl.BlockSpec((tr, TOP_K), lambda i: (i, 0)),
            pl.BlockSpec((tr, TOP_K), lambda i: (i, 0)),
        ],
        out_shape=[
            jax.ShapeDtypeStruct((T, D // 2), jnp.uint32),
            jax.ShapeDtypeStruct((T, TOP_K), jnp.int32),
            jax.ShapeDtypeStruct((T, TOP_K), jnp.float32),
        ],
        compiler_params=_cparams("arbitrary"),
        name="router_topk",
    )(x1, gain.reshape(1, D), w_router, b_router.reshape(1, E))


def _routing_tables(top_idx, n_experts):
    e = top_idx.reshape(-1)
    onehot = (e[:, None] == jnp.arange(n_experts, dtype=jnp.int32)[None, :]).astype(jnp.int32)
    csum = jnp.cumsum(onehot, axis=0)
    counts = csum[-1]
    tiles_per_expert = (counts + EXPERT_TILE - 1) // EXPERT_TILE
    padded = tiles_per_expert * EXPERT_TILE
    pend = jnp.cumsum(padded)
    pstart = pend - padded
    pos = jnp.sum(onehot * (csum - 1 + pstart[None, :]), axis=1).astype(jnp.int32)
    pad_lo = (pstart + counts).astype(jnp.int32)
    pad_hi = pstart + (counts + EXPERT_ROW_BLOCK - 1) // EXPERT_ROW_BLOCK * EXPERT_ROW_BLOCK
    return (pos, pad_lo, pad_hi.astype(jnp.int32), counts.astype(jnp.int32), tiles_per_expert.astype(jnp.int32),
            (pstart // EXPERT_TILE).astype(jnp.int32))


def _dispatch_kernel(pos_ref, pad_lo_ref, pad_hi_ref, hp_ref, xs_ref, zero_sc, sem, *, n_experts):
    step = pl.program_id(0)
    tt = hp_ref.shape[0]

    def row_copy(tok, dst_row):
        return pltpu.make_async_copy(hp_ref.at[pl.ds(tok, 1)], xs_ref.at[pl.ds(dst_row, 1)], sem)

    def issue(tok, carry):
        for k in range(TOP_K):
            row_copy(tok, pos_ref[0, 0, tok * TOP_K + k]).start(priority=k % 2)
        return carry

    lax.fori_loop(0, tt, issue, 0, unroll=4)

    def drain(tok, carry):
        for k in range(TOP_K):
            row_copy(0, 0).wait()
        return carry

    lax.fori_loop(0, tt, drain, 0, unroll=4)

    @pl.when(step == pl.num_programs(0) - 1)
    def _():
        zero_sc[...] = jnp.zeros_like(zero_sc)

        def pad_copy(dst_row):
            return pltpu.make_async_copy(zero_sc.at[pl.ds(0, 1)], xs_ref.at[pl.ds(dst_row, 1)], sem)

        for e in range(n_experts):
            lo = pad_lo_ref[e]
            hi = pad_hi_ref[e]

            def fill(p, carry):
                pad_copy(p).start()
                return carry

            lax.fori_loop(lo, hi, fill, 0)

            def fill_wait(p, carry):
                pad_copy(0).wait()
                return carry

            lax.fori_loop(lo, hi, fill_wait, 0)


def _dispatch(hp, pos, pad_lo, pad_hi, n_slots, tt=DISPATCH_TILE):
    T, W = hp.shape
    n_experts = pad_lo.shape[0]
    grid_spec = pltpu.PrefetchScalarGridSpec(
        num_scalar_prefetch=0,
        grid=(T // tt,),
        in_specs=[
            pl.BlockSpec((1, 1, tt * TOP_K), lambda s: (s, 0, 0), memory_space=pltpu.SMEM),
            pl.BlockSpec(memory_space=pltpu.SMEM),
            pl.BlockSpec(memory_space=pltpu.SMEM),
            pl.BlockSpec((tt, W), lambda s: (s, 0)),
        ],
        out_specs=pl.BlockSpec(memory_space=pl.ANY),
        scratch_shapes=[pltpu.VMEM((8, W), jnp.uint32), pltpu.SemaphoreType.DMA(())],
    )
    return pl.pallas_call(
        functools.partial(_dispatch_kernel, n_experts=n_experts),
        grid_spec=grid_spec,
        out_shape=jax.ShapeDtypeStruct((n_slots, W), jnp.uint32),
        compiler_params=_cparams("arbitrary"),
        name="moe_dispatch",
    )(pos.reshape(T // tt, 1, tt * TOP_K), pad_lo, pad_hi, hp)


def _expert_steps(counts, tiles_per_expert, tile_base, nj, n_tiles):
    n_steps = n_tiles * nj
    steps_e = tiles_per_expert * nj
    send = jnp.cumsum(steps_e)
    total = send[-1]
    step = jnp.arange(n_steps, dtype=jnp.int32)
    s = jnp.minimum(step, total - 1)
    e = jnp.sum((send[None, :] <= s[:, None]).astype(jnp.int32), axis=1)
    local = s - (send - steps_e)[e]
    nt = jnp.maximum(tiles_per_expert[e], 1)
    j = local // nt
    t_local = local - j * nt
    tile = tile_base[e] + t_local
    used = step < total
    first = used & (t_local == 0)
    slot = (jnp.cumsum(first.astype(jnp.int32)) - 1) % 2
    s_next = s - t_local + nt
    has_next = first & (s_next < total)
    s_next = jnp.minimum(s_next, total - 1)
    real_rows = jnp.clip(counts[e] - t_local * EXPERT_TILE, 1, EXPERT_TILE)
    row_blocks = (real_rows + EXPERT_ROW_BLOCK - 1) // EXPERT_ROW_BLOCK
    i32 = lambda a: a.astype(jnp.int32)
    return (i32(e), i32(j), i32(tile), i32(used), i32(first), i32(slot), i32(e[s_next]), i32(j[s_next]),
            i32(has_next), i32(row_blocks))


def _dot_f32_weight(x_parts, w_ref, part_rows):
    acc = None
    for p, x in enumerate(x_parts):
        for k0 in range(0, part_rows, WEIGHT_CAST_ROWS):
            w = w_ref[p * part_rows + k0:p * part_rows + k0 + WEIGHT_CAST_ROWS, :].astype(jnp.bfloat16)
            d = jnp.dot(x[:, k0:k0 + WEIGHT_CAST_ROWS], w, preferred_element_type=jnp.float32)
            acc = d if acc is None else acc + d
    return acc


def _weight_prefetch(tabs, w_hbm, wbuf, sems, col_offsets, tn):
    e_ref, j_ref, _, _, first_ref, slot_ref, ne_ref, nj_ref, has_next_ref, _ = tabs
    s = pl.program_id(0)
    slot = slot_ref[s]

    def copies(e, j, sl):
        return [pltpu.make_async_copy(w_hbm.at[e, :, pl.ds(pl.multiple_of(off + j * tn, tn), tn)],
                                      wbuf.at[sl, m], sems.at[sl])
                for m, off in enumerate(col_offsets)]

    @pl.when(s == 0)
    def _():
        for c in copies(e_ref[0], j_ref[0], 0):
            c.start(priority=1)

    @pl.when(first_ref[s] == 1)
    def _():
        for c in copies(e_ref[s], j_ref[s], slot):
            c.wait()

        @pl.when(has_next_ref[s] == 1)
        def _():
            for c in copies(ne_ref[s], nj_ref[s], 1 - slot):
                c.start(priority=1)

    return slot


N_STEP_TABLES = 10


def _run_tile_rows(tabs, body):
    s = pl.program_id(0)
    used = tabs[3][s] == 1
    n_blocks = tabs[9][s]
    for nb in range(1, EXPERT_TILE // EXPERT_ROW_BLOCK + 1):
        pl.when(used & (n_blocks == nb))(functools.partial(body, slice(0, nb * EXPERT_ROW_BLOCK)))


def _gate_up_kernel(*refs, tn, up_offset):
    tabs = refs[:N_STEP_TABLES]
    xs_ref, w_hbm, b_ref, o_ref, wbuf, sems = refs[N_STEP_TABLES:]
    slot = _weight_prefetch(tabs, w_hbm, wbuf, sems, (0, up_offset), tn)
    half = xs_ref.shape[1]
    j = tabs[1][pl.program_id(0)]
    b_gate = b_ref[0, pl.ds(j, 1), :]
    b_up = b_ref[0, pl.ds(up_offset // tn + j, 1), :]

    def body(rows):
        lo, hi = _unpack_halves(xs_ref[rows, :])
        x_parts = (lo.astype(jnp.bfloat16), hi.astype(jnp.bfloat16))
        gate = jnp.minimum(_dot_f32_weight(x_parts, wbuf.at[slot, 0], half) + b_gate, SWIGLU_LIMIT)
        lin = jnp.clip(_dot_f32_weight(x_parts, wbuf.at[slot, 1], half) + b_up, -SWIGLU_LIMIT, SWIGLU_LIMIT)
        o_ref[rows, :] = (gate * jax.nn.sigmoid(SWIGLU_ALPHA * gate) * (lin + 1.0)).astype(o_ref.dtype)
        if rows.stop < EXPERT_TILE:
            o_ref[rows.stop:, :] = jnp.zeros((EXPERT_TILE - rows.stop, o_ref.shape[1]), o_ref.dtype)

    _run_tile_rows(tabs, body)


def _expert_gate_up(xs, w_gu, b_gu, counts, tiles_per_expert, tile_base, tn=512):
    n_slots, half = xs.shape
    E, D, F2 = w_gu.shape
    F = F2 // 2
    tn = min(tn, F)
    nj = F // tn
    n_tiles = n_slots // EXPERT_TILE
    tables = _expert_steps(counts, tiles_per_expert, tile_base, nj, n_tiles)

    grid_spec = pltpu.PrefetchScalarGridSpec(
        num_scalar_prefetch=N_STEP_TABLES,
        grid=(n_tiles * nj,),
        in_specs=[
            pl.BlockSpec((EXPERT_TILE, half), lambda s, e, j, t, *_: (t[s], 0)),
            pl.BlockSpec(memory_space=pl.ANY),
            pl.BlockSpec((1, F2 // tn, tn), lambda s, e, *_: (e[s], 0, 0)),
        ],
        out_specs=pl.BlockSpec((EXPERT_TILE, tn), lambda s, e, j, t, *_: (t[s], j[s])),
        scratch_shapes=[pltpu.VMEM((2, 2, D, tn), jnp.float32), pltpu.SemaphoreType.DMA((2,))],
    )
    return pl.pallas_call(
        functools.partial(_gate_up_kernel, tn=tn, up_offset=F),
        grid_spec=grid_spec,
        out_shape=jax.ShapeDtypeStruct((n_slots, F), jnp.bfloat16),
        compiler_params=_cparams("arbitrary"),
        name="expert_gate_up",
    )(*tables, xs, w_gu, b_gu.reshape(E, F2 // tn, tn))


def _down_kernel(*refs, tn, hi_offset):
    tabs = refs[:N_STEP_TABLES]
    a_ref, w_hbm, b_ref, o_ref, wbuf, sems = refs[N_STEP_TABLES:]
    slot = _weight_prefetch(tabs, w_hbm, wbuf, sems, (0, hi_offset), tn)
    k_rows = a_ref.shape[1]
    j = tabs[1][pl.program_id(0)]
    b_lo = b_ref[0, pl.ds(j, 1), :]
    b_hi = b_ref[0, pl.ds(hi_offset // tn + j, 1), :]

    def body(rows):
        a = (a_ref[rows, :],)
        y_lo = _dot_f32_weight(a, wbuf.at[slot, 0], k_rows) + b_lo
        y_hi = _dot_f32_weight(a, wbuf.at[slot, 1], k_rows) + b_hi
        o_ref[rows, :] = _pack_halves(y_lo, y_hi)
        if rows.stop < EXPERT_TILE:
            o_ref[rows.stop:, :] = jnp.zeros((EXPERT_TILE - rows.stop, o_ref.shape[1]), o_ref.dtype)

    _run_tile_rows(tabs, body)


def _expert_down(act, w_down, b_down, counts, tiles_per_expert, tile_base, tn=512):
    n_slots, F = act.shape
    E, _, D = w_down.shape
    half = D // 2
    tn = min(tn, half)
    nj = half // tn
    n_tiles = n_slots // EXPERT_TILE
    tables = _expert_steps(counts, tiles_per_expert, tile_base, nj, n_tiles)

    grid_spec = pltpu.PrefetchScalarGridSpec(
        num_scalar_prefetch=N_STEP_TABLES,
        grid=(n_tiles * nj,),
        in_specs=[
            pl.BlockSpec((EXPERT_TILE, F), lambda s, e, j, t, *_: (t[s], 0)),
            pl.BlockSpec(memory_space=pl.ANY),
            pl.BlockSpec((1, D // tn, tn), lambda s, e, *_: (e[s], 0, 0)),
        ],
        out_specs=pl.BlockSpec((EXPERT_TILE, tn), lambda s, e, j, t, *_: (t[s], j[s])),
        scratch_shapes=[pltpu.VMEM((2, 2, F, tn), jnp.float32), pltpu.SemaphoreType.DMA((2,))],
    )
    return pl.pallas_call(
        functools.partial(_down_kernel, tn=tn, hi_offset=half),
        grid_spec=grid_spec,
        out_shape=jax.ShapeDtypeStruct((n_slots, half), jnp.uint32),
        compiler_params=_cparams("arbitrary"),
        name="expert_down",
    )(*tables, act, w_down, b_down.reshape(E, D // tn, tn))


def _combine_kernel(pos_ref, pos_next_ref, x_ref, gate_ref, ys_ref, o_ref, buf, sems, *, tt):
    i = pl.program_id(0)
    slot = i % 2

    def row_copy(src_row, s, k, tok):
        return pltpu.make_async_copy(ys_ref.at[pl.ds(src_row, 1)], buf.at[s, k, pl.ds(tok, 1)], sems.at[s])

    def fetch(p_ref, s):
        def issue(tok, carry):
            for k in range(TOP_K):
                row_copy(p_ref[0, 0, tok * TOP_K + k], s, k, tok).start(priority=k % 2)
            return carry
        lax.fori_loop(0, tt, issue, 0, unroll=4)

    pl.when(i == 0)(lambda: fetch(pos_ref, 0))
    pl.when(i + 1 < pl.num_programs(0))(lambda: fetch(pos_next_ref, 1 - slot))

    def drain(tok, carry):
        for k in range(TOP_K):
            row_copy(0, slot, k, 0).wait()
        return carry

    lax.fori_loop(0, tt, drain, 0, unroll=4)

    half = buf.shape[3]
    acc_lo = x_ref[:, :half]
    acc_hi = x_ref[:, half:]
    for k in range(TOP_K):
        lo, hi = _unpack_halves(buf[slot, k])
        gk = gate_ref[:, k:k + 1]
        acc_lo = acc_lo + gk * lo
        acc_hi = acc_hi + gk * hi
    o_ref[:, :half] = acc_lo
    o_ref[:, half:] = acc_hi


def _combine(x1, gates, pos, ys, row0, n_rows, tt=COMBINE_TILE):
    T, D = x1.shape
    half = D // 2
    b0 = row0 // tt
    n_steps = n_rows // tt
    pos_blocks = pos.reshape(T // tt, 1, tt * TOP_K)
    grid_spec = pltpu.PrefetchScalarGridSpec(
        num_scalar_prefetch=0,
        grid=(n_steps,),
        in_specs=[
            pl.BlockSpec((1, 1, tt * TOP_K), lambda i: (b0 + i, 0, 0), memory_space=pltpu.SMEM),
            pl.BlockSpec((1, 1, tt * TOP_K), lambda i: (b0 + jnp.minimum(i + 1, n_steps - 1), 0, 0),
                         memory_space=pltpu.SMEM),
            pl.BlockSpec((tt, D), lambda i: (b0 + i, 0)),
            pl.BlockSpec((tt, TOP_K), lambda i: (b0 + i, 0)),
            pl.BlockSpec(memory_space=pl.ANY),
        ],
        out_specs=pl.BlockSpec((tt, D), lambda i: (i, 0)),
        scratch_shapes=[pltpu.VMEM((2, TOP_K, tt, half), jnp.uint32), pltpu.SemaphoreType.DMA((2,))],
    )
    return pl.pallas_call(
        functools.partial(_combine_kernel, tt=tt),
        grid_spec=grid_spec,
        out_shape=jax.ShapeDtypeStruct((n_rows, D), jnp.float32),
        compiler_params=_cparams("arbitrary"),
        name="moe_combine",
    )(pos_blocks, pos_blocks, x1, gates, ys)


def _sequence_tables(seq_lens, tile):
    lo, hi, start = [], [], 0
    for n in seq_lens:
        assert n % tile == 0
        lo += [start] * (n // tile)
        hi += [start + n] * (n // tile)
        start += n
    return jnp.asarray(np.array(lo, np.int32)), jnp.asarray(np.array(hi, np.int32))


def _rope_tables(seq_lens):
    half = HEAD_DIM // 2
    inv_freq = ROPE_THETA ** (-jnp.arange(half, dtype=jnp.float32) / half)
    pos = jnp.concatenate([jnp.arange(n, dtype=jnp.float32) for n in seq_lens])
    ang = pos[:, None] * inv_freq[None, :]
    cos, sin = jnp.cos(ang), jnp.sin(ang)
    return jnp.concatenate([cos, cos], axis=1), jnp.concatenate([-sin, sin], axis=1)


def _encoder_layer(xa, xb, seq_lens, norm_mix, w_in, q_norm, k_norm, w_pool, pool_scale, w_branch_a,
                   w_branch_b, w_out, norm_ffn, w_router, b_router, w_gate_up, b_gate_up, w_down, b_down):
    Ta, D = xa.shape
    Tb = xb.shape[0]
    T = Ta + Tb
    bf16 = jnp.bfloat16
    A_out = w_branch_a.shape[0]
    H = A_out // HEAD_DIM
    A = N_GROUPS * A_out
    P = w_branch_b.shape[0]
    E = w_router.shape[1]

    w_in_b = w_in.astype(bf16)
    gain_cols = jnp.concatenate([jnp.tile(q_norm, (1, H)).reshape(1, A),
                                 jnp.tile(k_norm, (1, H)).reshape(1, A)], axis=1)
    cos_t, sin_t = _rope_tables(seq_lens)

    h = _rmsnorm(xa, xb, norm_mix)
    qk = _qk_proj(h, w_in_b, gain_cols, cos_t, sin_t, 2 * A)
    v = _v_proj(h, w_in_b, 2 * A, A)
    u = _u_proj(h, w_in_b, 3 * A, P)

    lo_a, hi_a = _sequence_tables(seq_lens, ATTN_CHUNK)
    o_attn = _attention(qk, v, lo_a, hi_a, H)
    tp = 512
    lo_p, hi_p = _sequence_tables(seq_lens, tp)
    o_pool = _pool(u, w_pool.astype(bf16), pool_scale, lo_p, hi_p, tp)

    mix = _mix(h, w_in_b, 3 * A + P, o_attn, w_branch_a.astype(bf16), o_pool, w_branch_b.astype(bf16))
    x1 = _out_proj(mix, w_out.astype(bf16), xa, xb)

    hp, top_idx, gates = _router(x1, norm_ffn, w_router.astype(bf16), b_router)
    n_assign = T * TOP_K
    n_tiles = n_assign // EXPERT_TILE + E
    pos, pad_lo, pad_hi, counts, tiles_per_expert, tile_base = _routing_tables(top_idx, E)
    xs = _dispatch(hp, pos, pad_lo, pad_hi, n_tiles * EXPERT_TILE)
    act = _expert_gate_up(xs, w_gate_up, b_gate_up, counts, tiles_per_expert, tile_base)
    ys = _expert_down(act, w_down, b_down, counts, tiles_per_expert, tile_base)
    return _combine(x1, gates, pos, ys, 0, Ta), _combine(x1, gates, pos, ys, Ta, Tb)


def kernel(x_prompt, x_sample, norm_mix, w_in, q_norm, k_norm, w_pool, pool_scale, w_branch_a, w_branch_b,
           w_out, norm_ffn, w_router, b_router, w_gate_up, b_gate_up, w_down, b_down):
    depth = norm_mix.shape[0]
    D = x_prompt.shape[-1]
    seq_lens = (x_prompt.shape[1],) * x_prompt.shape[0] + (x_sample.shape[1],) * x_sample.shape[0]
    xa, xb = x_prompt.reshape(-1, D), x_sample.reshape(-1, D)
    for layer in range(depth):
        xa, xb = _encoder_layer(xa, xb, seq_lens, norm_mix[layer], w_in[layer], q_norm[layer], k_norm[layer],
                                w_pool[layer], pool_scale[layer], w_branch_a[layer], w_branch_b[layer],
                                w_out[layer], norm_ffn[layer], w_router[layer], b_router[layer],
                                w_gate_up[layer], b_gate_up[layer], w_down[layer], b_down[layer])
    return xa.reshape(x_prompt.shape), xb.reshape(x_sample.shape)
```

```python
import functools

import numpy as np
import jax
import jax.numpy as jnp
from jax import lax
from jax.experimental import pallas as pl
from jax.experimental.pallas import tpu as pltpu

HEAD_DIM = 128
DILATION_PAIRS = ((128, 1), (512, 4), (2048, 16))
N_GROUPS = len(DILATION_PAIRS)
ATTN_SIDE = 64
ROPE_THETA = 10000.0
POOL_WINDOWS = (2, 4, 8, 16)
POOL_HALO = 8
TOP_K = 4
SWIGLU_LIMIT = 7.0
SWIGLU_ALPHA = 1.702
RMS_EPS = 1e-6

VMEM_LIMIT_BYTES = 62 * 1024 * 1024
PROJ_ROWS = 1024
PROJ_COL_TILES = (1024, 512)
PROJ_ROW_SPLIT = 512
QK_ROW_SPLIT = 128

ATTN_CHUNK = 2048
ATTN_QBLK = 128
ATTN_WIN = 256
EXPERT_TILE = 1024
EXPERT_ROW_BLOCK = 256
WEIGHT_CAST_ROWS = 512
DISPATCH_TILE = 1024
COMBINE_TILE = 256


def _cparams(*sem):
    return pltpu.CompilerParams(dimension_semantics=sem, vmem_limit_bytes=VMEM_LIMIT_BYTES)


def _pack_halves(lo, hi):
    lo_bits = lax.bitcast_convert_type(lo.astype(jnp.bfloat16).astype(jnp.float32), jnp.uint32)
    hi_bits = lax.bitcast_convert_type(hi.astype(jnp.bfloat16).astype(jnp.float32), jnp.uint32)
    return (lo_bits >> 16) | (hi_bits & jnp.uint32(0xFFFF0000))


def _unpack_halves(packed):
    lo = lax.bitcast_convert_type(packed << 16, jnp.float32)
    hi = lax.bitcast_convert_type(packed & jnp.uint32(0xFFFF0000), jnp.float32)
    return lo, hi


def _rmsnorm_kernel(xa_ref, xb_ref, g_ref, o_ref, *, n_first):
    def body(x_ref):
        x = x_ref[...]
        ms = jnp.mean(x * x, axis=-1, keepdims=True)
        o_ref[...] = (x * lax.rsqrt(ms + RMS_EPS) * g_ref[...]).astype(o_ref.dtype)

    pl.when(pl.program_id(0) < n_first)(lambda: body(xa_ref))
    pl.when(pl.program_id(0) >= n_first)(lambda: body(xb_ref))


def _rmsnorm(xa, xb, gain, tm=512):
    D = xa.shape[1]
    T = xa.shape[0] + xb.shape[0]
    n_first = xa.shape[0] // tm
    return pl.pallas_call(
        functools.partial(_rmsnorm_kernel, n_first=n_first),
        grid=(T // tm,),
        in_specs=[pl.BlockSpec((tm, D), lambda i: (jnp.minimum(i, n_first - 1), 0)),
                  pl.BlockSpec((tm, D), lambda i: (jnp.maximum(i - n_first, 0), 0)),
                  pl.BlockSpec((1, D), lambda i: (0, 0))],
        out_specs=pl.BlockSpec((tm, D), lambda i: (i, 0)),
        out_shape=jax.ShapeDtypeStruct((T, D), jnp.bfloat16),
        compiler_params=_cparams("arbitrary"),
        name="rmsnorm",
    )(xa, xb, gain.reshape(1, D))


def _qk_proj_kernel(h_ref, w_ref, gain_ref, cos_ref, sin_ref, o_ref):
    w = w_ref[...]
    tm = h_ref.shape[0]
    for r0 in range(0, tm, QK_ROW_SPLIT):
        rows = slice(r0, r0 + QK_ROW_SPLIT)
        acc = jnp.dot(h_ref[rows, :], w, preferred_element_type=jnp.float32)
        cos = cos_ref[rows, :]
        sin = sin_ref[rows, :]
        for c in range(acc.shape[1] // HEAD_DIM):
            t = acc[:, c * HEAD_DIM:(c + 1) * HEAD_DIM]
            ms = jnp.mean(t * t, axis=-1, keepdims=True)
            t = t * lax.rsqrt(ms + RMS_EPS) * gain_ref[:, c * HEAD_DIM:(c + 1) * HEAD_DIM]
            o_ref[c, rows, :] = t * cos + pltpu.roll(t, HEAD_DIM // 2, axis=1) * sin


def _col_tile(ncols, col0=0):
    return next(t for t in PROJ_COL_TILES if ncols % t == 0 and col0 % t == 0)


def _qk_proj(h, w_in, gain_cols, cos_t, sin_t, ncols, tm=PROJ_ROWS):
    T, D = h.shape
    tn = _col_tile(ncols)
    n_heads = ncols // HEAD_DIM
    return pl.pallas_call(
        _qk_proj_kernel,
        grid=(T // tm, ncols // tn),
        in_specs=[
            pl.BlockSpec((tm, D), lambda i, j: (i, 0)),
            pl.BlockSpec((D, tn), lambda i, j: (0, j)),
            pl.BlockSpec((1, tn), lambda i, j: (0, j)),
            pl.BlockSpec((tm, HEAD_DIM), lambda i, j: (i, 0)),
            pl.BlockSpec((tm, HEAD_DIM), lambda i, j: (i, 0)),
        ],
        out_specs=pl.BlockSpec((tn // HEAD_DIM, tm, HEAD_DIM), lambda i, j: (j, i, 0)),
        out_shape=jax.ShapeDtypeStruct((n_heads, T, HEAD_DIM), jnp.float32),
        compiler_params=_cparams("arbitrary", "arbitrary"),
        name="qk_proj",
    )(h, w_in, gain_cols, cos_t, sin_t)


def _v_proj_kernel(h_ref, w_ref, o_ref):
    w = w_ref[...]
    for r0 in range(0, h_ref.shape[0], PROJ_ROW_SPLIT):
        rows = slice(r0, r0 + PROJ_ROW_SPLIT)
        acc = jnp.dot(h_ref[rows, :], w, preferred_element_type=jnp.float32)
        for c in range(acc.shape[1] // HEAD_DIM):
            o_ref[c, rows, :] = acc[:, c * HEAD_DIM:(c + 1) * HEAD_DIM]


def _v_proj(h, w_in, col0, ncols, tm=PROJ_ROWS):
    T, D = h.shape
    tn = _col_tile(ncols, col0)
    cb0 = col0 // tn
    return pl.pallas_call(
        _v_proj_kernel,
        grid=(T // tm, ncols // tn),
        in_specs=[
            pl.BlockSpec((tm, D), lambda i, j: (i, 0)),
            pl.BlockSpec((D, tn), lambda i, j: (0, cb0 + j)),
        ],
        out_specs=pl.BlockSpec((tn // HEAD_DIM, tm, HEAD_DIM), lambda i, j: (j, i, 0)),
        out_shape=jax.ShapeDtypeStruct((ncols // HEAD_DIM, T, HEAD_DIM), jnp.float32),
        compiler_params=_cparams("arbitrary", "arbitrary"),
        name="v_proj",
    )(h, w_in)


def _u_proj_kernel(h_ref, w_ref, o_ref):
    w = w_ref[...]
    for r0 in range(0, h_ref.shape[0], PROJ_ROW_SPLIT):
        rows = slice(r0, r0 + PROJ_ROW_SPLIT)
        o_ref[rows, :] = jnp.dot(h_ref[rows, :], w, preferred_element_type=jnp.float32)


def _u_proj(h, w_in, col0, ncols, tm=PROJ_ROWS):
    T, D = h.shape
    tn = _col_tile(ncols, col0)
    cb0 = col0 // tn
    return pl.pallas_call(
        _u_proj_kernel,
        grid=(T // tm, ncols // tn),
        in_specs=[
            pl.BlockSpec((tm, D), lambda i, j: (i, 0)),
            pl.BlockSpec((D, tn), lambda i, j: (0, cb0 + j)),
        ],
        out_specs=pl.BlockSpec((tm, tn), lambda i, j: (i, j)),
        out_shape=jax.ShapeDtypeStruct((T, ncols), jnp.float32),
        compiler_params=_cparams("arbitrary", "arbitrary"),
        name="u_proj",
    )(h, w_in)


def _attn_kernel(lo_ref, hi_ref, *refs, chunk):
    in_refs = refs[:7 * N_GROUPS]
    o_ref = refs[7 * N_GROUPS]
    o_sc, lse_sc = refs[7 * N_GROUPS + 1:]
    c = pl.program_id(0)
    t0 = c * chunk
    seq_lo = lo_ref[c]
    seq_hi = hi_ref[c]
    win = ATTN_WIN
    slack = win - ATTN_QBLK - 2 * ATTN_SIDE
    qq = lax.broadcasted_iota(jnp.int32, (ATTN_QBLK, win), 0)
    kk = lax.broadcasted_iota(jnp.int32, (ATTN_QBLK, win), 1)
    band_bias = jnp.where(jnp.abs(kk - ATTN_SIDE - qq) <= ATTN_SIDE, 0.0, -jnp.inf)
    k_row = lax.broadcasted_iota(jnp.int32, (1, win), 1)
    scale = HEAD_DIM ** -0.5

    for g, (_, d) in enumerate(DILATION_PAIRS):
        q_ref, kc_ref, kl_ref, kr_ref, vc_ref, vl_ref, vr_ref = in_refs[7 * g:7 * g + 7]
        shift = d.bit_length() - 1
        rows = chunk // d
        m0 = t0 >> shift
        m_lo = seq_lo >> shift
        m_hi = seq_hi >> shift
        for r in range(d):
            def sub(ref, n, r=r, d=d):
                return ref[0, pl.ds(r, n, stride=d), :]
            q_r = sub(q_ref, rows).astype(jnp.bfloat16)
            def padded(l_ref, c_ref, r_ref):
                parts = [sub(l_ref, ATTN_SIDE), sub(c_ref, rows), sub(r_ref, ATTN_SIDE)]
                if slack:
                    parts.append(sub(r_ref, slack))
                return jnp.concatenate(parts, axis=0).astype(jnp.bfloat16)
            k_r = padded(kl_ref, kc_ref, kr_ref)
            v_r = padded(vl_ref, vc_ref, vr_ref)
            for b in range(rows // ATTN_QBLK):
                qs = q_r[b * ATTN_QBLK:(b + 1) * ATTN_QBLK]
                ks = k_r[b * ATTN_QBLK:b * ATTN_QBLK + win]
                vs = v_r[b * ATTN_QBLK:b * ATTN_QBLK + win]
                key_m = k_row + (m0 + (b * ATTN_QBLK - ATTN_SIDE))
                ends_bias = jnp.where((key_m >= m_lo) & (key_m < m_hi), 0.0, -jnp.inf)
                s = lax.dot_general(qs, ks, (((1,), (1,)), ((), ())),
                                    preferred_element_type=jnp.float32) * scale + band_bias + ends_bias
                m = jnp.max(s, axis=-1, keepdims=True)
                p = jnp.exp(s - m)
                denom = jnp.sum(p, axis=-1, keepdims=True)
                o = jnp.dot(p.astype(jnp.bfloat16), vs, preferred_element_type=jnp.float32) / denom
                lse = m + jnp.log(denom)
                dst = pl.ds(b * ATTN_QBLK * d + r, ATTN_QBLK, stride=d)
                o_sc[g, dst, :] = o
                lse_sc[g, dst, :] = jnp.broadcast_to(lse, (ATTN_QBLK, HEAD_DIM))

    lses = [lse_sc[g] for g in range(N_GROUPS)]
    top = functools.reduce(jnp.maximum, lses)
    ws = [jnp.exp(l - top) for l in lses]
    num = sum(w * o_sc[g] for g, w in enumerate(ws))
    o_ref[...] = (num / sum(ws)).astype(o_ref.dtype)


def _attention(qk, v, seq_lo, seq_hi, heads_per_group):
    n_qk, T, _ = qk.shape
    H = heads_per_group
    k_base = n_qk // 2
    chunk = ATTN_CHUNK
    n_chunks = T // chunk
    in_specs = []
    operands = []
    for g, (_, d) in enumerate(DILATION_PAIRS):
        halo = ATTN_SIDE * d
        per = chunk // halo
        n_halo = T // halo

        def center(base, g=g):
            return pl.BlockSpec((1, chunk, HEAD_DIM), lambda c, h, lo, hi: (base + g * H + h, c, 0))

        def left(base, g=g, per=per):
            return pl.BlockSpec((1, halo, HEAD_DIM),
                                lambda c, h, lo, hi: (base + g * H + h, jnp.maximum(c * per - 1, 0), 0))

        def right(base, g=g, per=per, n_halo=n_halo):
            return pl.BlockSpec((1, halo, HEAD_DIM),
                                lambda c, h, lo, hi: (base + g * H + h, jnp.minimum((c + 1) * per, n_halo - 1), 0))

        in_specs += [center(0), center(k_base), left(k_base), right(k_base), center(0), left(0), right(0)]
        operands += [qk, qk, qk, qk, v, v, v]
    grid_spec = pltpu.PrefetchScalarGridSpec(
        num_scalar_prefetch=2,
        grid=(n_chunks, H),
        in_specs=in_specs,
        out_specs=pl.BlockSpec((chunk, HEAD_DIM), lambda c, h, lo, hi: (c, h)),
        scratch_shapes=[pltpu.VMEM((N_GROUPS, chunk, HEAD_DIM), jnp.float32),
                        pltpu.VMEM((N_GROUPS, chunk, HEAD_DIM), jnp.float32)],
    )
    return pl.pallas_call(
        functools.partial(_attn_kernel, chunk=chunk),
        grid_spec=grid_spec,
        out_shape=jax.ShapeDtypeStruct((T, H * HEAD_DIM), jnp.bfloat16),
        compiler_params=_cparams("arbitrary", "arbitrary"),
        name="dilated_attention",
    )(seq_lo, seq_hi, *operands)


def _pool_kernel(lo_ref, hi_ref, uc_ref, ul_ref, ur_ref, w_ref, scale_ref, o_ref, ext_sc, *, tp):
    i = pl.program_id(0)
    t0 = i * tp
    seq_lo = lo_ref[i]
    seq_hi = hi_ref[i]
    gw = w_ref.shape[1]
    pos_l = t0 - POOL_HALO + lax.broadcasted_iota(jnp.int32, (POOL_HALO, 1), 0)
    pos_r = t0 + tp + lax.broadcasted_iota(jnp.int32, (POOL_HALO, 1), 0)
    ext_sc[0:POOL_HALO, :] = jnp.where(pos_l >= seq_lo, ul_ref[...], 0.0)
    ext_sc[POOL_HALO:POOL_HALO + tp, :] = uc_ref[...]
    ext_sc[POOL_HALO + tp:, :] = jnp.where(pos_r < seq_hi, ur_ref[...], 0.0)
    pos = t0 + lax.broadcasted_iota(jnp.int32, (tp, 1), 0)
    for gi, w in enumerate(POOL_WINDOWS):
        cols = slice(gi * gw, (gi + 1) * gw)
        total = None
        for j in range(-(w // 2), w - w // 2):
            piece = ext_sc[POOL_HALO + j:POOL_HALO + j + tp, cols]
            total = piece if total is None else total + piece
        cnt = jnp.minimum(pos + (w - w // 2), seq_hi) - jnp.maximum(pos - w // 2, seq_lo)
        pooled = total / cnt.astype(jnp.float32) - uc_ref[:, cols]
        mixed = jnp.dot(pooled.astype(jnp.bfloat16), w_ref[gi], preferred_element_type=jnp.float32)
        o_ref[:, cols] = (mixed * scale_ref[:, cols]).astype(o_ref.dtype)


def _pool(u, w_pool, pool_scale, seq_lo, seq_hi, tp):
    T, P = u.shape
    per = tp // POOL_HALO
    n_halo = T // POOL_HALO
    grid_spec = pltpu.PrefetchScalarGridSpec(
        num_scalar_prefetch=2,
        grid=(T // tp,),
        in_specs=[
            pl.BlockSpec((tp, P), lambda i, lo, hi: (i, 0)),
            pl.BlockSpec((POOL_HALO, P), lambda i, lo, hi: (jnp.maximum(i * per - 1, 0), 0)),
            pl.BlockSpec((POOL_HALO, P), lambda i, lo, hi: (jnp.minimum((i + 1) * per, n_halo - 1), 0)),
            pl.BlockSpec(w_pool.shape, lambda i, lo, hi: (0, 0, 0)),
            pl.BlockSpec((1, P), lambda i, lo, hi: (0, 0)),
        ],
        out_specs=pl.BlockSpec((tp, P), lambda i, lo, hi: (i, 0)),
        scratch_shapes=[pltpu.VMEM((tp + 2 * POOL_HALO, P), jnp.float32)],
    )
    return pl.pallas_call(
        functools.partial(_pool_kernel, tp=tp),
        grid_spec=grid_spec,
        out_shape=jax.ShapeDtypeStruct((T, P), jnp.bfloat16),
        compiler_params=_cparams("arbitrary"),
        name="multiscale_pool",
    )(seq_lo, seq_hi, u, u, u, w_pool, pool_scale.reshape(1, P))


def _mix_kernel(h_ref, wga_ref, wgb_ref, oa_ref, wa_ref, ob_ref, wb_ref, o_ref):
    for r0 in range(0, h_ref.shape[0], PROJ_ROW_SPLIT):
        rows = slice(r0, r0 + PROJ_ROW_SPLIT)
        h = h_ref[rows, :]
        g_a = jnp.dot(h, wga_ref[...], preferred_element_type=jnp.float32)
        g_b = jnp.dot(h, wgb_ref[...], preferred_element_type=jnp.float32)
        a = jnp.dot(oa_ref[rows, :], wa_ref[...], preferred_element_type=jnp.float32)
        b = jnp.dot(ob_ref[rows, :], wb_ref[...], preferred_element_type=jnp.float32)
        o_ref[rows, :] = (jax.nn.sigmoid(g_a) * a + jax.nn.sigmoid(g_b) * b).astype(o_ref.dtype)


def _mix(h, w_in, ga_col0, o_attn, w_a, o_pool, w_b, tm=PROJ_ROWS, tn=512):
    T, D = h.shape
    cb_a = ga_col0 // tn
    cb_b = (ga_col0 + D) // tn
    A = o_attn.shape[1]
    P = o_pool.shape[1]
    return pl.pallas_call(
        _mix_kernel,
        grid=(T // tm, D // tn),
        in_specs=[
            pl.BlockSpec((tm, D), lambda i, j: (i, 0)),
            pl.BlockSpec((D, tn), lambda i, j: (0, cb_a + j)),
            pl.BlockSpec((D, tn), lambda i, j: (0, cb_b + j)),
            pl.BlockSpec((tm, A), lambda i, j: (i, 0)),
            pl.BlockSpec((A, tn), lambda i, j: (0, j)),
            pl.BlockSpec((tm, P), lambda i, j: (i, 0)),
            pl.BlockSpec((P, tn), lambda i, j: (0, j)),
        ],
        out_specs=pl.BlockSpec((tm, tn), lambda i, j: (i, j)),
        out_shape=jax.ShapeDtypeStruct((T, D), jnp.bfloat16),
        compiler_params=_cparams("arbitrary", "arbitrary"),
        name="gated_mix",
    )(h, w_in, w_in, o_attn, w_a, o_pool, w_b)


def _out_proj_kernel(m_ref, w_ref, xa_ref, xb_ref, o_ref, *, n_first):
    i = pl.program_id(0)

    def body(x_ref):
        w = w_ref[...]
        half = m_ref.shape[0] // 2
        for rows in (slice(0, half), slice(half, 2 * half)):
            o_ref[rows, :] = x_ref[rows, :] + jnp.dot(m_ref[rows, :], w, preferred_element_type=jnp.float32)

    pl.when(i < n_first)(lambda: body(xa_ref))
    pl.when(i >= n_first)(lambda: body(xb_ref))


def _out_proj(mix, w_out, xa, xb, tm=PROJ_ROWS, tn=512):
    T, D = mix.shape
    tn = min(tn, D)
    nj = D // tn
    n_first = xa.shape[0] // tm
    return pl.pallas_call(
        functools.partial(_out_proj_kernel, n_first=n_first),
        grid=(T // tm, nj),
        in_specs=[
            pl.BlockSpec((tm, D), lambda i, j: (i, 0)),
            pl.BlockSpec((D, tn), lambda i, j: (0, j)),
            pl.BlockSpec((tm, tn), lambda i, j: (jnp.minimum(i, n_first - 1), jnp.where(i < n_first, j, nj - 1))),
            pl.BlockSpec((tm, tn), lambda i, j: (jnp.maximum(i - n_first, 0), jnp.where(i >= n_first, j, 0))),
        ],
        out_specs=pl.BlockSpec((tm, tn), lambda i, j: (i, j)),
        out_shape=jax.ShapeDtypeStruct((T, D), jnp.float32),
        compiler_params=_cparams("arbitrary", "arbitrary"),
        name="out_proj_residual",
    )(mix, w_out, xa, xb)


def _router_kernel(x_ref, g_ref, wr_ref, br_ref, hp_ref, idx_ref, gate_ref):
    x = x_ref[...]
    ms = jnp.mean(x * x, axis=-1, keepdims=True)
    h = x * lax.rsqrt(ms + RMS_EPS) * g_ref[...]
    half = h.shape[1] // 2
    hp_ref[...] = _pack_halves(h[:, :half], h[:, half:])
    logits = jnp.dot(h.astype(jnp.bfloat16), wr_ref[...], preferred_element_type=jnp.float32) + br_ref[...]
    n_exp = logits.shape[1]
    lane = lax.broadcasted_iota(jnp.int32, logits.shape, 1)
    vals, idxs = [], []
    for _ in range(TOP_K):
        m = jnp.max(logits, axis=-1, keepdims=True)
        idx = jnp.min(jnp.where(logits == m, lane, n_exp), axis=-1, keepdims=True)
        vals.append(m)
        idxs.append(idx)
        logits = jnp.where(lane == idx, -jnp.inf, logits)
    exps = [jnp.exp(v - vals[0]) for v in vals]
    total = sum(exps)
    for k in range(TOP_K):
        idx_ref[:, k:k + 1] = idxs[k]
        gate_ref[:, k:k + 1] = exps[k] / total


def _router(x1, gain, w_router, b_router, tr=512):
    T, D = x1.shape
    E = w_router.shape[1]
    return pl.pallas_call(
        _router_kernel,
        grid=(T // tr,),
        in_specs=[
            pl.BlockSpec((tr, D), lambda i: (i, 0)),
            pl.BlockSpec((1, D), lambda i: (0, 0)),
            pl.BlockSpec((D, E), lambda i: (0, 0)),
            pl.BlockSpec((1, E), lambda i: (0, 0)),
        ],
        out_specs=[
            pl.BlockSpec((tr, D // 2), lambda i: (i, 0)),
            pl.BlockSpec((tr, TOP_K), lambda i: (i, 0)),
            pl.BlockSpec((tr, TOP_K), lambda i: (i, 0)),
        ],
        out_shape=[
            jax.ShapeDtypeStruct((T, D // 2), jnp.uint32),
            jax.ShapeDtypeStruct((T, TOP_K), jnp.int32),
            jax.ShapeDtypeStruct((T, TOP_K), jnp.float32),
        ],
        compiler_params=_cparams("arbitrary"),
        name="router_topk",
    )(x1, gain.reshape(1, D), w_router, b_router.reshape(1, E))


def _routing_tables(top_idx, n_experts):
    e = top_idx.reshape(-1)
    onehot = (e[:, None] == jnp.arange(n_experts, dtype=jnp.int32)[None, :]).astype(jnp.int32)
    csum = jnp.cumsum(onehot, axis=0)
    counts = csum[-1]
    tiles_per_expert = (counts + EXPERT_TILE - 1) // EXPERT_TILE
    padded = tiles_per_expert * EXPERT_TILE
    pend = jnp.cumsum(padded)
    pstart = pend - padded
    pos = jnp.sum(onehot * (csum - 1 + pstart[None, :]), axis=1).astype(jnp.int32)
    pad_lo = (pstart + counts).astype(jnp.int32)
    pad_hi = pstart + (counts + EXPERT_ROW_BLOCK - 1) // EXPERT_ROW_BLOCK * EXPERT_ROW_BLOCK
    return (pos, pad_lo, pad_hi.astype(jnp.int32), counts.astype(jnp.int32), tiles_per_expert.astype(jnp.int32),
            (pstart // EXPERT_TILE).astype(jnp.int32))


def _dispatch_kernel(pos_ref, pad_lo_ref, pad_hi_ref, hp_ref, xs_ref, zero_sc, sem, *, n_experts):
    step = pl.program_id(0)
    tt = hp_ref.shape[0]

    def row_copy(tok, dst_row):
        return pltpu.make_async_copy(hp_ref.at[pl.ds(tok, 1)], xs_ref.at[pl.ds(dst_row, 1)], sem)

    def issue(tok, carry):
        for k in range(TOP_K):
            row_copy(tok, pos_ref[0, 0, tok * TOP_K + k]).start(priority=k % 2)
        return carry

    lax.fori_loop(0, tt, issue, 0, unroll=4)

    def drain(tok, carry):
        for k in range(TOP_K):
            row_copy(0, 0).wait()
        return carry

    lax.fori_loop(0, tt, drain, 0, unroll=4)

    @pl.when(step == pl.num_programs(0) - 1)
    def _():
        zero_sc[...] = jnp.zeros_like(zero_sc)

        def pad_copy(dst_row):
            return pltpu.make_async_copy(zero_sc.at[pl.ds(0, 1)], xs_ref.at[pl.ds(dst_row, 1)], sem)

        for e in range(n_experts):
            lo = pad_lo_ref[e]
            hi = pad_hi_ref[e]

            def fill(p, carry):
                pad_copy(p).start()
                return carry

            lax.fori_loop(lo, hi, fill, 0)

            def fill_wait(p, carry):
                pad_copy(0).wait()
                return carry

            lax.fori_loop(lo, hi, fill_wait, 0)


def _dispatch(hp, pos, pad_lo, pad_hi, n_slots, tt=DISPATCH_TILE):
    T, W = hp.shape
    n_experts = pad_lo.shape[0]
    grid_spec = pltpu.PrefetchScalarGridSpec(
        num_scalar_prefetch=0,
        grid=(T // tt,),
        in_specs=[
            pl.BlockSpec((1, 1, tt * TOP_K), lambda s: (s, 0, 0), memory_space=pltpu.SMEM),
            pl.BlockSpec(memory_space=pltpu.SMEM),
            pl.BlockSpec(memory_space=pltpu.SMEM),
            pl.BlockSpec((tt, W), lambda s: (s, 0)),
        ],
        out_specs=pl.BlockSpec(memory_space=pl.ANY),
        scratch_shapes=[pltpu.VMEM((8, W), jnp.uint32), pltpu.SemaphoreType.DMA(())],
    )
    return pl.pallas_call(
        functools.partial(_dispatch_kernel, n_experts=n_experts),
        grid_spec=grid_spec,
        out_shape=jax.ShapeDtypeStruct((n_slots, W), jnp.uint32),
        compiler_params=_cparams("arbitrary"),
        name="moe_dispatch",
    )(pos.reshape(T // tt, 1, tt * TOP_K), pad_lo, pad_hi, hp)


def _expert_steps(counts, tiles_per_expert, tile_base, nj, n_tiles):
    n_steps = n_tiles * nj
    steps_e = tiles_per_expert * nj
    send = jnp.cumsum(steps_e)
    total = send[-1]
    step = jnp.arange(n_steps, dtype=jnp.int32)
    s = jnp.minimum(step, total - 1)
    e = jnp.sum((send[None, :] <= s[:, None]).astype(jnp.int32), axis=1)
    local = s - (send - steps_e)[e]
    nt = jnp.maximum(tiles_per_expert[e], 1)
    j = local // nt
    t_local = local - j * nt
    tile = tile_base[e] + t_local
    used = step < total
    first = used & (t_local == 0)
    slot = (jnp.cumsum(first.astype(jnp.int32)) - 1) % 2
    s_next = s - t_local + nt
    has_next = first & (s_next < total)
    s_next = jnp.minimum(s_next, total - 1)
    real_rows = jnp.clip(counts[e] - t_local * EXPERT_TILE, 1, EXPERT_TILE)
    row_blocks = (real_rows + EXPERT_ROW_BLOCK - 1) // EXPERT_ROW_BLOCK
    i32 = lambda a: a.astype(jnp.int32)
    return (i32(e), i32(j), i32(tile), i32(used), i32(first), i32(slot), i32(e[s_next]), i32(j[s_next]),
            i32(has_next), i32(row_blocks))


def _dot_f32_weight(x_parts, w_ref, part_rows):
    acc = None
    for p, x in enumerate(x_parts):
        for k0 in range(0, part_rows, WEIGHT_CAST_ROWS):
            w = w_ref[p * part_rows + k0:p * part_rows + k0 + WEIGHT_CAST_ROWS, :].astype(jnp.bfloat16)
            d = jnp.dot(x[:, k0:k0 + WEIGHT_CAST_ROWS], w, preferred_element_type=jnp.float32)
            acc = d if acc is None else acc + d
    return acc


def _weight_prefetch(tabs, w_hbm, wbuf, sems, col_offsets, tn):
    e_ref, j_ref, _, _, first_ref, slot_ref, ne_ref, nj_ref, has_next_ref, _ = tabs
    s = pl.program_id(0)
    slot = slot_ref[s]

    def copies(e, j, sl):
        return [pltpu.make_async_copy(w_hbm.at[e, :, pl.ds(pl.multiple_of(off + j * tn, tn), tn)],
                                      wbuf.at[sl, m], sems.at[sl])
                for m, off in enumerate(col_offsets)]

    @pl.when(s == 0)
    def _():
        for c in copies(e_ref[0], j_ref[0], 0):
            c.start(priority=1)

    @pl.when(first_ref[s] == 1)
    def _():
        for c in copies(e_ref[s], j_ref[s], slot):
            c.wait()

        @pl.when(has_next_ref[s] == 1)
        def _():
            for c in copies(ne_ref[s], nj_ref[s], 1 - slot):
                c.start(priority=1)

    return slot


N_STEP_TABLES = 10


def _run_tile_rows(tabs, body):
    s = pl.program_id(0)
    used = tabs[3][s] == 1
    n_blocks = tabs[9][s]
    for nb in range(1, EXPERT_TILE // EXPERT_ROW_BLOCK + 1):
        pl.when(used & (n_blocks == nb))(functools.partial(body, slice(0, nb * EXPERT_ROW_BLOCK)))


def _gate_up_kernel(*refs, tn, up_offset):
    tabs = refs[:N_STEP_TABLES]
    xs_ref, w_hbm, b_ref, o_ref, wbuf, sems = refs[N_STEP_TABLES:]
    slot = _weight_prefetch(tabs, w_hbm, wbuf, sems, (0, up_offset), tn)
    half = xs_ref.shape[1]
    j = tabs[1][pl.program_id(0)]
    b_gate = b_ref[0, pl.ds(j, 1), :]
    b_up = b_ref[0, pl.ds(up_offset // tn + j, 1), :]

    def body(rows):
        lo, hi = _unpack_halves(xs_ref[rows, :])
        x_parts = (lo.astype(jnp.bfloat16), hi.astype(jnp.bfloat16))
        gate = jnp.minimum(_dot_f32_weight(x_parts, wbuf.at[slot, 0], half) + b_gate, SWIGLU_LIMIT)
        lin = jnp.clip(_dot_f32_weight(x_parts, wbuf.at[slot, 1], half) + b_up, -SWIGLU_LIMIT, SWIGLU_LIMIT)
        o_ref[rows, :] = (gate * jax.nn.sigmoid(SWIGLU_ALPHA * gate) * (lin + 1.0)).astype(o_ref.dtype)
        if rows.stop < EXPERT_TILE:
            o_ref[rows.stop:, :] = jnp.zeros((EXPERT_TILE - rows.stop, o_ref.shape[1]), o_ref.dtype)

    _run_tile_rows(tabs, body)


def _expert_gate_up(xs, w_gu, b_gu, counts, tiles_per_expert, tile_base, tn=512):
    n_slots, half = xs.shape
    E, D, F2 = w_gu.shape
    F = F2 // 2
    tn = min(tn, F)
    nj = F // tn
    n_tiles = n_slots // EXPERT_TILE
    tables = _expert_steps(counts, tiles_per_expert, tile_base, nj, n_tiles)

    grid_spec = pltpu.PrefetchScalarGridSpec(
        num_scalar_prefetch=N_STEP_TABLES,
        grid=(n_tiles * nj,),
        in_specs=[
            pl.BlockSpec((EXPERT_TILE, half), lambda s, e, j, t, *_: (t[s], 0)),
            pl.BlockSpec(memory_space=pl.ANY),
            pl.BlockSpec((1, F2 // tn, tn), lambda s, e, *_: (e[s], 0, 0)),
        ],
        out_specs=pl.BlockSpec((EXPERT_TILE, tn), lambda s, e, j, t, *_: (t[s], j[s])),
        scratch_shapes=[pltpu.VMEM((2, 2, D, tn), jnp.float32), pltpu.SemaphoreType.DMA((2,))],
    )
    return pl.pallas_call(
        functools.partial(_gate_up_kernel, tn=tn, up_offset=F),
        grid_spec=grid_spec,
        out_shape=jax.ShapeDtypeStruct((n_slots, F), jnp.bfloat16),
        compiler_params=_cparams("arbitrary"),
        name="expert_gate_up",
    )(*tables, xs, w_gu, b_gu.reshape(E, F2 // tn, tn))


def _down_kernel(*refs, tn, hi_offset):
    tabs = refs[:N_STEP_TABLES]
    a_ref, w_hbm, b_ref, o_ref, wbuf, sems = refs[N_STEP_TABLES:]
    slot = _weight_prefetch(tabs, w_hbm, wbuf, sems, (0, hi_offset), tn)
    k_rows = a_ref.shape[1]
    j = tabs[1][pl.program_id(0)]
    b_lo = b_ref[0, pl.ds(j, 1), :]
    b_hi = b_ref[0, pl.ds(hi_offset // tn + j, 1), :]

    def body(rows):
        a = (a_ref[rows, :],)
        y_lo = _dot_f32_weight(a, wbuf.at[slot, 0], k_rows) + b_lo
        y_hi = _dot_f32_weight(a, wbuf.at[slot, 1], k_rows) + b_hi
        o_ref[rows, :] = _pack_halves(y_lo, y_hi)
        if rows.stop < EXPERT_TILE:
            o_ref[rows.stop:, :] = jnp.zeros((EXPERT_TILE - rows.stop, o_ref.shape[1]), o_ref.dtype)

    _run_tile_rows(tabs, body)


def _expert_down(act, w_down, b_down, counts, tiles_per_expert, tile_base, tn=512):
    n_slots, F = act.shape
    E, _, D = w_down.shape
    half = D // 2
    tn = min(tn, half)
    nj = half // tn
    n_tiles = n_slots // EXPERT_TILE
    tables = _expert_steps(counts, tiles_per_expert, tile_base, nj, n_tiles)

    grid_spec = pltpu.PrefetchScalarGridSpec(
        num_scalar_prefetch=N_STEP_TABLES,
        grid=(n_tiles * nj,),
        in_specs=[
            pl.BlockSpec((EXPERT_TILE, F), lambda s, e, j, t, *_: (t[s], 0)),
            pl.BlockSpec(memory_space=pl.ANY),
            pl.BlockSpec((1, D // tn, tn), lambda s, e, *_: (e[s], 0, 0)),
        ],
        out_specs=pl.BlockSpec((EXPERT_TILE, tn), lambda s, e, j, t, *_: (t[s], j[s])),
        scratch_shapes=[pltpu.VMEM((2, 2, F, tn), jnp.float32), pltpu.SemaphoreType.DMA((2,))],
    )
    return pl.pallas_call(
        functools.partial(_down_kernel, tn=tn, hi_offset=half),
        grid_spec=grid_spec,
        out_shape=jax.ShapeDtypeStruct((n_slots, half), jnp.uint32),
        compiler_params=_cparams("arbitrary"),
        name="expert_down",
    )(*tables, act, w_down, b_down.reshape(E, D // tn, tn))


def _combine_kernel(pos_ref, pos_next_ref, x_ref, gate_ref, ys_ref, o_ref, buf, sems, *, tt):
    i = pl.program_id(0)
    slot = i % 2

    def row_copy(src_row, s, k, tok):
        return pltpu.make_async_copy(ys_ref.at[pl.ds(src_row, 1)], buf.at[s, k, pl.ds(tok, 1)], sems.at[s])

    def fetch(p_ref, s):
        def issue(tok, carry):
            for k in range(TOP_K):
                row_copy(p_ref[0, 0, tok * TOP_K + k], s, k, tok).start(priority=k % 2)
            return carry
        lax.fori_loop(0, tt, issue, 0, unroll=4)

    pl.when(i == 0)(lambda: fetch(pos_ref, 0))
    pl.when(i + 1 < pl.num_programs(0))(lambda: fetch(pos_next_ref, 1 - slot))

    def drain(tok, carry):
        for k in range(TOP_K):
            row_copy(0, slot, k, 0).wait()
        return carry

    lax.fori_loop(0, tt, drain, 0, unroll=4)

    half = buf.shape[3]
    acc_lo = x_ref[:, :half]
    acc_hi = x_ref[:, half:]
    for k in range(TOP_K):
        lo, hi = _unpack_halves(buf[slot, k])
        gk = gate_ref[:, k:k + 1]
        acc_lo = acc_lo + gk * lo
        acc_hi = acc_hi + gk * hi
    o_ref[:, :half] = acc_lo
    o_ref[:, half:] = acc_hi


def _combine(x1, gates, pos, ys, row0, n_rows, tt=COMBINE_TILE):
    T, D = x1.shape
    half = D // 2
    b0 = row0 // tt
    n_steps = n_rows // tt
    pos_blocks = pos.reshape(T // tt, 1, tt * TOP_K)
    grid_spec = pltpu.PrefetchScalarGridSpec(
        num_scalar_prefetch=0,
        grid=(n_steps,),
        in_specs=[
            pl.BlockSpec((1, 1, tt * TOP_K), lambda i: (b0 + i, 0, 0), memory_space=pltpu.SMEM),
            pl.BlockSpec((1, 1, tt * TOP_K), lambda i: (b0 + jnp.minimum(i + 1, n_steps - 1), 0, 0),
                         memory_space=pltpu.SMEM),
            pl.BlockSpec((tt, D), lambda i: (b0 + i, 0)),
            pl.BlockSpec((tt, TOP_K), lambda i: (b0 + i, 0)),
            pl.BlockSpec(memory_space=pl.ANY),
        ],
        out_specs=pl.BlockSpec((tt, D), lambda i: (i, 0)),
        scratch_shapes=[pltpu.VMEM((2, TOP_K, tt, half), jnp.uint32), pltpu.SemaphoreType.DMA((2,))],
    )
    return pl.pallas_call(
        functools.partial(_combine_kernel, tt=tt),
        grid_spec=grid_spec,
        out_shape=jax.ShapeDtypeStruct((n_rows, D), jnp.float32),
        compiler_params=_cparams("arbitrary"),
        name="moe_combine",
    )(pos_blocks, pos_blocks, x1, gates, ys)


def _sequence_tables(seq_lens, tile):
    lo, hi, start = [], [], 0
    for n in seq_lens:
        assert n % tile == 0
        lo += [start] * (n // tile)
        hi += [start + n] * (n // tile)
        start += n
    return jnp.asarray(np.array(lo, np.int32)), jnp.asarray(np.array(hi, np.int32))


def _rope_tables(seq_lens):
    half = HEAD_DIM // 2
    inv_freq = ROPE_THETA ** (-jnp.arange(half, dtype=jnp.float32) / half)
    pos = jnp.concatenate([jnp.arange(n, dtype=jnp.float32) for n in seq_lens])
    ang = pos[:, None] * inv_freq[None, :]
    cos, sin = jnp.cos(ang), jnp.sin(ang)
    return jnp.concatenate([cos, cos], axis=1), jnp.concatenate([-sin, sin], axis=1)


def _encoder_layer(xa, xb, seq_lens, norm_mix, w_in, q_norm, k_norm, w_pool, pool_scale, w_branch_a,
                   w_branch_b, w_out, norm_ffn, w_router, b_router, w_gate_up, b_gate_up, w_down, b_down):
    Ta, D = xa.shape
    Tb = xb.shape[0]
    T = Ta + Tb
    bf16 = jnp.bfloat16
    A_out = w_branch_a.shape[0]
    H = A_out // HEAD_DIM
    A = N_GROUPS * A_out
    P = w_branch_b.shape[0]
    E = w_router.shape[1]

    w_in_b = w_in.astype(bf16)
    gain_cols = jnp.concatenate([jnp.tile(q_norm, (1, H)).reshape(1, A),
                                 jnp.tile(k_norm, (1, H)).reshape(1, A)], axis=1)
    cos_t, sin_t = _rope_tables(seq_lens)

    h = _rmsnorm(xa, xb, norm_mix)
    qk = _qk_proj(h, w_in_b, gain_cols, cos_t, sin_t, 2 * A)
    v = _v_proj(h, w_in_b, 2 * A, A)
    u = _u_proj(h, w_in_b, 3 * A, P)

    lo_a, hi_a = _sequence_tables(seq_lens, ATTN_CHUNK)
    o_attn = _attention(qk, v, lo_a, hi_a, H)
    tp = 512
    lo_p, hi_p = _sequence_tables(seq_lens, tp)
    o_pool = _pool(u, w_pool.astype(bf16), pool_scale, lo_p, hi_p, tp)

    mix = _mix(h, w_in_b, 3 * A + P, o_attn, w_branch_a.astype(bf16), o_pool, w_branch_b.astype(bf16))
    x1 = _out_proj(mix, w_out.astype(bf16), xa, xb)

    hp, top_idx, gates = _router(x1, norm_ffn, w_router.astype(bf16), b_router)
    n_assign = T * TOP_K
    n_tiles = n_assign // EXPERT_TILE + E
    pos, pad_lo, pad_hi, counts, tiles_per_expert, tile_base = _routing_tables(top_idx, E)
    xs = _dispatch(hp, pos, pad_lo, pad_hi, n_tiles * EXPERT_TILE)
    act = _expert_gate_up(xs, w_gate_up, b_gate_up, counts, tiles_per_expert, tile_base)
    ys = _expert_down(act, w_down, b_down, counts, tiles_per_expert, tile_base)
    return _combine(x1, gates, pos, ys, 0, Ta), _combine(x1, gates, pos, ys, Ta, Tb)


def kernel(x_prompt, x_sample, norm_mix, w_in, q_norm, k_norm, w_pool, pool_scale, w_branch_a, w_branch_b,
           w_out, norm_ffn, w_router, b_router, w_gate_up, b_gate_up, w_down, b_down):
    depth = norm_mix.shape[0]
    D = x_prompt.shape[-1]
    seq_lens = (x_prompt.shape[1],) * x_prompt.shape[0] + (x_sample.shape[1],) * x_sample.shape[0]
    xa, xb = x_prompt.reshape(-1, D), x_sample.reshape(-1, D)
    for layer in range(depth):
        xa, xb = _encoder_layer(xa, xb, seq_lens, norm_mix[layer], w_in[layer], q_norm[layer], k_norm[layer],
                                w_pool[layer], pool_scale[layer], w_branch_a[layer], w_branch_b[layer],
                                w_out[layer], norm_ffn[layer], w_router[layer], b_router[layer],
                                w_gate_up[layer], b_gate_up[layer], w_down[layer], b_down[layer])
    return xa.reshape(x_prompt.shape), xb.reshape(x_sample.shape)
```

```python
import functools

import numpy as np
import jax
import jax.numpy as jnp
from jax import lax
from jax.experimental import pallas as pl
from jax.experimental.pallas import tpu as pltpu

HEAD_DIM = 128
DILATION_PAIRS = ((128, 1), (512, 4), (2048, 16))
N_GROUPS = len(DILATION_PAIRS)
ATTN_SIDE = 64
ROPE_THETA = 10000.0
POOL_WINDOWS = (2, 4, 8, 16)
POOL_HALO = 8
TOP_K = 4
SWIGLU_LIMIT = 7.0
SWIGLU_ALPHA = 1.702
RMS_EPS = 1e-6

VMEM_LIMIT_BYTES = 62 * 1024 * 1024
PROJ_ROWS = 1024
PROJ_COL_TILES = (1024, 512)
PROJ_ROW_SPLIT = 512
QK_ROW_SPLIT = 128

ATTN_CHUNK = 2048
ATTN_QBLK = 128
ATTN_WIN = 256
EXPERT_TILE = 1024
EXPERT_ROW_BLOCK = 256
WEIGHT_CAST_ROWS = 512
DISPATCH_TILE = 1024
COMBINE_TILE = 256


def _cparams(*sem):
    return pltpu.CompilerParams(dimension_semantics=sem, vmem_limit_bytes=VMEM_LIMIT_BYTES)


def _pack_halves(lo, hi):
    lo_bits = lax.bitcast_convert_type(lo.astype(jnp.bfloat16).astype(jnp.float32), jnp.uint32)
    hi_bits = lax.bitcast_convert_type(hi.astype(jnp.bfloat16).astype(jnp.float32), jnp.uint32)
    return (lo_bits >> 16) | (hi_bits & jnp.uint32(0xFFFF0000))


def _unpack_halves(packed):
    lo = lax.bitcast_convert_type(packed << 16, jnp.float32)
    hi = lax.bitcast_convert_type(packed & jnp.uint32(0xFFFF0000), jnp.float32)
    return lo, hi


def _rmsnorm_kernel(xa_ref, xb_ref, g_ref, o_ref, *, n_first):
    def body(x_ref):
        x = x_ref[...]
        ms = jnp.mean(x * x, axis=-1, keepdims=True)
        o_ref[...] = (x * lax.rsqrt(ms + RMS_EPS) * g_ref[...]).astype(o_ref.dtype)

    pl.when(pl.program_id(0) < n_first)(lambda: body(xa_ref))
    pl.when(pl.program_id(0) >= n_first)(lambda: body(xb_ref))


def _rmsnorm(xa, xb, gain, tm=512):
    D = xa.shape[1]
    T = xa.shape[0] + xb.shape[0]
    n_first = xa.shape[0] // tm
    return pl.pallas_call(
        functools.partial(_rmsnorm_kernel, n_first=n_first),
        grid=(T // tm,),
        in_specs=[pl.BlockSpec((tm, D), lambda i: (jnp.minimum(i, n_first - 1), 0)),
                  pl.BlockSpec((tm, D), lambda i: (jnp.maximum(i - n_first, 0), 0)),
                  pl.BlockSpec((1, D), lambda i: (0, 0))],
        out_specs=pl.BlockSpec((tm, D), lambda i: (i, 0)),
        out_shape=jax.ShapeDtypeStruct((T, D), jnp.bfloat16),
        compiler_params=_cparams("arbitrary"),
        name="rmsnorm",
    )(xa, xb, gain.reshape(1, D))


def _qk_proj_kernel(h_ref, w_ref, gain_ref, cos_ref, sin_ref, o_ref):
    w = w_ref[...]
    tm = h_ref.shape[0]
    for r0 in range(0, tm, QK_ROW_SPLIT):
        rows = slice(r0, r0 + QK_ROW_SPLIT)
        acc = jnp.dot(h_ref[rows, :], w, preferred_element_type=jnp.float32)
        cos = cos_ref[rows, :]
        sin = sin_ref[rows, :]
        for c in range(acc.shape[1] // HEAD_DIM):
            t = acc[:, c * HEAD_DIM:(c + 1) * HEAD_DIM]
            ms = jnp.mean(t * t, axis=-1, keepdims=True)
            t = t * lax.rsqrt(ms + RMS_EPS) * gain_ref[:, c * HEAD_DIM:(c + 1) * HEAD_DIM]
            o_ref[c, rows, :] = t * cos + pltpu.roll(t, HEAD_DIM // 2, axis=1) * sin


def _col_tile(ncols, col0=0):
    return next(t for t in PROJ_COL_TILES if ncols % t == 0 and col0 % t == 0)


def _qk_proj(h, w_in, gain_cols, cos_t, sin_t, ncols, tm=PROJ_ROWS):
    T, D = h.shape
    tn = _col_tile(ncols)
    n_heads = ncols // HEAD_DIM
    return pl.pallas_call(
        _qk_proj_kernel,
        grid=(T // tm, ncols // tn),
        in_specs=[
            pl.BlockSpec((tm, D), lambda i, j: (i, 0)),
            pl.BlockSpec((D, tn), lambda i, j: (0, j)),
            pl.BlockSpec((1, tn), lambda i, j: (0, j)),
            pl.BlockSpec((tm, HEAD_DIM), lambda i, j: (i, 0)),
            pl.BlockSpec((tm, HEAD_DIM), lambda i, j: (i, 0)),
        ],
        out_specs=pl.BlockSpec((tn // HEAD_DIM, tm, HEAD_DIM), lambda i, j: (j, i, 0)),
        out_shape=jax.ShapeDtypeStruct((n_heads, T, HEAD_DIM), jnp.float32),
        compiler_params=_cparams("arbitrary", "arbitrary"),
        name="qk_proj",
    )(h, w_in, gain_cols, cos_t, sin_t)


def _v_proj_kernel(h_ref, w_ref, o_ref):
    w = w_ref[...]
    for r0 in range(0, h_ref.shape[0], PROJ_ROW_SPLIT):
        rows = slice(r0, r0 + PROJ_ROW_SPLIT)
        acc = jnp.dot(h_ref[rows, :], w, preferred_element_type=jnp.float32)
        for c in range(acc.shape[1] // HEAD_DIM):
            o_ref[c, rows, :] = acc[:, c * HEAD_DIM:(c + 1) * HEAD_DIM]


def _v_proj(h, w_in, col0, ncols, tm=PROJ_ROWS):
    T, D = h.shape
    tn = _col_tile(ncols, col0)
    cb0 = col0 // tn
    return pl.pallas_call(
        _v_proj_kernel,
        grid=(T // tm, ncols // tn),
        in_specs=[
            pl.BlockSpec((tm, D), lambda i, j: (i, 0)),
            pl.BlockSpec((D, tn), lambda i, j: (0, cb0 + j)),
        ],
        out_specs=pl.BlockSpec((tn // HEAD_DIM, tm, HEAD_DIM), lambda i, j: (j, i, 0)),
        out_shape=jax.ShapeDtypeStruct((ncols // HEAD_DIM, T, HEAD_DIM), jnp.float32),
        compiler_params=_cparams("arbitrary", "arbitrary"),
        name="v_proj",
    )(h, w_in)


def _u_proj_kernel(h_ref, w_ref, o_ref):
    w = w_ref[...]
    for r0 in range(0, h_ref.shape[0], PROJ_ROW_SPLIT):
        rows = slice(r0, r0 + PROJ_ROW_SPLIT)
        o_ref[rows, :] = jnp.dot(h_ref[rows, :], w, preferred_element_type=jnp.float32)


def _u_proj(h, w_in, col0, ncols, tm=PROJ_ROWS):
    T, D = h.shape
    tn = _col_tile(ncols, col0)
    cb0 = col0 // tn
    return pl.pallas_call(
        _u_proj_kernel,
        grid=(T // tm, ncols // tn),
        in_specs=[
            pl.BlockSpec((tm, D), lambda i, j: (i, 0)),
            pl.BlockSpec((D, tn), lambda i, j: (0, cb0 + j)),
        ],
        out_specs=pl.BlockSpec((tm, tn), lambda i, j: (i, j)),
        out_shape=jax.ShapeDtypeStruct((T, ncols), jnp.float32),
        compiler_params=_cparams("arbitrary", "arbitrary"),
        name="u_proj",
    )(h, w_in)


def _attn_kernel(lo_ref, hi_ref, *refs, chunk):
    in_refs = refs[:7 * N_GROUPS]
    o_ref = refs[7 * N_GROUPS]
    o_sc, lse_sc = refs[7 * N_GROUPS + 1:]
    c = pl.program_id(0)
    t0 = c * chunk
    seq_lo = lo_ref[c]
    seq_hi = hi_ref[c]
    win = ATTN_WIN
    slack = win - ATTN_QBLK - 2 * ATTN_SIDE
    qq = lax.broadcasted_iota(jnp.int32, (ATTN_QBLK, win), 0)
    kk = lax.broadcasted_iota(jnp.int32, (ATTN_QBLK, win), 1)
    band_bias = jnp.where(jnp.abs(kk - ATTN_SIDE - qq) <= ATTN_SIDE, 0.0, -jnp.inf)
    k_row = lax.broadcasted_iota(jnp.int32, (1, win), 1)
    scale = HEAD_DIM ** -0.5

    for g, (_, d) in enumerate(DILATION_PAIRS):
        q_ref, kc_ref, kl_ref, kr_ref, vc_ref, vl_ref, vr_ref = in_refs[7 * g:7 * g + 7]
        shift = d.bit_length() - 1
        rows = chunk // d
        m0 = t0 >> shift
        m_lo = seq_lo >> shift
        m_hi = seq_hi >> shift
        for r in range(d):
            def sub(ref, n, r=r, d=d):
                return ref[0, pl.ds(r, n, stride=d), :]
            q_r = sub(q_ref, rows).astype(jnp.bfloat16)
            def padded(l_ref, c_ref, r_ref):
                parts = [sub(l_ref, ATTN_SIDE), sub(c_ref, rows), sub(r_ref, ATTN_SIDE)]
                if slack:
                    parts.append(sub(r_ref, slack))
                return jnp.concatenate(parts, axis=0).astype(jnp.bfloat16)
            k_r = padded(kl_ref, kc_ref, kr_ref)
            v_r = padded(vl_ref, vc_ref, vr_ref)
            for b in range(rows // ATTN_QBLK):
                qs = q_r[b * ATTN_QBLK:(b + 1) * ATTN_QBLK]
                ks = k_r[b * ATTN_QBLK:b * ATTN_QBLK + win]
                vs = v_r[b * ATTN_QBLK:b * ATTN_QBLK + win]
                key_m = k_row + (m0 + (b * ATTN_QBLK - ATTN_SIDE))
                ends_bias = jnp.where((key_m >= m_lo) & (key_m < m_hi), 0.0, -jnp.inf)
                s = lax.dot_general(qs, ks, (((1,), (1,)), ((), ())),
                                    preferred_element_type=jnp.float32) * scale + band_bias + ends_bias
                m = jnp.max(s, axis=-1, keepdims=True)
                p = jnp.exp(s - m)
                denom = jnp.sum(p, axis=-1, keepdims=True)
                o = jnp.dot(p.astype(jnp.bfloat16), vs, preferred_element_type=jnp.float32) / denom
                lse = m + jnp.log(denom)
                dst = pl.ds(b * ATTN_QBLK * d + r, ATTN_QBLK, stride=d)
                o_sc[g, dst, :] = o
                lse_sc[g, dst, :] = jnp.broadcast_to(lse, (ATTN_QBLK, HEAD_DIM))

    lses = [lse_sc[g] for g in range(N_GROUPS)]
    top = functools.reduce(jnp.maximum, lses)
    ws = [jnp.exp(l - top) for l in lses]
    num = sum(w * o_sc[g] for g, w in enumerate(ws))
    o_ref[...] = (num / sum(ws)).astype(o_ref.dtype)


def _attention(qk, v, seq_lo, seq_hi, heads_per_group):
    n_qk, T, _ = qk.shape
    H = heads_per_group
    k_base = n_qk // 2
    chunk = ATTN_CHUNK
    n_chunks = T // chunk
    in_specs = []
    operands = []
    for g, (_, d) in enumerate(DILATION_PAIRS):
        halo = ATTN_SIDE * d
        per = chunk // halo
        n_halo = T // halo

        def center(base, g=g):
            return pl.BlockSpec((1, chunk, HEAD_DIM), lambda c, h, lo, hi: (base + g * H + h, c, 0))

        def left(base, g=g, per=per):
            return pl.BlockSpec((1, halo, HEAD_DIM),
                                lambda c, h, lo, hi: (base + g * H + h, jnp.maximum(c * per - 1, 0), 0))

        def right(base, g=g, per=per, n_halo=n_halo):
            return pl.BlockSpec((1, halo, HEAD_DIM),
                                lambda c, h, lo, hi: (base + g * H + h, jnp.minimum((c + 1) * per, n_halo - 1), 0))

        in_specs += [center(0), center(k_base), left(k_base), right(k_base), center(0), left(0), right(0)]
        operands += [qk, qk, qk, qk, v, v, v]
    grid_spec = pltpu.PrefetchScalarGridSpec(
        num_scalar_prefetch=2,
        grid=(n_chunks, H),
        in_specs=in_specs,
        out_specs=pl.BlockSpec((chunk, HEAD_DIM), lambda c, h, lo, hi: (c, h)),
        scratch_shapes=[pltpu.VMEM((N_GROUPS, chunk, HEAD_DIM), jnp.float32),
                        pltpu.VMEM((N_GROUPS, chunk, HEAD_DIM), jnp.float32)],
    )
    return pl.pallas_call(
        functools.partial(_attn_kernel, chunk=chunk),
        grid_spec=grid_spec,
        out_shape=jax.ShapeDtypeStruct((T, H * HEAD_DIM), jnp.bfloat16),
        compiler_params=_cparams("arbitrary", "arbitrary"),
        name="dilated_attention",
    )(seq_lo, seq_hi, *operands)


def _pool_kernel(lo_ref, hi_ref, uc_ref, ul_ref, ur_ref, w_ref, scale_ref, o_ref, ext_sc, *, tp):
    i = pl.program_id(0)
    t0 = i * tp
    seq_lo = lo_ref[i]
    seq_hi = hi_ref[i]
    gw = w_ref.shape[1]
    pos_l = t0 - POOL_HALO + lax.broadcasted_iota(jnp.int32, (POOL_HALO, 1), 0)
    pos_r = t0 + tp + lax.broadcasted_iota(jnp.int32, (POOL_HALO, 1), 0)
    ext_sc[0:POOL_HALO, :] = jnp.where(pos_l >= seq_lo, ul_ref[...], 0.0)
    ext_sc[POOL_HALO:POOL_HALO + tp, :] = uc_ref[...]
    ext_sc[POOL_HALO + tp:, :] = jnp.where(pos_r < seq_hi, ur_ref[...], 0.0)
    pos = t0 + lax.broadcasted_iota(jnp.int32, (tp, 1), 0)
    for gi, w in enumerate(POOL_WINDOWS):
        cols = slice(gi * gw, (gi + 1) * gw)
        total = None
        for j in range(-(w // 2), w - w // 2):
            piece = ext_sc[POOL_HALO + j:POOL_HALO + j + tp, cols]
            total = piece if total is None else total + piece
        cnt = jnp.minimum(pos + (w - w // 2), seq_hi) - jnp.maximum(pos - w // 2, seq_lo)
        pooled = total / cnt.astype(jnp.float32) - uc_ref[:, cols]
        mixed = jnp.dot(pooled.astype(jnp.bfloat16), w_ref[gi], preferred_element_type=jnp.float32)
        o_ref[:, cols] = (mixed * scale_ref[:, cols]).astype(o_ref.dtype)


def _pool(u, w_pool, pool_scale, seq_lo, seq_hi, tp):
    T, P = u.shape
    per = tp // POOL_HALO
    n_halo = T // POOL_HALO
    grid_spec = pltpu.PrefetchScalarGridSpec(
        num_scalar_prefetch=2,
        grid=(T // tp,),
        in_specs=[
            pl.BlockSpec((tp, P), lambda i, lo, hi: (i, 0)),
            pl.BlockSpec((POOL_HALO, P), lambda i, lo, hi: (jnp.maximum(i * per - 1, 0), 0)),
            pl.BlockSpec((POOL_HALO, P), lambda i, lo, hi: (jnp.minimum((i + 1) * per, n_halo - 1), 0)),
            pl.BlockSpec(w_pool.shape, lambda i, lo, hi: (0, 0, 0)),
            pl.BlockSpec((1, P), lambda i, lo, hi: (0, 0)),
        ],
        out_specs=pl.BlockSpec((tp, P), lambda i, lo, hi: (i, 0)),
        scratch_shapes=[pltpu.VMEM((tp + 2 * POOL_HALO, P), jnp.float32)],
    )
    return pl.pallas_call(
        functools.partial(_pool_kernel, tp=tp),
        grid_spec=grid_spec,
        out_shape=jax.ShapeDtypeStruct((T, P), jnp.bfloat16),
        compiler_params=_cparams("arbitrary"),
        name="multiscale_pool",
    )(seq_lo, seq_hi, u, u, u, w_pool, pool_scale.reshape(1, P))


def _mix_kernel(h_ref, wga_ref, wgb_ref, oa_ref, wa_ref, ob_ref, wb_ref, o_ref):
    for r0 in range(0, h_ref.shape[0], PROJ_ROW_SPLIT):
        rows = slice(r0, r0 + PROJ_ROW_SPLIT)
        h = h_ref[rows, :]
        g_a = jnp.dot(h, wga_ref[...], preferred_element_type=jnp.float32)
        g_b = jnp.dot(h, wgb_ref[...], preferred_element_type=jnp.float32)
        a = jnp.dot(oa_ref[rows, :], wa_ref[...], preferred_element_type=jnp.float32)
        b = jnp.dot(ob_ref[rows, :], wb_ref[...], preferred_element_type=jnp.float32)
        o_ref[rows, :] = (jax.nn.sigmoid(g_a) * a + jax.nn.sigmoid(g_b) * b).astype(o_ref.dtype)


def _mix(h, w_in, ga_col0, o_attn, w_a, o_pool, w_b, tm=PROJ_ROWS, tn=512):
    T, D = h.shape
    cb_a = ga_col0 // tn
    cb_b = (ga_col0 + D) // tn
    A = o_attn.shape[1]
    P = o_pool.shape[1]
    return pl.pallas_call(
        _mix_kernel,
        grid=(T // tm, D // tn),
        in_specs=[
            pl.BlockSpec((tm, D), lambda i, j: (i, 0)),
            pl.BlockSpec((D, tn), lambda i, j: (0, cb_a + j)),
            pl.BlockSpec((D, tn), lambda i, j: (0, cb_b + j)),
            pl.BlockSpec((tm, A), lambda i, j: (i, 0)),
            pl.BlockSpec((A, tn), lambda i, j: (0, j)),
            pl.BlockSpec((tm, P), lambda i, j: (i, 0)),
            pl.BlockSpec((P, tn), lambda i, j: (0, j)),
        ],
        out_specs=pl.BlockSpec((tm, tn), lambda i, j: (i, j)),
        out_shape=jax.ShapeDtypeStruct((T, D), jnp.bfloat16),
        compiler_params=_cparams("arbitrary", "arbitrary"),
        name="gated_mix",
    )(h, w_in, w_in, o_attn, w_a, o_pool, w_b)


def _out_proj_kernel(m_ref, w_ref, xa_ref, xb_ref, o_ref, *, n_first):
    i = pl.program_id(0)

    def body(x_ref):
        w = w_ref[...]
        half = m_ref.shape[0] // 2
        for rows in (slice(0, half), slice(half, 2 * half)):
            o_ref[rows, :] = x_ref[rows, :] + jnp.dot(m_ref[rows, :], w, preferred_element_type=jnp.float32)

    pl.when(i < n_first)(lambda: body(xa_ref))
    pl.when(i >= n_first)(lambda: body(xb_ref))


def _out_proj(mix, w_out, xa, xb, tm=PROJ_ROWS, tn=1024):
    T, D = mix.shape
    tn = min(tn, D)
    nj = D // tn
    n_first = xa.shape[0] // tm
    return pl.pallas_call(
        functools.partial(_out_proj_kernel, n_first=n_first),
        grid=(T // tm, nj),
        in_specs=[
            pl.BlockSpec((tm, D), lambda i, j: (i, 0), pipeline_mode=pl.Buffered(1)),
            pl.BlockSpec((D, tn), lambda i, j: (0, j)),
            pl.BlockSpec((tm, tn), lambda i, j: (jnp.minimum(i, n_first - 1), jnp.where(i < n_first, j, nj - 1))),
            pl.BlockSpec((tm, tn), lambda i, j: (jnp.maximum(i - n_first, 0), jnp.where(i >= n_first, j, 0))),
        ],
        out_specs=pl.BlockSpec((tm, tn), lambda i, j: (i, j)),
        out_shape=jax.ShapeDtypeStruct((T, D), jnp.float32),
        compiler_params=_cparams("arbitrary", "arbitrary"),
        name="out_proj_residual",
    )(mix, w_out, xa, xb)


def _router_kernel(x_ref, g_ref, wr_ref, br_ref, hp_ref, idx_ref, gate_ref):
    x = x_ref[...]
    ms = jnp.mean(x * x, axis=-1, keepdims=True)
    h = x * lax.rsqrt(ms + RMS_EPS) * g_ref[...]
    half = h.shape[1] // 2
    hp_ref[...] = _pack_halves(h[:, :half], h[:, half:])
    logits = jnp.dot(h.astype(jnp.bfloat16), wr_ref[...], preferred_element_type=jnp.float32) + br_ref[...]
    n_exp = logits.shape[1]
    lane = lax.broadcasted_iota(jnp.int32, logits.shape, 1)
    vals, idxs = [], []
    for _ in range(TOP_K):
        m = jnp.max(logits, axis=-1, keepdims=True)
        idx = jnp.min(jnp.where(logits == m, lane, n_exp), axis=-1, keepdims=True)
        vals.append(m)
        idxs.append(idx)
        logits = jnp.where(lane == idx, -jnp.inf, logits)
    exps = [jnp.exp(v - vals[0]) for v in vals]
    total = sum(exps)
    for k in range(TOP_K):
        idx_ref[:, k:k + 1] = idxs[k]
        gate_ref[:, k:k + 1] = exps[k] / total


def _router(x1, gain, w_router, b_router, tr=512):
    T, D = x1.shape
    E = w_router.shape[1]
    return pl.pallas_call(
        _router_kernel,
        grid=(T // tr,),
        in_specs=[
            pl.BlockSpec((tr, D), lambda i: (i, 0)),
            pl.BlockSpec((1, D), lambda i: (0, 0)),
            pl.BlockSpec((D, E), lambda i: (0, 0)),
            pl.BlockSpec((1, E), lambda i: (0, 0)),
        ],
        out_specs=[
            pl.BlockSpec((tr, D // 2), lambda i: (i, 0)),
            pl.BlockSpec((tr, TOP_K), lambda i: (i, 0)),
            pl.BlockSpec((tr, TOP_K), lambda i: (i, 0)),
        ],
        out_shape=[
            jax.ShapeDtypeStruct((T, D // 2), jnp.uint32),
            jax.ShapeDtypeStruct((T, TOP_K), jnp.int32),
            jax.ShapeDtypeStruct((T, TOP_K), jnp.float32),
        ],
        compiler_params=_cparams("arbitrary"),
        name="router_topk",
    )(x1, gain.reshape(1, D), w_router, b_router.reshape(1, E))


def _routing_tables(top_idx, n_experts):
    e = top_idx.reshape(-1)
    onehot = (e[:, None] == jnp.arange(n_experts, dtype=jnp.int32)[None, :]).astype(jnp.int32)
    csum = jnp.cumsum(onehot, axis=0)
    counts = csum[-1]
    tiles_per_expert = (counts + EXPERT_TILE - 1) // EXPERT_TILE
    padded = tiles_per_expert * EXPERT_TILE
    pend = jnp.cumsum(padded)
    pstart = pend - padded
    pos = jnp.sum(onehot * (csum - 1 + pstart[None, :]), axis=1).astype(jnp.int32)
    pad_lo = (pstart + counts).astype(jnp.int32)
    pad_hi = pstart + (counts + EXPERT_ROW_BLOCK - 1) // EXPERT_ROW_BLOCK * EXPERT_ROW_BLOCK
    return (pos, pad_lo, pad_hi.astype(jnp.int32), counts.astype(jnp.int32), tiles_per_expert.astype(jnp.int32),
            (pstart // EXPERT_TILE).astype(jnp.int32))


def _dispatch_kernel(pos_ref, pad_lo_ref, pad_hi_ref, hp_ref, xs_ref, zero_sc, sem, *, n_experts):
    step = pl.program_id(0)
    tt = hp_ref.shape[0]

    def row_copy(tok, dst_row):
        return pltpu.make_async_copy(hp_ref.at[pl.ds(tok, 1)], xs_ref.at[pl.ds(dst_row, 1)], sem)

    def issue(tok, carry):
        for k in range(TOP_K):
            row_copy(tok, pos_ref[0, 0, tok * TOP_K + k]).start(priority=k % 2)
        return carry

    lax.fori_loop(0, tt, issue, 0, unroll=4)

    def drain(tok, carry):
        for k in range(TOP_K):
            row_copy(0, 0).wait()
        return carry

    lax.fori_loop(0, tt, drain, 0, unroll=4)

    @pl.when(step == pl.num_programs(0) - 1)
    def _():
        zero_sc[...] = jnp.zeros_like(zero_sc)

        def pad_copy(dst_row):
            return pltpu.make_async_copy(zero_sc.at[pl.ds(0, 1)], xs_ref.at[pl.ds(dst_row, 1)], sem)

        for e in range(n_experts):
            lo = pad_lo_ref[e]
            hi = pad_hi_ref[e]

            def fill(p, carry):
                pad_copy(p).start()
                return carry

            lax.fori_loop(lo, hi, fill, 0)

            def fill_wait(p, carry):
                pad_copy(0).wait()
                return carry

            lax.fori_loop(lo, hi, fill_wait, 0)


def _dispatch(hp, pos, pad_lo, pad_hi, n_slots, tt=DISPATCH_TILE):
    T, W = hp.shape
    n_experts = pad_lo.shape[0]
    grid_spec = pltpu.PrefetchScalarGridSpec(
        num_scalar_prefetch=0,
        grid=(T // tt,),
        in_specs=[
            pl.BlockSpec((1, 1, tt * TOP_K), lambda s: (s, 0, 0), memory_space=pltpu.SMEM),
            pl.BlockSpec(memory_space=pltpu.SMEM),
            pl.BlockSpec(memory_space=pltpu.SMEM),
            pl.BlockSpec((tt, W), lambda s: (s, 0)),
        ],
        out_specs=pl.BlockSpec(memory_space=pl.ANY),
        scratch_shapes=[pltpu.VMEM((8, W), jnp.uint32), pltpu.SemaphoreType.DMA(())],
    )
    return pl.pallas_call(
        functools.partial(_dispatch_kernel, n_experts=n_experts),
        grid_spec=grid_spec,
        out_shape=jax.ShapeDtypeStruct((n_slots, W), jnp.uint32),
        compiler_params=_cparams("arbitrary"),
        name="moe_dispatch",
    )(pos.reshape(T // tt, 1, tt * TOP_K), pad_lo, pad_hi, hp)


def _expert_steps(counts, tiles_per_expert, tile_base, nj, n_tiles):
    n_steps = n_tiles * nj
    steps_e = tiles_per_expert * nj
    send = jnp.cumsum(steps_e)
    total = send[-1]
    step = jnp.arange(n_steps, dtype=jnp.int32)
    s = jnp.minimum(step, total - 1)
    e = jnp.sum((send[None, :] <= s[:, None]).astype(jnp.int32), axis=1)
    local = s - (send - steps_e)[e]
    nt = jnp.maximum(tiles_per_expert[e], 1)
    j = local // nt
    t_local = local - j * nt
    tile = tile_base[e] + t_local
    used = step < total
    first = used & (t_local == 0)
    slot = (jnp.cumsum(first.astype(jnp.int32)) - 1) % 2
    s_next = s - t_local + nt
    has_next = first & (s_next < total)
    s_next = jnp.minimum(s_next, total - 1)
    real_rows = jnp.clip(counts[e] - t_local * EXPERT_TILE, 1, EXPERT_TILE)
    row_blocks = (real_rows + EXPERT_ROW_BLOCK - 1) // EXPERT_ROW_BLOCK
    i32 = lambda a: a.astype(jnp.int32)
    return (i32(e), i32(j), i32(tile), i32(used), i32(first), i32(slot), i32(e[s_next]), i32(j[s_next]),
            i32(has_next), i32(row_blocks))


def _dot_f32_weight(x_parts, w_ref, part_rows):
    acc = None
    for p, x in enumerate(x_parts):
        for k0 in range(0, part_rows, WEIGHT_CAST_ROWS):
            w = w_ref[p * part_rows + k0:p * part_rows + k0 + WEIGHT_CAST_ROWS, :].astype(jnp.bfloat16)
            d = jnp.dot(x[:, k0:k0 + WEIGHT_CAST_ROWS], w, preferred_element_type=jnp.float32)
            acc = d if acc is None else acc + d
    return acc


def _weight_prefetch(tabs, w_hbm, wbuf, sems, col_offsets, tn):
    e_ref, j_ref, _, _, first_ref, slot_ref, ne_ref, nj_ref, has_next_ref, _ = tabs
    s = pl.program_id(0)
    slot = slot_ref[s]

    def copies(e, j, sl):
        return [pltpu.make_async_copy(w_hbm.at[e, :, pl.ds(pl.multiple_of(off + j * tn, tn), tn)],
                                      wbuf.at[sl, m], sems.at[sl])
                for m, off in enumerate(col_offsets)]

    @pl.when(s == 0)
    def _():
        for c in copies(e_ref[0], j_ref[0], 0):
            c.start(priority=1)

    @pl.when(first_ref[s] == 1)
    def _():
        for c in copies(e_ref[s], j_ref[s], slot):
            c.wait()

        @pl.when(has_next_ref[s] == 1)
        def _():
            for c in copies(ne_ref[s], nj_ref[s], 1 - slot):
                c.start(priority=1)

    return slot


N_STEP_TABLES = 10


def _run_tile_rows(tabs, body):
    s = pl.program_id(0)
    used = tabs[3][s] == 1
    n_blocks = tabs[9][s]
    for nb in range(1, EXPERT_TILE // EXPERT_ROW_BLOCK + 1):
        pl.when(used & (n_blocks == nb))(functools.partial(body, slice(0, nb * EXPERT_ROW_BLOCK)))


def _gate_up_kernel(*refs, tn, up_offset):
    tabs = refs[:N_STEP_TABLES]
    xs_ref, w_hbm, b_ref, o_ref, wbuf, sems = refs[N_STEP_TABLES:]
    slot = _weight_prefetch(tabs, w_hbm, wbuf, sems, (0, up_offset), tn)
    half = xs_ref.shape[1]
    j = tabs[1][pl.program_id(0)]
    b_gate = b_ref[0, pl.ds(j, 1), :]
    b_up = b_ref[0, pl.ds(up_offset // tn + j, 1), :]

    def body(rows):
        lo, hi = _unpack_halves(xs_ref[rows, :])
        x_parts = (lo.astype(jnp.bfloat16), hi.astype(jnp.bfloat16))
        gate = jnp.minimum(_dot_f32_weight(x_parts, wbuf.at[slot, 0], half) + b_gate, SWIGLU_LIMIT)
        lin = jnp.clip(_dot_f32_weight(x_parts, wbuf.at[slot, 1], half) + b_up, -SWIGLU_LIMIT, SWIGLU_LIMIT)
        o_ref[rows, :] = (gate * jax.nn.sigmoid(SWIGLU_ALPHA * gate) * (lin + 1.0)).astype(o_ref.dtype)
        if rows.stop < EXPERT_TILE:
            o_ref[rows.stop:, :] = jnp.zeros((EXPERT_TILE - rows.stop, o_ref.shape[1]), o_ref.dtype)

    _run_tile_rows(tabs, body)


def _expert_gate_up(xs, w_gu, b_gu, counts, tiles_per_expert, tile_base, tn=512):
    n_slots, half = xs.shape
    E, D, F2 = w_gu.shape
    F = F2 // 2
    tn = min(tn, F)
    nj = F // tn
    n_tiles = n_slots // EXPERT_TILE
    tables = _expert_steps(counts, tiles_per_expert, tile_base, nj, n_tiles)

    grid_spec = pltpu.PrefetchScalarGridSpec(
        num_scalar_prefetch=N_STEP_TABLES,
        grid=(n_tiles * nj,),
        in_specs=[
            pl.BlockSpec((EXPERT_TILE, half), lambda s, e, j, t, *_: (t[s], 0)),
            pl.BlockSpec(memory_space=pl.ANY),
            pl.BlockSpec((1, F2 // tn, tn), lambda s, e, *_: (e[s], 0, 0)),
        ],
        out_specs=pl.BlockSpec((EXPERT_TILE, tn), lambda s, e, j, t, *_: (t[s], j[s])),
        scratch_shapes=[pltpu.VMEM((2, 2, D, tn), jnp.float32), pltpu.SemaphoreType.DMA((2,))],
    )
    return pl.pallas_call(
        functools.partial(_gate_up_kernel, tn=tn, up_offset=F),
        grid_spec=grid_spec,
        out_shape=jax.ShapeDtypeStruct((n_slots, F), jnp.bfloat16),
        compiler_params=_cparams("arbitrary"),
        name="expert_gate_up",
    )(*tables, xs, w_gu, b_gu.reshape(E, F2 // tn, tn))


def _down_kernel(*refs, tn, hi_offset):
    tabs = refs[:N_STEP_TABLES]
    a_ref, w_hbm, b_ref, o_ref, wbuf, sems = refs[N_STEP_TABLES:]
    slot = _weight_prefetch(tabs, w_hbm, wbuf, sems, (0, hi_offset), tn)
    k_rows = a_ref.shape[1]
    j = tabs[1][pl.program_id(0)]
    b_lo = b_ref[0, pl.ds(j, 1), :]
    b_hi = b_ref[0, pl.ds(hi_offset // tn + j, 1), :]

    def body(rows):
        a = (a_ref[rows, :],)
        y_lo = _dot_f32_weight(a, wbuf.at[slot, 0], k_rows) + b_lo
        y_hi = _dot_f32_weight(a, wbuf.at[slot, 1], k_rows) + b_hi
        o_ref[rows, :] = _pack_halves(y_lo, y_hi)
        if rows.stop < EXPERT_TILE:
            o_ref[rows.stop:, :] = jnp.zeros((EXPERT_TILE - rows.stop, o_ref.shape[1]), o_ref.dtype)

    _run_tile_rows(tabs, body)


def _expert_down(act, w_down, b_down, counts, tiles_per_expert, tile_base, tn=512):
    n_slots, F = act.shape
    E, _, D = w_down.shape
    half = D // 2
    tn = min(tn, half)
    nj = half // tn
    n_tiles = n_slots // EXPERT_TILE
    tables = _expert_steps(counts, tiles_per_expert, tile_base, nj, n_tiles)

    grid_spec = pltpu.PrefetchScalarGridSpec(
        num_scalar_prefetch=N_STEP_TABLES,
        grid=(n_tiles * nj,),
        in_specs=[
            pl.BlockSpec((EXPERT_TILE, F), lambda s, e, j, t, *_: (t[s], 0)),
            pl.BlockSpec(memory_space=pl.ANY),
            pl.BlockSpec((1, D // tn, tn), lambda s, e, *_: (e[s], 0, 0)),
        ],
        out_specs=pl.BlockSpec((EXPERT_TILE, tn), lambda s, e, j, t, *_: (t[s], j[s])),
        scratch_shapes=[pltpu.VMEM((2, 2, F, tn), jnp.float32), pltpu.SemaphoreType.DMA((2,))],
    )
    return pl.pallas_call(
        functools.partial(_down_kernel, tn=tn, hi_offset=half),
        grid_spec=grid_spec,
        out_shape=jax.ShapeDtypeStruct((n_slots, half), jnp.uint32),
        compiler_params=_cparams("arbitrary"),
        name="expert_down",
    )(*tables, act, w_down, b_down.reshape(E, D // tn, tn))


def _combine_kernel(pos_ref, pos_next_ref, x_ref, gate_ref, ys_ref, o_ref, buf, sems, *, tt):
    i = pl.program_id(0)
    slot = i % 2

    def row_copy(src_row, s, k, tok):
        return pltpu.make_async_copy(ys_ref.at[pl.ds(src_row, 1)], buf.at[s, k, pl.ds(tok, 1)], sems.at[s])

    def fetch(p_ref, s):
        def issue(tok, carry):
            for k in range(TOP_K):
                row_copy(p_ref[0, 0, tok * TOP_K + k], s, k, tok).start(priority=k % 2)
            return carry
        lax.fori_loop(0, tt, issue, 0, unroll=4)

    pl.when(i == 0)(lambda: fetch(pos_ref, 0))
    pl.when(i + 1 < pl.num_programs(0))(lambda: fetch(pos_next_ref, 1 - slot))

    def drain(tok, carry):
        for k in range(TOP_K):
            row_copy(0, slot, k, 0).wait()
        return carry

    lax.fori_loop(0, tt, drain, 0, unroll=4)

    half = buf.shape[3]
    acc_lo = x_ref[:, :half]
    acc_hi = x_ref[:, half:]
    for k in range(TOP_K):
        lo, hi = _unpack_halves(buf[slot, k])
        gk = gate_ref[:, k:k + 1]
        acc_lo = acc_lo + gk * lo
        acc_hi = acc_hi + gk * hi
    o_ref[:, :half] = acc_lo
    o_ref[:, half:] = acc_hi


def _combine(x1, gates, pos, ys, row0, n_rows, tt=COMBINE_TILE):
    T, D = x1.shape
    half = D // 2
    b0 = row0 // tt
    n_steps = n_rows // tt
    pos_blocks = pos.reshape(T // tt, 1, tt * TOP_K)
    grid_spec = pltpu.PrefetchScalarGridSpec(
        num_scalar_prefetch=0,
        grid=(n_steps,),
        in_specs=[
            pl.BlockSpec((1, 1, tt * TOP_K), lambda i: (b0 + i, 0, 0), memory_space=pltpu.SMEM),
            pl.BlockSpec((1, 1, tt * TOP_K), lambda i: (b0 + jnp.minimum(i + 1, n_steps - 1), 0, 0),
                         memory_space=pltpu.SMEM),
            pl.BlockSpec((tt, D), lambda i: (b0 + i, 0)),
            pl.BlockSpec((tt, TOP_K), lambda i: (b0 + i, 0)),
            pl.BlockSpec(memory_space=pl.ANY),
        ],
        out_specs=pl.BlockSpec((tt, D), lambda i: (i, 0)),
        scratch_shapes=[pltpu.VMEM((2, TOP_K, tt, half), jnp.uint32), pltpu.SemaphoreType.DMA((2,))],
    )
    return pl.pallas_call(
        functools.partial(_combine_kernel, tt=tt),
        grid_spec=grid_spec,
        out_shape=jax.ShapeDtypeStruct((n_rows, D), jnp.float32),
        compiler_params=_cparams("arbitrary"),
        name="moe_combine",
    )(pos_blocks, pos_blocks, x1, gates, ys)


def _sequence_tables(seq_lens, tile):
    lo, hi, start = [], [], 0
    for n in seq_lens:
        assert n % tile == 0
        lo += [start] * (n // tile)
        hi += [start + n] * (n // tile)
        start += n
    return jnp.asarray(np.array(lo, np.int32)), jnp.asarray(np.array(hi, np.int32))


def _rope_tables(seq_lens):
    half = HEAD_DIM // 2
    inv_freq = ROPE_THETA ** (-jnp.arange(half, dtype=jnp.float32) / half)
    pos = jnp.concatenate([jnp.arange(n, dtype=jnp.float32) for n in seq_lens])
    ang = pos[:, None] * inv_freq[None, :]
    cos, sin = jnp.cos(ang), jnp.sin(ang)
    return jnp.concatenate([cos, cos], axis=1), jnp.concatenate([-sin, sin], axis=1)


def _encoder_layer(xa, xb, seq_lens, norm_mix, w_in, q_norm, k_norm, w_pool, pool_scale, w_branch_a,
                   w_branch_b, w_out, norm_ffn, w_router, b_router, w_gate_up, b_gate_up, w_down, b_down):
    Ta, D = xa.shape
    Tb = xb.shape[0]
    T = Ta + Tb
    bf16 = jnp.bfloat16
    A_out = w_branch_a.shape[0]
    H = A_out // HEAD_DIM
    A = N_GROUPS * A_out
    P = w_branch_b.shape[0]
    E = w_router.shape[1]

    w_in_b = w_in.astype(bf16)
    gain_cols = jnp.concatenate([jnp.tile(q_norm, (1, H)).reshape(1, A),
                                 jnp.tile(k_norm, (1, H)).reshape(1, A)], axis=1)
    cos_t, sin_t = _rope_tables(seq_lens)

    h = _rmsnorm(xa, xb, norm_mix)
    qk = _qk_proj(h, w_in_b, gain_cols, cos_t, sin_t, 2 * A)
    v = _v_proj(h, w_in_b, 2 * A, A)
    u = _u_proj(h, w_in_b, 3 * A, P)

    lo_a, hi_a = _sequence_tables(seq_lens, ATTN_CHUNK)
    o_attn = _attention(qk, v, lo_a, hi_a, H)
    tp = 512
    lo_p, hi_p = _sequence_tables(seq_lens, tp)
    o_pool = _pool(u, w_pool.astype(bf16), pool_scale, lo_p, hi_p, tp)

    mix = _mix(h, w_in_b, 3 * A + P, o_attn, w_branch_a.astype(bf16), o_pool, w_branch_b.astype(bf16))
    x1 = _out_proj(mix, w_out.astype(bf16), xa, xb)

    hp, top_idx, gates = _router(x1, norm_ffn, w_router.astype(bf16), b_router)
    n_assign = T * TOP_K
    n_tiles = n_assign // EXPERT_TILE + E
    pos, pad_lo, pad_hi, counts, tiles_per_expert, tile_base = _routing_tables(top_idx, E)
    xs = _dispatch(hp, pos, pad_lo, pad_hi, n_tiles * EXPERT_TILE)
    act = _expert_gate_up(xs, w_gate_up, b_gate_up, counts, tiles_per_expert, tile_base)
    ys = _expert_down(act, w_down, b_down, counts, tiles_per_expert, tile_base)
    return _combine(x1, gates, pos, ys, 0, Ta), _combine(x1, gates, pos, ys, Ta, Tb)


def kernel(x_prompt, x_sample, norm_mix, w_in, q_norm, k_norm, w_pool, pool_scale, w_branch_a, w_branch_b,
           w_out, norm_ffn, w_router, b_router, w_gate_up, b_gate_up, w_down, b_down):
    depth = norm_mix.shape[0]
    D = x_prompt.shape[-1]
    seq_lens = (x_prompt.shape[1],) * x_prompt.shape[0] + (x_sample.shape[1],) * x_sample.shape[0]
    xa, xb = x_prompt.reshape(-1, D), x_sample.reshape(-1, D)
    for layer in range(depth):
        xa, xb = _encoder_layer(xa, xb, seq_lens, norm_mix[layer], w_in[layer], q_norm[layer], k_norm[layer],
                                w_pool[layer], pool_scale[layer], w_branch_a[layer], w_branch_b[layer],
                                w_out[layer], norm_ffn[layer], w_router[layer], b_router[layer],
                                w_gate_up[layer], b_gate_up[layer], w_down[layer], b_down[layer])
    return xa.reshape(x_prompt.shape), xb.reshape(x_sample.shape)
```
